```python
import math
import jax, jax.numpy as jnp
from jax import lax
import numpy as np

D_MODEL = 1024
BATCH = 8
SEQ = 2048
DEPTH = 2
DEC_BATCH = 128
DEC_SEQ = 4
PAST_LEN = 16384
PAGE_SIZE = 128

S5_WIDTH = D_MODEL // 2
S5_GROUP = 16
S5_GROUPS = S5_WIDTH // S5_GROUP
S5_STATE = 64
SSD_INNER = D_MODEL
SSD_HEAD_DIM = 64
SSD_HEADS = SSD_INNER // SSD_HEAD_DIM
SSD_GROUPS = 4
SSD_STATE = 128
SSD_CONV = 4
SSD_CONV_DIM = SSD_INNER + 2 * SSD_GROUPS * SSD_STATE
SSD_CHUNK = 128
HG_WIDTH = D_MODEL // 2
HG_EXPAND = 128
HG_HEADS = HG_WIDTH // HG_EXPAND
HG_KDIM = HG_EXPAND
HG_VDIM = HG_WIDTH // HG_HEADS
HG_CHUNK = 64
D_FF = ((8 * D_MODEL // 3 + 127) // 128) * 128
FFN_CONV = 3
N_BRANCH = 3
EPS = 1e-6
IN_COLS = S5_WIDTH + SSD_INNER + SSD_CONV_DIM + SSD_HEADS + 4 * HG_WIDTH + N_BRANCH * D_MODEL

kernel_name = 'hybrid_s5_ssd_hgrn2_gated_step'


def rms_norm(x, w):
    xf = x.astype(jnp.float32)
    y = xf * lax.rsqrt(jnp.mean(xf * xf, axis=-1, keepdims=True) + EPS)
    return (y * w.astype(jnp.float32)).astype(x.dtype)


def pad_time(t, n):
    return jnp.pad(t, [(0, 0), (0, n)] + [(0, 0)] * (t.ndim - 2))


def causal_dwconv(h, prev, w, b):
    L = h.shape[1]
    K = w.shape[0]
    cat = jnp.concatenate([prev.astype(h.dtype), h], axis=1)
    y = cat[:, 0:L] * w[0]
    for j in range(1, K):
        y = y + cat[:, j:j + L] * w[j]
    return y + b, cat[:, L:]


def s5_mixer(u, h0_re, h0_im, log_dt, lam_re, lam_im, b_re, b_im, c_re, c_im, d):
    f32 = jnp.float32
    bsz, L, _ = u.shape
    uf = u.astype(f32)
    ug = uf.reshape(bsz, L, S5_GROUPS, S5_GROUP)
    delta = jnp.exp(log_dt.astype(f32))[:, None]
    lr = lam_re.astype(f32)
    li = lam_im.astype(f32)
    mag = jnp.exp(lr * delta)
    ab_re = mag * jnp.cos(li * delta)
    ab_im = mag * jnp.sin(li * delta)
    den = lr * lr + li * li
    nr = ab_re - 1.0
    co_re = (nr * lr + ab_im * li) / den
    co_im = (ab_im * lr - nr * li) / den
    br = b_re.astype(f32)
    bi = b_im.astype(f32)
    bb_re = co_re[..., None] * br - co_im[..., None] * bi
    bb_im = co_re[..., None] * bi + co_im[..., None] * br
    bu_re = jnp.einsum('gnj,blgj->blgn', bb_re, ug)
    bu_im = jnp.einsum('gnj,blgj->blgn', bb_im, ug)
    a_re = jnp.broadcast_to(ab_re, bu_re.shape)
    a_im = jnp.broadcast_to(ab_im, bu_im.shape)

    def combine(e1, e2):
        a1r, a1i, b1r, b1i = e1
        a2r, a2i, b2r, b2i = e2
        return (a1r * a2r - a1i * a2i, a1r * a2i + a1i * a2r,
                a2r * b1r - a2i * b1i + b2r, a2r * b1i + a2i * b1r + b2i)

    _, _, xr, xi = lax.associative_scan(combine, (a_re, a_im, bu_re, bu_im), axis=1)
    t = jnp.arange(1, L + 1, dtype=f32)[:, None, None]
    pm = jnp.exp(lr * delta * t)
    ph = li * delta * t
    p_re = pm * jnp.cos(ph)
    p_im = pm * jnp.sin(ph)
    h0r = h0_re.astype(f32)[:, None]
    h0i = h0_im.astype(f32)[:, None]
    xr = xr + p_re * h0r - p_im * h0i
    xi = xi + p_re * h0i + p_im * h0r
    y = (jnp.einsum('gjn,blgn->blgj', c_re.astype(f32), xr)
         - jnp.einsum('gjn,blgn->blgj', c_im.astype(f32), xi))
    y = y.reshape(bsz, L, S5_WIDTH) + d.astype(f32) * uf
    return y.astype(u.dtype), xr[:, -1], xi[:, -1]


def ssd_scan(x, dt, a, bm, cm, s0):
    bsz, L = x.shape[:2]
    Q = min(SSD_CHUNK, L)
    nc = -(-L // Q)
    pad = nc * Q - L
    R = SSD_HEADS // SSD_GROUPS
    xc = pad_time(x, pad).reshape(bsz, nc, Q, SSD_GROUPS, R, SSD_HEAD_DIM)
    dtc = pad_time(dt, pad).reshape(bsz, nc, Q, SSD_GROUPS, R)
    bc = pad_time(bm, pad).reshape(bsz, nc, Q, SSD_GROUPS, SSD_STATE)
    cc = pad_time(cm, pad).reshape(bsz, nc, Q, SSD_GROUPS, SSD_STATE)
    acum = jnp.cumsum(dtc * a.reshape(SSD_GROUPS, R), axis=2)
    mask = jnp.tril(jnp.ones((Q, Q), dtype=bool))[:, :, None, None]
    seg = acum[:, :, :, None] - acum[:, :, None]
    decay = jnp.exp(jnp.where(mask, seg, -jnp.inf))
    cb = jnp.einsum('bcqgn,bcsgn->bcqsg', cc, bc)
    xdt = xc * dtc[..., None]
    y_diag = jnp.einsum('bcqsgr,bcsgrp->bcqgrp', cb[..., None] * decay, xdt)
    last = acum[:, :, -1:]
    st = jnp.einsum('bcsgn,bcsgrp->bcgrpn', bc, xdt * jnp.exp(last - acum)[..., None])

    def step(s, inp):
        dec, add = inp
        return dec[..., None, None] * s + add, s

    s_fin, s_prev = lax.scan(step, s0.reshape(bsz, SSD_GROUPS, R, SSD_HEAD_DIM, SSD_STATE),
                             (jnp.moveaxis(jnp.exp(last[:, :, 0]), 1, 0), jnp.moveaxis(st, 1, 0)))
    s_prev = jnp.moveaxis(s_prev, 0, 1)
    y_off = jnp.einsum('bcqgn,bcgrpn->bcqgrp', cc, s_prev) * jnp.exp(acum)[..., None]
    y = (y_diag + y_off).reshape(bsz, nc * Q, SSD_HEADS, SSD_HEAD_DIM)[:, :L]
    return y, s_fin.reshape(bsz, SSD_HEADS, SSD_HEAD_DIM, SSD_STATE)


def hgrn2_scan(q, logf, k, v, s0):
    bsz, L = q.shape[:2]
    Q = min(HG_CHUNK, L)
    nc = -(-L // Q)
    pad = nc * Q - L

    def to_chunks(t):
        t = pad_time(t, pad)
        return jnp.moveaxis(t.reshape((bsz, nc, Q) + t.shape[2:]), 1, 0)

    mask = jnp.tril(jnp.ones((Q, Q), dtype=bool))[None, :, :, None, None]

    def step(s, inp):
        qb, fb, kb, vb = inp
        bcum = jnp.cumsum(fb, axis=1)
        seg = bcum[:, :, None] - bcum[:, None]
        dec = jnp.exp(jnp.where(mask, seg, -jnp.inf))
        att = jnp.einsum('bqhk,bqshk,bshk->bhqs', qb, dec, kb)
        o = (jnp.einsum('bhqs,bshv->bqhv', att, vb)
             + jnp.einsum('bqhk,bhkv->bqhv', qb * jnp.exp(bcum), s))
        blast = bcum[:, -1]
        kd = kb * jnp.exp(blast[:, None] - bcum)
        s_new = jnp.exp(blast)[..., None] * s + jnp.einsum('bshk,bshv->bhkv', kd, vb)
        return s_new, o

    s_fin, o = lax.scan(step, s0, (to_chunks(q), to_chunks(logf), to_chunks(k), to_chunks(v)))
    o = jnp.moveaxis(o, 0, 1).reshape(bsz, nc * Q, HG_HEADS, HG_VDIM)[:, :L]
    return o, s_fin


def layer(x, st, p, lb):
    s5_re0, s5_im0, ssd0, ssd_conv0, hg0, ffn_conv0 = st
    f32 = jnp.float32
    bsz, L, _ = x.shape
    h = rms_norm(x, p['norm1_w'])
    proj = h @ p['w_in']
    widths = (S5_WIDTH, SSD_INNER, SSD_CONV_DIM, SSD_HEADS, HG_WIDTH, HG_WIDTH, HG_WIDTH, HG_WIDTH)
    cuts = [sum(widths[:i + 1]) for i in range(len(widths))]
    u_a, z_b, xbc, dt_raw, q_c, f_c, i_c, g_c, gate_logits = jnp.split(proj, cuts, axis=-1)

    y_a, s5_re1, s5_im1 = s5_mixer(u_a, s5_re0, s5_im0, p['s5_log_dt'], p['s5_lambda_re'], p['s5_lambda_im'],
                                   p['s5_b_re'], p['s5_b_im'], p['s5_c_re'], p['s5_c_im'], p['s5_d'])
    glu = jax.nn.gelu(y_a) @ p['s5_w_glu']
    out_a = glu[..., :D_MODEL] * jax.nn.sigmoid(glu[..., D_MODEL:])

    xbc_c, ssd_conv1 = causal_dwconv(xbc, ssd_conv0, p['ssd_conv_w'], p['ssd_conv_b'])
    xbc_c = jax.nn.silu(xbc_c)
    nb = SSD_GROUPS * SSD_STATE
    xs = xbc_c[..., :SSD_INNER].reshape(bsz, L, SSD_HEADS, SSD_HEAD_DIM).astype(f32)
    bm = xbc_c[..., SSD_INNER:SSD_INNER + nb].reshape(bsz, L, SSD_GROUPS, SSD_STATE).astype(f32)
    cm = xbc_c[..., SSD_INNER + nb:].reshape(bsz, L, SSD_GROUPS, SSD_STATE).astype(f32)
    dt = jax.nn.softplus(dt_raw.astype(f32) + p['ssd_dt_bias'].astype(f32))
    a = -jnp.exp(p['ssd_a_log'].astype(f32))
    y_b, ssd1 = ssd_scan(xs, dt, a, bm, cm, ssd0.astype(f32))
    y_b = y_b + p['ssd_d'].astype(f32)[:, None] * xs
    y_b = y_b.reshape(bsz, L, SSD_INNER).astype(x.dtype) * jax.nn.silu(z_b)
    out_b = rms_norm(y_b, p['ssd_norm_w']) @ p['ssd_w_out']

    zf = f_c.astype(f32).reshape(bsz, L, HG_HEADS, HG_KDIM)
    lbh = lb.reshape(HG_HEADS, HG_KDIM)
    logf = jnp.logaddexp(jnp.log(lbh), jnp.log1p(-lbh) + jax.nn.log_sigmoid(zf))
    k_c = (1.0 - lbh) * jax.nn.sigmoid(-zf)
    q = q_c.astype(f32).reshape(bsz, L, HG_HEADS, HG_KDIM)
    v = i_c.astype(f32).reshape(bsz, L, HG_HEADS, HG_VDIM)
    o_c, hg1 = hgrn2_scan(q, logf, k_c, v, hg0.astype(f32))
    o_c = rms_norm(o_c, p['hg_norm_w'].reshape(HG_HEADS, HG_VDIM)).reshape(bsz, L, HG_WIDTH).astype(x.dtype)
    out_c = (o_c * jax.nn.silu(g_c)) @ p['hg_w_out']

    gts = jax.nn.sigmoid(gate_logits.reshape(bsz, L, N_BRANCH, D_MODEL))
    mixed = gts[..., 0, :] * out_a + gts[..., 1, :] * out_b + gts[..., 2, :] * out_c
    x = x + mixed @ p['w_o']

    up = rms_norm(x, p['norm2_w']) @ p['ffn_w_up']
    a_ff = up[..., :D_FF]
    g_ff = up[..., D_FF:]
    a_conv, ffn_conv1 = causal_dwconv(a_ff, ffn_conv0, p['ffn_conv_w'], p['ffn_conv_b'])
    x = x + (jax.nn.silu(a_conv) * g_ff) @ p['ffn_w_down']
    dt_ = x.dtype
    new = (s5_re1.astype(dt_), s5_im1.astype(dt_), ssd1.astype(dt_), ssd_conv1.astype(dt_),
           hg1.astype(dt_), ffn_conv1.astype(dt_))
    return x, new


def setup_inputs(seed: int = 0) -> dict:
    key = jax.random.key(seed)
    ks = iter(jax.random.split(key, 48))
    f32 = jnp.float32

    def nrm(shape, scale):
        return jax.random.normal(next(ks), shape, f32) * scale

    def unif(shape, lo, hi):
        return jax.random.uniform(next(ks), shape, f32, lo, hi)

    x_prompt = nrm((BATCH, SEQ, D_MODEL), 1.0)
    x_sample = nrm((DEC_BATCH, DEC_SEQ, D_MODEL), 1.0)
    state_s5_re = nrm((DEPTH, DEC_BATCH, S5_GROUPS, S5_STATE), 0.1)
    state_s5_im = nrm((DEPTH, DEC_BATCH, S5_GROUPS, S5_STATE), 0.1)
    state_ssd = nrm((DEPTH, DEC_BATCH, SSD_HEADS, SSD_HEAD_DIM, SSD_STATE), 0.1)
    state_ssd_conv = nrm((DEPTH, DEC_BATCH, SSD_CONV - 1, SSD_CONV_DIM), 1.0)
    state_hgrn = nrm((DEPTH, DEC_BATCH, HG_HEADS, HG_KDIM, HG_VDIM), 0.5)
    state_ffn_conv = nrm((DEPTH, DEC_BATCH, FFN_CONV - 1, D_FF), 1.0)

    norm1_w = 1.0 + nrm((DEPTH, D_MODEL), 0.1)
    w_in = nrm((DEPTH, D_MODEL, IN_COLS), D_MODEL ** -0.5)
    s5_log_dt = unif((DEPTH, S5_GROUPS), math.log(1e-3), math.log(1e-1))
    s5_lambda_re = -0.5 + nrm((DEPTH, S5_GROUPS, S5_STATE), 0.01)
    s5_lambda_im = jnp.pi * jnp.arange(S5_STATE, dtype=f32) + nrm((DEPTH, S5_GROUPS, S5_STATE), 0.01)
    s5_b_re = nrm((DEPTH, S5_GROUPS, S5_STATE, S5_GROUP), (2 * S5_GROUP) ** -0.5)
    s5_b_im = nrm((DEPTH, S5_GROUPS, S5_STATE, S5_GROUP), (2 * S5_GROUP) ** -0.5)
    s5_c_re = nrm((DEPTH, S5_GROUPS, S5_GROUP, S5_STATE), S5_STATE ** -0.5)
    s5_c_im = nrm((DEPTH, S5_GROUPS, S5_GROUP, S5_STATE), S5_STATE ** -0.5)
    s5_d = nrm((DEPTH, S5_WIDTH), 1.0)
    s5_w_glu = nrm((DEPTH, S5_WIDTH, 2 * D_MODEL), S5_WIDTH ** -0.5)
    ssd_conv_w = nrm((DEPTH, SSD_CONV, SSD_CONV_DIM), SSD_CONV ** -0.5)
    ssd_conv_b = nrm((DEPTH, SSD_CONV_DIM), 0.01)
    dt0 = jnp.exp(unif((DEPTH, SSD_HEADS), math.log(1e-3), math.log(1e-1)))
    ssd_dt_bias = dt0 + jnp.log(-jnp.expm1(-dt0))
    ssd_a_log = jnp.log(unif((DEPTH, SSD_HEADS), 1.0, 16.0))
    ssd_d = 1.0 + nrm((DEPTH, SSD_HEADS), 0.1)
    ssd_norm_w = 1.0 + nrm((DEPTH, SSD_INNER), 0.1)
    ssd_w_out = nrm((DEPTH, SSD_INNER, D_MODEL), SSD_INNER ** -0.5)
    hg_lb_logits = nrm((DEPTH, HG_WIDTH), 1.0)
    hg_norm_w = 1.0 + nrm((DEPTH, HG_WIDTH), 0.1)
    hg_w_out = nrm((DEPTH, HG_WIDTH, D_MODEL), HG_WIDTH ** -0.5)
    w_o = nrm((DEPTH, D_MODEL, D_MODEL), D_MODEL ** -0.5)
    norm2_w = 1.0 + nrm((DEPTH, D_MODEL), 0.1)
    ffn_w_up = nrm((DEPTH, D_MODEL, 2 * D_FF), D_MODEL ** -0.5)
    ffn_conv_w = nrm((DEPTH, FFN_CONV, D_FF), FFN_CONV ** -0.5)
    ffn_conv_b = nrm((DEPTH, D_FF), 0.01)
    ffn_w_down = nrm((DEPTH, D_FF, D_MODEL), D_FF ** -0.5)
    norm_f_w = 1.0 + nrm((D_MODEL,), 0.1)
    return {
        'x_prompt': x_prompt, 'x_sample': x_sample,
        'state_s5_re': state_s5_re, 'state_s5_im': state_s5_im, 'state_ssd': state_ssd,
        'state_ssd_conv': state_ssd_conv, 'state_hgrn': state_hgrn, 'state_ffn_conv': state_ffn_conv,
        'norm1_w': norm1_w, 'w_in': w_in,
        's5_log_dt': s5_log_dt, 's5_lambda_re': s5_lambda_re, 's5_lambda_im': s5_lambda_im,
        's5_b_re': s5_b_re, 's5_b_im': s5_b_im, 's5_c_re': s5_c_re, 's5_c_im': s5_c_im,
        's5_d': s5_d, 's5_w_glu': s5_w_glu,
        'ssd_conv_w': ssd_conv_w, 'ssd_conv_b': ssd_conv_b, 'ssd_dt_bias': ssd_dt_bias,
        'ssd_a_log': ssd_a_log, 'ssd_d': ssd_d, 'ssd_norm_w': ssd_norm_w, 'ssd_w_out': ssd_w_out,
        'hg_lb_logits': hg_lb_logits, 'hg_norm_w': hg_norm_w, 'hg_w_out': hg_w_out,
        'w_o': w_o, 'norm2_w': norm2_w,
        'ffn_w_up': ffn_w_up, 'ffn_conv_w': ffn_conv_w, 'ffn_conv_b': ffn_conv_b, 'ffn_w_down': ffn_w_down,
        'norm_f_w': norm_f_w,
    }


def reference(x_prompt, x_sample, state_s5_re, state_s5_im, state_ssd, state_ssd_conv, state_hgrn,
              state_ffn_conv, norm1_w, w_in, s5_log_dt, s5_lambda_re, s5_lambda_im, s5_b_re, s5_b_im,
              s5_c_re, s5_c_im, s5_d, s5_w_glu, ssd_conv_w, ssd_conv_b, ssd_dt_bias, ssd_a_log, ssd_d,
              ssd_norm_w, ssd_w_out, hg_lb_logits, hg_norm_w, hg_w_out, w_o, norm2_w, ffn_w_up,
              ffn_conv_w, ffn_conv_b, ffn_w_down, norm_f_w):
    f32 = jnp.float32
    lb_cum = jnp.cumsum(jax.nn.softmax(hg_lb_logits.astype(f32), axis=0), axis=0)
    lb_all = lb_cum - lb_cum[0:1]
    bp = x_prompt.shape[0]
    dtp = x_prompt.dtype
    zeros_p = (jnp.zeros((bp, S5_GROUPS, S5_STATE), dtp), jnp.zeros((bp, S5_GROUPS, S5_STATE), dtp),
               jnp.zeros((bp, SSD_HEADS, SSD_HEAD_DIM, SSD_STATE), dtp),
               jnp.zeros((bp, SSD_CONV - 1, SSD_CONV_DIM), dtp),
               jnp.zeros((bp, HG_HEADS, HG_KDIM, HG_VDIM), dtp),
               jnp.zeros((bp, FFN_CONV - 1, D_FF), dtp))
    xp = x_prompt
    xs = x_sample
    new_p = []
    new_s = []
    for l in range(DEPTH):
        p = {'norm1_w': norm1_w[l], 'w_in': w_in[l], 's5_log_dt': s5_log_dt[l],
             's5_lambda_re': s5_lambda_re[l], 's5_lambda_im': s5_lambda_im[l],
             's5_b_re': s5_b_re[l], 's5_b_im': s5_b_im[l], 's5_c_re': s5_c_re[l], 's5_c_im': s5_c_im[l],
             's5_d': s5_d[l], 's5_w_glu': s5_w_glu[l],
             'ssd_conv_w': ssd_conv_w[l], 'ssd_conv_b': ssd_conv_b[l], 'ssd_dt_bias': ssd_dt_bias[l],
             'ssd_a_log': ssd_a_log[l], 'ssd_d': ssd_d[l], 'ssd_norm_w': ssd_norm_w[l],
             'ssd_w_out': ssd_w_out[l], 'hg_norm_w': hg_norm_w[l], 'hg_w_out': hg_w_out[l],
             'w_o': w_o[l], 'norm2_w': norm2_w[l], 'ffn_w_up': ffn_w_up[l],
             'ffn_conv_w': ffn_conv_w[l], 'ffn_conv_b': ffn_conv_b[l], 'ffn_w_down': ffn_w_down[l]}
        xp, sp = layer(xp, zeros_p, p, lb_all[l])
        st_s = (state_s5_re[l], state_s5_im[l], state_ssd[l], state_ssd_conv[l], state_hgrn[l],
                state_ffn_conv[l])
        xs, ss = layer(xs, st_s, p, lb_all[l])
        new_p.append(sp)
        new_s.append(ss)

    def stk(lst, i):
        return jnp.stack([s[i] for s in lst])

    y_prompt = rms_norm(xp, norm_f_w)
    y_sample = rms_norm(xs, norm_f_w)
    return (y_prompt, y_sample,
            stk(new_p, 0), stk(new_p, 1), stk(new_p, 2), stk(new_p, 3), stk(new_p, 4), stk(new_p, 5),
            stk(new_s, 0), stk(new_s, 1), stk(new_s, 2), stk(new_s, 3), stk(new_s, 4), stk(new_s, 5))
```

```python
import functools
import math

import jax
import jax.numpy as jnp
import numpy as np
from jax import lax
from jax.experimental import pallas as pl
from jax.experimental.pallas import tpu as pltpu

F32 = jnp.float32
BF16 = jnp.bfloat16

D_MODEL = 1024
DEPTH = 2
S5_WIDTH = 512
S5_GROUP = 16
S5_GROUPS = 32
S5_STATE = 64
S5_LANES = S5_GROUPS * S5_STATE
SSD_INNER = 1024
SSD_HEAD_DIM = 64
SSD_HEADS = 16
SSD_GROUPS = 4
SSD_STATE = 128
SSD_CONV = 4
SSD_CONV_DIM = 2048
HG_WIDTH = 512
HG_HEADS = 4
HG_KDIM = 128
HG_VDIM = 128
D_FF = 2816
FFN_CONV = 3
EPS = 1e-6

LANE = 128
SUBLANE = 8
SAMPLE_PAD = 8
HG_SAFE_LOG = -80.0

PRE_TM = 256
S5_T = 256
S5_CH = 8
S5_R = S5_T // S5_CH
S5_LC = 512
MIX_T = 128
SSD_Q = 128
HG_Q = 64
MIX_NB_SAMPLE = 8
VMEM_LIMIT = 56 * 1024 * 1024


def _const_spec(shape):
    nd = len(shape)
    return pl.BlockSpec(shape, lambda *_: (0,) * nd, pipeline_mode=pl.Buffered(1))


def _params(sem):
    return pltpu.CompilerParams(dimension_semantics=sem, vmem_limit_bytes=VMEM_LIMIT)


def _bdot(a, b):
    return jnp.dot(a.astype(BF16), b.astype(BF16), preferred_element_type=F32)


def _bdot_nt(a, b):
    return lax.dot_general(a.astype(BF16), b.astype(BF16), (((1,), (1,)), ((), ())),
                           preferred_element_type=F32)


def _bdot_tn(a, b):
    return lax.dot_general(a.astype(BF16), b.astype(BF16), (((0,), (0,)), ((), ())),
                           preferred_element_type=F32)


def _split3(x):
    h = x.astype(BF16)
    r = x - h.astype(F32)
    m = r.astype(BF16)
    l = (r - m.astype(F32)).astype(BF16)
    return h, m, l


def _sel_left(m01, x):
    h, m, l = _split3(x)
    d = lambda p: jnp.dot(m01, p, preferred_element_type=F32)
    return (d(h) + d(m)) + d(l)


def _sel_right(x, m01):
    h, m, l = _split3(x)
    d = lambda p: jnp.dot(p, m01, preferred_element_type=F32)
    return (d(h) + d(m)) + d(l)


def _sigmoid(x):
    return 1.0 / (1.0 + jnp.exp(-x))


def _softplus(x):
    return jnp.maximum(x, 0.0) + jnp.log1p(jnp.exp(-jnp.abs(x)))


def _rms(x, w):
    return x * lax.rsqrt(jnp.mean(x * x, axis=-1, keepdims=True) + EPS) * w


def _pre_kernel(x_ref, n1_ref, w_ua, w_z, w_xbc, w_dt, w_q, w_f, w_i, w_g, w_gt,
                dtb_ref, loglb_ref, log1mlb_ref, omlb_ref,
                ua_o, sz_o, xbc_o, dt_o, q_o, lf_o, kk_o, v_o, sg_o, gt_o):
    hb = _rms(x_ref[...], n1_ref[...]).astype(BF16)
    dot = lambda w: jnp.dot(hb, w[...], preferred_element_type=F32)
    ua_o[...] = dot(w_ua)
    z = dot(w_z)
    sz_o[...] = z * _sigmoid(z)
    xbc_o[...] = dot(w_xbc)
    dt_o[...] = _softplus(dot(w_dt) + dtb_ref[...])
    q_o[...] = dot(w_q)
    zf = dot(w_f)
    log_sig = jnp.minimum(zf, 0.0) - jnp.log1p(jnp.exp(-jnp.abs(zf)))
    a = loglb_ref[...]
    b = log1mlb_ref[...] + log_sig
    lf_o[...] = jnp.maximum(a, b) + jnp.log1p(jnp.exp(-jnp.abs(a - b)))
    kk_o[...] = omlb_ref[...] * _sigmoid(-zf)
    v_o[...] = dot(w_i)
    g = dot(w_g)
    sg_o[...] = g * _sigmoid(g)
    gt_o[...] = _sigmoid(dot(w_gt))


def _pre_call(x2d, lp):
    rows = x2d.shape[0]
    tm = PRE_TM
    row = lambda c: pl.BlockSpec((tm, c), lambda i: (i, 0))
    widths = (S5_WIDTH, SSD_INNER, SSD_CONV_DIM, LANE, HG_WIDTH, HG_WIDTH, HG_WIDTH, HG_WIDTH,
              HG_WIDTH, 3 * D_MODEL)
    consts = (lp['n1'], lp['w_ua'], lp['w_z'], lp['w_xbc'], lp['w_dt'], lp['w_q'], lp['w_f'],
              lp['w_i'], lp['w_g'], lp['w_gt'], lp['dtb'], lp['loglb'], lp['log1mlb'], lp['omlb'])
    return pl.pallas_call(
        _pre_kernel,
        grid=(rows // tm,),
        in_specs=[row(D_MODEL)] + [_const_spec(c.shape) for c in consts],
        out_specs=[row(c) for c in widths],
        out_shape=[jax.ShapeDtypeStruct((rows, c), F32) for c in widths],
        compiler_params=_params(("parallel",)),
        name="pre_proj",
    )(x2d, *consts)


def _cmul_add(ar, ai, xr, xi, br, bi):
    return ar * xr - ai * xi + br, ar * xi + ai * xr + bi


def _s5_prompt_kernel(ua_ref, perm_ref, permt_ref, bb_ref, cc_ref, apr_ref, api_ref, d_ref,
                      ya_ref, sre_ref, sim_ref, bu_ref, xb_ref):
    @pl.when(pl.program_id(1) == 0)
    def _():
        sre_ref[...] = jnp.zeros_like(sre_ref)
        sim_ref[...] = jnp.zeros_like(sim_ref)

    ua = ua_ref[...]
    up = jnp.dot(perm_ref[...], ua.astype(BF16), preferred_element_type=F32).astype(BF16)
    bu_ref[...] = jnp.dot(up, bb_ref[...], preferred_element_type=F32)
    rowid = lax.broadcasted_iota(jnp.int32, (S5_CH, S5_LC), 0)

    for lc in range(S5_LANES // S5_LC):
        re = slice(lc * S5_LC, (lc + 1) * S5_LC)
        im = slice(S5_LANES + lc * S5_LC, S5_LANES + (lc + 1) * S5_LC)
        bcast = lambda ref, k: jnp.broadcast_to(ref[k:k + 1, re], (S5_CH, S5_LC))
        a_r, a_i = bcast(apr_ref, 0), bcast(api_ref, 0)
        sr = jnp.zeros((S5_CH, S5_LC), F32)
        si = jnp.zeros((S5_CH, S5_LC), F32)
        for r in range(S5_R):
            rows = slice(r * S5_CH, (r + 1) * S5_CH)
            sr, si = _cmul_add(a_r, a_i, sr, si, bu_ref[rows, re], bu_ref[rows, im])
            bu_ref[rows, re] = sr
            bu_ref[rows, im] = si
        pr = jnp.broadcast_to(sre_ref[0, :, re], (S5_CH, S5_LC))
        pi = jnp.broadcast_to(sim_ref[0, :, re], (S5_CH, S5_LC))
        vr = jnp.where(rowid == 0, pr, pltpu.roll(sr, 1, axis=0))
        vi = jnp.where(rowid == 0, pi, pltpu.roll(si, 1, axis=0))
        for d, k in ((1, S5_R - 1), (2, S5_R), (4, S5_R + 1)):
            m_r, m_i = bcast(apr_ref, k), bcast(api_ref, k)
            keep = rowid >= d
            tr = jnp.where(keep, pltpu.roll(vr, d, axis=0), 0.0)
            ti = jnp.where(keep, pltpu.roll(vi, d, axis=0), 0.0)
            vr, vi = _cmul_add(m_r, m_i, tr, ti, vr, vi)
        for r in range(S5_R):
            rows = slice(r * S5_CH, (r + 1) * S5_CH)
            p_r, p_i = bcast(apr_ref, r), bcast(api_ref, r)
            xr, xi = _cmul_add(p_r, p_i, vr, vi, bu_ref[rows, re], bu_ref[rows, im])
            xb_ref[rows, re] = xr.astype(BF16)
            xb_ref[rows, im] = xi.astype(BF16)
        sre_ref[0, :, re] = xr[S5_CH - 1:S5_CH, :]
        sim_ref[0, :, re] = xi[S5_CH - 1:S5_CH, :]

    yp = jnp.dot(xb_ref[...], cc_ref[...], preferred_element_type=F32)
    ya_ref[...] = _sel_left(permt_ref[...], yp) + d_ref[...] * ua


def _s5_perm():
    p = np.zeros((S5_T, S5_T), np.float32)
    for i in range(S5_T):
        p[i, (i % S5_CH) * S5_R + i // S5_CH] = 1.0
    return jnp.asarray(p, BF16), jnp.asarray(p.T, BF16)


def _s5_prompt_call(ua, lp, nseq, seqlen):
    nchunk = seqlen // S5_T
    perm, permt = _s5_perm()
    consts = (perm, permt, lp['s5_bb'], lp['s5_cc'], lp['s5_apr'], lp['s5_api'], lp['s5_d'])
    st_spec = pl.BlockSpec((1, 1, S5_LANES), lambda b, c: (b, 0, 0))
    st_shape = jax.ShapeDtypeStruct((nseq, 1, S5_LANES), F32)
    return pl.pallas_call(
        _s5_prompt_kernel,
        grid=(nseq, nchunk),
        in_specs=[pl.BlockSpec((S5_T, S5_WIDTH), lambda b, c: (b * nchunk + c, 0))]
        + [_const_spec(c.shape) for c in consts],
        out_specs=[pl.BlockSpec((S5_T, S5_WIDTH), lambda b, c: (b * nchunk + c, 0)), st_spec, st_spec],
        out_shape=[jax.ShapeDtypeStruct(ua.shape, F32), st_shape, st_shape],
        scratch_shapes=[pltpu.VMEM((S5_T, 2 * S5_LANES), F32), pltpu.VMEM((S5_T, 2 * S5_LANES), BF16)],
        compiler_params=_params(("parallel", "arbitrary")),
        name="s5_prompt",
    )(ua, *consts)


def _s5_sample_kernel(nstep, nseq, ua_ref, sre0_ref, sim0_ref, bb_ref, cc_ref, apr_ref, api_ref,
                      d_ref, ya_ref, sre_ref, sim_ref, bu_ref, xb_ref):
    ua = ua_ref[...]
    bu_ref[...] = jnp.dot(ua.astype(BF16), bb_ref[...], preferred_element_type=F32)
    for lc in range(S5_LANES // S5_LC):
        re = slice(lc * S5_LC, (lc + 1) * S5_LC)
        im = slice(S5_LANES + lc * S5_LC, S5_LANES + (lc + 1) * S5_LC)
        a_r = apr_ref[0:1, re]
        a_i = api_ref[0:1, re]
        sr = sre0_ref[:, re]
        si = sim0_ref[:, re]
        for t in range(nstep):
            rows = slice(t * nseq, (t + 1) * nseq)
            sr, si = _cmul_add(a_r, a_i, sr, si, bu_ref[rows, re], bu_ref[rows, im])
            xb_ref[rows, re] = sr.astype(BF16)
            xb_ref[rows, im] = si.astype(BF16)
        sre_ref[:, re] = sr
        sim_ref[:, re] = si
    ya_ref[...] = jnp.dot(xb_ref[...], cc_ref[...], preferred_element_type=F32) + d_ref[...] * ua


def _s5_sample_call(ua_tm, sre0, sim0, lp, nstep, nseq):
    rows = nstep * nseq
    args = (ua_tm, sre0, sim0, lp['s5_bb'], lp['s5_cc'], lp['s5_apr'], lp['s5_api'], lp['s5_d'])
    st_shape = jax.ShapeDtypeStruct((nseq, S5_LANES), F32)
    full = lambda s: pl.BlockSpec(s, lambda i: (0,) * len(s))
    return pl.pallas_call(
        functools.partial(_s5_sample_kernel, nstep, nseq),
        grid=(1,),
        in_specs=[full(a.shape) for a in args],
        out_specs=[full((rows, S5_WIDTH)), full((nseq, S5_LANES)), full((nseq, S5_LANES))],
        out_shape=[jax.ShapeDtypeStruct((rows, S5_WIDTH), F32), st_shape, st_shape],
        scratch_shapes=[pltpu.VMEM((rows, 2 * S5_LANES), F32), pltpu.VMEM((rows, 2 * S5_LANES), BF16)],
        compiler_params=_params(("arbitrary",)),
        name="s5_sample",
    )(*args)


def _mix_kernel(t_rows, n_valid, nb, ssd_q, hg_q,
                xbc_ref, dt_ref, q_ref, lf_ref, kk_ref, v_ref, ssd0_ref, conv0_ref, hg0_ref,
                cw_ref, cb_ref, arow_ref, dexp_ref, expand_ref, ltri_s_ref, ltri_h_ref,
                yb_ref, oc_ref, ssd_ref, conv_ref, hg_ref,
                xpad_ref, hb_ref, hk_ref, hv_ref, ho_ref):
    @pl.when(pl.program_id(1) == 0)
    def _():
        ssd_ref[...] = ssd0_ref[...]
        conv_ref[...] = conv0_ref[...]
        hg_ref[...] = hg0_ref[...]

    if n_valid < t_rows:
        valid = (lax.broadcasted_iota(jnp.int32, (t_rows, 1), 0) < n_valid).astype(F32)
    else:
        valid = None
    expand = expand_ref[...]
    tail = SSD_CONV - 1

    def ssd_chunk(j, r0):
        q_ = ssd_q
        rows = slice(r0, r0 + q_)
        xc = cb_ref[...]
        for k in range(SSD_CONV):
            xc = xc + cw_ref[k:k + 1, :] * xpad_ref[SUBLANE - tail + k + r0:SUBLANE - tail + k + r0 + q_, :]
        xc = xc * _sigmoid(xc)
        xs = xc[:, :SSD_INNER]
        bm = xc[:, SSD_INNER:SSD_INNER + SSD_GROUPS * SSD_STATE]
        cm = xc[:, SSD_INNER + SSD_GROUPS * SSD_STATE:]
        dt = dt_ref[j, rows, :]
        if valid is not None:
            dt = dt * valid[rows, :]
        acum = _sel_left(ltri_s_ref[...], dt * arow_ref[...])
        if q_ < LANE:
            acum_sq = jnp.concatenate([acum, jnp.zeros((LANE - q_, LANE), F32)], axis=0)
        else:
            acum_sq = acum
        acum_t = acum_sq.T
        alast = acum[q_ - 1:q_, :]
        xdt = xs * _sel_right(dt, expand)
        eac = _sel_right(jnp.exp(acum), expand)
        xw = xdt * _sel_right(jnp.exp(alast - acum), expand)
        ela_t = jnp.exp(jnp.broadcast_to(acum_t[:, q_ - 1:q_], (LANE, LANE)))
        tri = lax.broadcasted_iota(jnp.int32, (q_, q_), 0) >= lax.broadcasted_iota(jnp.int32, (q_, q_), 1)
        hpg = SSD_HEADS // SSD_GROUPS
        gw = hpg * SSD_HEAD_DIM
        for g in range(SSD_GROUPS):
            cg = cm[:, g * SSD_STATE:(g + 1) * SSD_STATE]
            bg = bm[:, g * SSD_STATE:(g + 1) * SSD_STATE]
            cb = _bdot_nt(cg, bg)
            sg = ssd_ref[j, g * gw:(g + 1) * gw, :]
            yoff = _bdot_nt(cg, sg)
            scale = []
            for r in range(hpg):
                h = g * hpg + r
                hl = slice(h * SSD_HEAD_DIM, (h + 1) * SSD_HEAD_DIM)
                seg = acum[:, h:h + 1] - acum_t[h:h + 1, :q_]
                dec = jnp.exp(jnp.where(tri, seg, -jnp.inf))
                yd = _bdot(cb * dec, xdt[:, hl])
                yb_ref[j, rows, hl] = (yd + yoff[:, r * SSD_HEAD_DIM:(r + 1) * SSD_HEAD_DIM] * eac[:, hl]
                                       + dexp_ref[:, hl] * xs[:, hl])
                scale.append(jnp.broadcast_to(ela_t[h:h + 1, :], (SSD_HEAD_DIM, SSD_STATE)))
            upd = _bdot_tn(xw[:, g * gw:(g + 1) * gw], bg)
            ssd_ref[j, g * gw:(g + 1) * gw, :] = sg * jnp.concatenate(scale, axis=0) + upd

    def hg_chunk(j, r0):
        q_ = hg_q
        rows = slice(r0, r0 + q_)
        qq = q_ref[j, rows, :]
        lf = lf_ref[j, rows, :]
        kk = kk_ref[j, rows, :]
        vv = v_ref[j, rows, :]
        if valid is not None:
            lf = lf * valid[rows, :]
            kk = kk * valid[rows, :]
        bc = _sel_left(ltri_h_ref[...], lf)
        blast = bc[q_ - 1:q_, :]
        qe = qq * jnp.exp(bc)
        kd = kk * jnp.exp(blast - bc)
        safe = jnp.min(bc) >= HG_SAFE_LOG
        tri = lax.broadcasted_iota(jnp.int32, (q_, q_), 0) >= lax.broadcasted_iota(jnp.int32, (q_, q_), 1)

        @pl.when(safe)
        def _():
            ke = kk * jnp.exp(-bc)
            for h in range(HG_HEADS):
                hl = slice(h * HG_KDIM, (h + 1) * HG_KDIM)
                att = jnp.where(tri, _bdot_nt(qe[:, hl], ke[:, hl]), 0.0)
                ho_ref[0:q_, hl] = _bdot(att, vv[:, hl])

        @pl.when(jnp.logical_not(safe))
        def _():
            hb_ref[0:q_, :] = bc
            hk_ref[0:q_, :] = kk
            hv_ref[0:q_, :] = vv
            ho_ref[0:q_, :] = jnp.zeros((q_, HG_WIDTH), F32)
            rid = lax.broadcasted_iota(jnp.int32, (q_, 1), 0)

            def body(s, c):
                brow = hb_ref[pl.ds(s, 1), :]
                krow = hk_ref[pl.ds(s, 1), :]
                vrow = hv_ref[pl.ds(s, 1), :]
                w = qq * krow * jnp.exp(jnp.minimum(bc - brow, 0.0))
                w = jnp.where(rid >= s, w, 0.0)
                for h in range(HG_HEADS):
                    hl = slice(h * HG_KDIM, (h + 1) * HG_KDIM)
                    a = jnp.sum(w[:, hl], axis=-1, keepdims=True)
                    ho_ref[0:q_, hl] = ho_ref[0:q_, hl] + a * vrow[:, hl]
                return c

            lax.fori_loop(0, q_, body, 0)

        eb = jnp.exp(blast)
        for h in range(HG_HEADS):
            hl = slice(h * HG_KDIM, (h + 1) * HG_KDIM)
            sh = hg_ref[j, hl, :]
            oc_ref[j, rows, hl] = ho_ref[0:q_, hl] + _bdot(qe[:, hl], sh)
            eb_t = jnp.broadcast_to(eb[:, hl], (HG_KDIM, HG_KDIM)).T
            hg_ref[j, hl, :] = sh * eb_t + _bdot_tn(kd[:, hl], vv[:, hl])

    def seq_body(j, carry):
        xpad_ref[SUBLANE - tail:SUBLANE, :] = conv_ref[j]
        xpad_ref[SUBLANE:SUBLANE + t_rows, :] = xbc_ref[j]
        for c in range(t_rows // ssd_q):
            ssd_chunk(j, c * ssd_q)
        for c in range(t_rows // hg_q):
            hg_chunk(j, c * hg_q)
        conv_ref[j] = xpad_ref[SUBLANE + n_valid - tail:SUBLANE + n_valid, :]
        return carry

    if nb == 1:
        seq_body(0, 0)
    else:
        lax.fori_loop(0, nb, seq_body, 0)


def _ltri(n):
    return jnp.asarray(np.tril(np.ones((n, n), np.float32)), BF16)


def _mix_call(pre, ssd0, conv0, hg0, lp, nseq, seqlen, t_rows, n_valid, nb, ssd_q, hg_q):
    nchunk = seqlen // t_rows
    view = lambda a: a.reshape(nseq, seqlen, a.shape[-1])
    seq_in = [view(pre[k]) for k in ('xbc', 'dt', 'q', 'lf', 'kk', 'v')]
    consts = (lp['conv_w'], lp['conv_b'], lp['ssd_arow'], lp['ssd_dexp'], lp['ssd_expand'],
              _ltri(ssd_q), _ltri(hg_q))
    tile = lambda c: pl.BlockSpec((nb, t_rows, c), lambda i, k: (i, k, 0))
    st = lambda a: pl.BlockSpec((nb,) + a.shape[1:], lambda i, k: (i, 0, 0))
    states = (ssd0, conv0, hg0)
    kern = functools.partial(_mix_kernel, t_rows, n_valid, nb, ssd_q, hg_q)
    return pl.pallas_call(
        kern,
        grid=(nseq // nb, nchunk),
        in_specs=[tile(a.shape[-1]) for a in seq_in] + [st(a) for a in states]
        + [_const_spec(c.shape) for c in consts],
        out_specs=[tile(SSD_INNER), tile(HG_WIDTH)] + [st(a) for a in states],
        out_shape=[jax.ShapeDtypeStruct((nseq, seqlen, SSD_INNER), F32),
                   jax.ShapeDtypeStruct((nseq, seqlen, HG_WIDTH), F32)]
        + [jax.ShapeDtypeStruct(a.shape, F32) for a in states],
        scratch_shapes=[pltpu.VMEM((t_rows + SUBLANE, SSD_CONV_DIM), F32)]
        + [pltpu.VMEM((max(hg_q, SUBLANE), HG_WIDTH), F32) for _ in range(4)],
        compiler_params=_params(("parallel", "arbitrary")),
        name="ssd_hgrn_mix",
    )(*seq_in, *states, *consts)


def _post_kernel(x_ref, ya_ref, yb_ref, sz_ref, oc_ref, sg_ref, gt_ref,
                 wglu_ref, ssdnw_ref, wssd_ref, hgnw_ref, whg_ref, wo_ref, x1_ref):
    ya = ya_ref[...]
    cdf = 0.5 * (1.0 + jnp.tanh(math.sqrt(2.0 / math.pi) * (ya + 0.044715 * (ya * ya * ya))))
    glu = _bdot(ya * cdf, wglu_ref[...])
    out_a = glu[:, :D_MODEL] * _sigmoid(glu[:, D_MODEL:])
    out_b = _bdot(_rms(yb_ref[...] * sz_ref[...], ssdnw_ref[...]), wssd_ref[...])
    oc = oc_ref[...]
    normed = []
    for h in range(HG_HEADS):
        hl = slice(h * HG_VDIM, (h + 1) * HG_VDIM)
        normed.append(_rms(oc[:, hl], hgnw_ref[:, hl]))
    out_c = _bdot(jnp.concatenate(normed, axis=-1) * sg_ref[...], whg_ref[...])
    mixed = (gt_ref[:, :D_MODEL] * out_a + gt_ref[:, D_MODEL:2 * D_MODEL] * out_b
             + gt_ref[:, 2 * D_MODEL:] * out_c)
    x1_ref[...] = x_ref[...] + _bdot(mixed, wo_ref[...])


def _post_call(x2d, ya, yb, pre, oc, lp):
    rows = x2d.shape[0]
    tm = PRE_TM
    seq_in = (x2d, ya, yb, pre['sz'], oc, pre['sg'], pre['gt'])
    consts = (lp['w_glu'], lp['ssd_nw'], lp['w_ssd'], lp['hg_nw'], lp['w_hg'], lp['w_o'])
    row = lambda c: pl.BlockSpec((tm, c), lambda i: (i, 0))
    return pl.pallas_call(
        _post_kernel,
        grid=(rows // tm,),
        in_specs=[row(a.shape[-1]) for a in seq_in] + [_const_spec(c.shape) for c in consts],
        out_specs=row(D_MODEL),
        out_shape=jax.ShapeDtypeStruct((rows, D_MODEL), F32),
        compiler_params=_params(("parallel",)),
        name="post_merge",
    )(*seq_in, *consts)


def _ffn_kernel(final, seq_rows, tm, x_ref, *rest):
    if seq_rows:
        (sp1_ref, sp2_ref, n2_ref, wup_ref, cw_ref, cb_ref, wdn_ref, nf_ref,
         out_ref, aff_ref, apad_ref) = rest
    else:
        (n2_ref, wup_ref, cw_ref, cb_ref, wdn_ref, nf_ref, out_ref, aff_ref, apad_ref) = rest
    tail = FFN_CONV - 1
    x = x_ref[...]
    up = _bdot(_rms(x, n2_ref[...]), wup_ref[...])
    a = up[:, :D_FF]
    g = up[:, D_FF:]
    if seq_rows:
        apad_ref[0:SUBLANE, :] = jnp.zeros((SUBLANE, D_FF), F32)
    else:
        @pl.when(pl.program_id(1) == 0)
        def _():
            apad_ref[0:SUBLANE, :] = jnp.zeros((SUBLANE, D_FF), F32)
    apad_ref[SUBLANE:SUBLANE + tm, :] = a
    p1 = apad_ref[SUBLANE - 1:SUBLANE - 1 + tm, :]
    p2 = apad_ref[SUBLANE - 2:SUBLANE - 2 + tm, :]
    if seq_rows:
        t = lax.broadcasted_iota(jnp.int32, (tm, 1), 0) % seq_rows
        p1 = jnp.where(t < 1, sp1_ref[...], p1)
        p2 = jnp.where(t < 2, sp2_ref[...], p2)
        aff_ref[...] = a
    else:
        apad_ref[SUBLANE - tail:SUBLANE, :] = a[tm - tail:tm, :]
        aff_ref[0] = a[tm - tail:tm, :]
    ac = cw_ref[0:1, :] * p2 + cw_ref[1:2, :] * p1 + cw_ref[2:3, :] * a + cb_ref[...]
    y = x + _bdot(ac * _sigmoid(ac) * g, wdn_ref[...])
    if final:
        y = _rms(y, nf_ref[...])
    out_ref[...] = y


def _ffn_prompt_call(x1, lp, nf, final, nseq, seqlen):
    tm = PRE_TM
    nchunk = seqlen // tm
    consts = (lp['n2'], lp['w_up'], lp['ffn_cw'], lp['ffn_cb'], lp['w_dn'], nf)
    row = pl.BlockSpec((tm, D_MODEL), lambda b, c: (b * nchunk + c, 0))
    return pl.pallas_call(
        functools.partial(_ffn_kernel, final, 0, tm),
        grid=(nseq, nchunk),
        in_specs=[row] + [_const_spec(c.shape) for c in consts],
        out_specs=[row, pl.BlockSpec((1, FFN_CONV - 1, D_FF), lambda b, c: (b, 0, 0))],
        out_shape=[jax.ShapeDtypeStruct(x1.shape, F32),
                   jax.ShapeDtypeStruct((nseq, FFN_CONV - 1, D_FF), F32)],
        scratch_shapes=[pltpu.VMEM((tm + SUBLANE, D_FF), F32)],
        compiler_params=_params(("parallel", "arbitrary")),
        name="ffn_prompt",
    )(x1, *consts)


def _ffn_sample_call(x1, sp1, sp2, lp, nf, final):
    rows = x1.shape[0]
    tm = PRE_TM
    consts = (lp['n2'], lp['w_up'], lp['ffn_cw'], lp['ffn_cb'], lp['w_dn'], nf)
    row = lambda c: pl.BlockSpec((tm, c), lambda i: (i, 0))
    return pl.pallas_call(
        functools.partial(_ffn_kernel, final, SAMPLE_PAD, tm),
        grid=(rows // tm,),
        in_specs=[row(D_MODEL), row(D_FF), row(D_FF)] + [_const_spec(c.shape) for c in consts],
        out_specs=[row(D_MODEL), row(D_FF)],
        out_shape=[jax.ShapeDtypeStruct(x1.shape, F32), jax.ShapeDtypeStruct((rows, D_FF), F32)],
        scratch_shapes=[pltpu.VMEM((tm + SUBLANE, D_FF), F32)],
        compiler_params=_params(("arbitrary",)),
        name="ffn_sample",
    )(x1, sp1, sp2, *consts)


def _layer_params(l, lb, norm1_w, w_in, s5_log_dt, s5_lambda_re, s5_lambda_im, s5_b_re, s5_b_im,
                  s5_c_re, s5_c_im, s5_d, s5_w_glu, ssd_conv_w, ssd_conv_b, ssd_dt_bias, ssd_a_log,
                  ssd_d, ssd_norm_w, ssd_w_out, hg_norm_w, hg_w_out, w_o, norm2_w, ffn_w_up,
                  ffn_conv_w, ffn_conv_b, ffn_w_down):
    row = lambda v: v.astype(F32).reshape(1, -1)
    lp = {}
    lp['n1'] = row(norm1_w[l])
    widths = (S5_WIDTH, SSD_INNER, SSD_CONV_DIM, SSD_HEADS, HG_WIDTH, HG_WIDTH, HG_WIDTH, HG_WIDTH,
              3 * D_MODEL)
    names = ('w_ua', 'w_z', 'w_xbc', 'w_dt', 'w_q', 'w_f', 'w_i', 'w_g', 'w_gt')
    off = 0
    for name, w in zip(names, widths):
        lp[name] = w_in[l][:, off:off + w].astype(BF16)
        off += w
    lp['w_dt'] = jnp.pad(lp['w_dt'], ((0, 0), (0, LANE - SSD_HEADS)))
    lp['dtb'] = jnp.pad(row(ssd_dt_bias[l]), ((0, 0), (0, LANE - SSD_HEADS)))
    lp['loglb'] = jnp.log(row(lb))
    lp['log1mlb'] = jnp.log1p(-row(lb))
    lp['omlb'] = 1.0 - row(lb)

    delta = jnp.exp(s5_log_dt[l].astype(F32))[:, None]
    lr = s5_lambda_re[l].astype(F32)
    li = s5_lambda_im[l].astype(F32)
    mag = jnp.exp(lr * delta)
    ab_re = mag * jnp.cos(li * delta)
    ab_im = mag * jnp.sin(li * delta)
    den = lr * lr + li * li
    nr = ab_re - 1.0
    co_re = (nr * lr + ab_im * li) / den
    co_im = (ab_im * lr - nr * li) / den
    br = s5_b_re[l].astype(F32)
    bi = s5_b_im[l].astype(F32)
    bb_re = co_re[..., None] * br - co_im[..., None] * bi
    bb_im = co_re[..., None] * bi + co_im[..., None] * br
    eye = jnp.eye(S5_GROUPS, dtype=F32)
    bd_in = lambda m: jnp.einsum('gnj,gh->gjhn', m, eye).reshape(S5_WIDTH, S5_LANES)
    bd_out = lambda m: jnp.einsum('gjn,gh->gnhj', m, eye).reshape(S5_LANES, S5_WIDTH)
    lp['s5_bb'] = jnp.concatenate([bd_in(bb_re), bd_in(bb_im)], axis=1).astype(BF16)
    lp['s5_cc'] = jnp.concatenate([bd_out(s5_c_re[l].astype(F32)),
                                   -bd_out(s5_c_im[l].astype(F32))], axis=0).astype(BF16)
    pw = jnp.asarray(list(range(1, S5_R + 1)) + [2 * S5_R, 4 * S5_R], F32)[:, None, None]
    pm = jnp.exp(lr * delta * pw)
    ph = li * delta * pw
    pad = ((0, 40 - (S5_R + 2)), (0, 0))
    lp['s5_apr'] = jnp.pad((pm * jnp.cos(ph)).reshape(-1, S5_LANES), pad)
    lp['s5_api'] = jnp.pad((pm * jnp.sin(ph)).reshape(-1, S5_LANES), pad)
    lp['s5_d'] = row(s5_d[l])
    lp['w_glu'] = s5_w_glu[l].astype(BF16)

    lp['conv_w'] = ssd_conv_w[l].astype(F32)
    lp['conv_b'] = row(ssd_conv_b[l])
    lp['ssd_arow'] = jnp.pad(-jnp.exp(row(ssd_a_log[l])), ((0, 0), (0, LANE - SSD_HEADS)))
    lp['ssd_dexp'] = jnp.repeat(row(ssd_d[l]), SSD_HEAD_DIM, axis=1)
    ex = np.zeros((LANE, SSD_INNER), np.float32)
    for h in range(SSD_HEADS):
        ex[h, h * SSD_HEAD_DIM:(h + 1) * SSD_HEAD_DIM] = 1.0
    lp['ssd_expand'] = jnp.asarray(ex, BF16)
    lp['ssd_nw'] = row(ssd_norm_w[l])
    lp['w_ssd'] = ssd_w_out[l].astype(BF16)
    lp['hg_nw'] = row(hg_norm_w[l])
    lp['w_hg'] = hg_w_out[l].astype(BF16)
    lp['w_o'] = w_o[l].astype(BF16)
    lp['n2'] = row(norm2_w[l])
    lp['w_up'] = ffn_w_up[l].astype(BF16)
    lp['ffn_cw'] = ffn_conv_w[l].astype(F32)
    lp['ffn_cb'] = row(ffn_conv_b[l])
    lp['w_dn'] = ffn_w_down[l].astype(BF16)
    return lp


_PRE_KEYS = ('ua', 'sz', 'xbc', 'dt', 'q', 'lf', 'kk', 'v', 'sg', 'gt')


def kernel(x_prompt, x_sample, state_s5_re, state_s5_im, state_ssd, state_ssd_conv, state_hgrn, state_ffn_conv, norm1_w, w_in, s5_log_dt, s5_lambda_re, s5_lambda_im, s5_b_re, s5_b_im, s5_c_re, s5_c_im, s5_d, s5_w_glu, ssd_conv_w, ssd_conv_b, ssd_dt_bias, ssd_a_log, ssd_d, ssd_norm_w, ssd_w_out, hg_lb_logits, hg_norm_w, hg_w_out, w_o, norm2_w, ffn_w_up, ffn_conv_w, ffn_conv_b, ffn_w_down, norm_f_w):
    nb_p, len_p, _ = x_prompt.shape
    nb_s, len_s, _ = x_sample.shape
    assert len_p % S5_T == 0 and len_p % MIX_T == 0 and len_p % PRE_TM == 0
    assert len_s <= SAMPLE_PAD and len_s >= SSD_CONV - 1 and (nb_s * SAMPLE_PAD) % PRE_TM == 0
    lb_cum = jnp.cumsum(jax.nn.softmax(hg_lb_logits.astype(F32), axis=0), axis=0)
    lb_all = lb_cum - lb_cum[0:1]
    nf = norm_f_w.astype(F32).reshape(1, -1)

    xp = x_prompt.astype(F32).reshape(nb_p * len_p, D_MODEL)
    xs = jnp.pad(x_sample.astype(F32), ((0, 0), (0, SAMPLE_PAD - len_s), (0, 0)))
    xs = xs.reshape(nb_s * SAMPLE_PAD, D_MODEL)
    zeros_p = (jnp.zeros((nb_p, SSD_INNER, SSD_STATE), F32),
               jnp.zeros((nb_p, SSD_CONV - 1, SSD_CONV_DIM), F32),
               jnp.zeros((nb_p, HG_WIDTH, HG_VDIM), F32))
    new_p = []
    new_s = []
    for l in range(DEPTH):
        lp = _layer_params(l, lb_all[l], norm1_w, w_in, s5_log_dt, s5_lambda_re, s5_lambda_im,
                           s5_b_re, s5_b_im, s5_c_re, s5_c_im, s5_d, s5_w_glu, ssd_conv_w,
                           ssd_conv_b, ssd_dt_bias, ssd_a_log, ssd_d, ssd_norm_w, ssd_w_out,
                           hg_norm_w, hg_w_out, w_o, norm2_w, ffn_w_up, ffn_conv_w, ffn_conv_b,
                           ffn_w_down)
        final = l == DEPTH - 1

        pre = dict(zip(_PRE_KEYS, _pre_call(xp, lp)))
        ya, p_re, p_im = _s5_prompt_call(pre['ua'], lp, nb_p, len_p)
        yb, oc, p_ssd, p_conv, p_hg = _mix_call(pre, *zeros_p, lp, nb_p, len_p, MIX_T, MIX_T, 1,
                                                SSD_Q, HG_Q)
        x1 = _post_call(xp, ya, yb.reshape(-1, SSD_INNER), pre, oc.reshape(-1, HG_WIDTH), lp)
        xp, p_ffn = _ffn_prompt_call(x1, lp, nf, final, nb_p, len_p)
        new_p.append((p_re.reshape(nb_p, S5_GROUPS, S5_STATE), p_im.reshape(nb_p, S5_GROUPS, S5_STATE),
                      p_ssd.reshape(nb_p, SSD_HEADS, SSD_HEAD_DIM, SSD_STATE), p_conv,
                      p_hg.reshape(nb_p, HG_HEADS, HG_KDIM, HG_VDIM), p_ffn))

        pre = dict(zip(_PRE_KEYS, _pre_call(xs, lp)))
        ua_tm = pre['ua'].reshape(nb_s, SAMPLE_PAD, S5_WIDTH)[:, :len_s].transpose(1, 0, 2)
        ya_tm, s_re, s_im = _s5_sample_call(
            ua_tm.reshape(len_s * nb_s, S5_WIDTH),
            state_s5_re[l].astype(F32).reshape(nb_s, S5_LANES),
            state_s5_im[l].astype(F32).reshape(nb_s, S5_LANES), lp, len_s, nb_s)
        ya = jnp.pad(ya_tm.reshape(len_s, nb_s, S5_WIDTH).transpose(1, 0, 2),
                     ((0, 0), (0, SAMPLE_PAD - len_s), (0, 0))).reshape(nb_s * SAMPLE_PAD, S5_WIDTH)
        yb, oc, s_ssd, s_conv, s_hg = _mix_call(
            pre, state_ssd[l].astype(F32).reshape(nb_s, SSD_INNER, SSD_STATE),
            state_ssd_conv[l].astype(F32),
            state_hgrn[l].astype(F32).reshape(nb_s, HG_WIDTH, HG_VDIM),
            lp, nb_s, SAMPLE_PAD, SAMPLE_PAD, len_s, MIX_NB_SAMPLE, SAMPLE_PAD, SAMPLE_PAD)
        x1 = _post_call(xs, ya, yb.reshape(-1, SSD_INNER), pre, oc.reshape(-1, HG_WIDTH), lp)
        st = state_ffn_conv[l].astype(F32)
        zrow = jnp.zeros((nb_s, SAMPLE_PAD - 2, D_FF), F32)
        sp1 = jnp.concatenate([st[:, 1:2], st[:, 1:2], zrow], axis=1).reshape(-1, D_FF)
        sp2 = jnp.concatenate([st[:, 0:1], st[:, 1:2], zrow], axis=1).reshape(-1, D_FF)
        xs, aff = _ffn_sample_call(x1, sp1, sp2, lp, nf, final)
        s_ffn = aff.reshape(nb_s, SAMPLE_PAD, D_FF)[:, len_s - (FFN_CONV - 1):len_s]
        new_s.append((s_re.reshape(nb_s, S5_GROUPS, S5_STATE), s_im.reshape(nb_s, S5_GROUPS, S5_STATE),
                      s_ssd.reshape(nb_s, SSD_HEADS, SSD_HEAD_DIM, SSD_STATE), s_conv,
                      s_hg.reshape(nb_s, HG_HEADS, HG_KDIM, HG_VDIM), s_ffn))

    stk = lambda lst, i: jnp.stack([s[i] for s in lst])
    y_prompt = xp.reshape(nb_p, len_p, D_MODEL)
    y_sample = xs.reshape(nb_s, SAMPLE_PAD, D_MODEL)[:, :len_s]
    return (y_prompt, y_sample,
            stk(new_p, 0), stk(new_p, 1), stk(new_p, 2), stk(new_p, 3), stk(new_p, 4), stk(new_p, 5),
            stk(new_s, 0), stk(new_s, 1), stk(new_s, 2), stk(new_s, 3), stk(new_s, 4), stk(new_s, 5))
```

```python
import functools
import math

import jax
import jax.numpy as jnp
import numpy as np
from jax import lax
from jax.experimental import pallas as pl
from jax.experimental.pallas import tpu as pltpu

F32 = jnp.float32
BF16 = jnp.bfloat16

D_MODEL = 1024
DEPTH = 2
S5_WIDTH = 512
S5_GROUP = 16
S5_GROUPS = 32
S5_STATE = 64
S5_LANES = S5_GROUPS * S5_STATE
SSD_INNER = 1024
SSD_HEAD_DIM = 64
SSD_HEADS = 16
SSD_GROUPS = 4
SSD_STATE = 128
SSD_CONV = 4
SSD_CONV_DIM = 2048
HG_WIDTH = 512
HG_HEADS = 4
HG_KDIM = 128
HG_VDIM = 128
D_FF = 2816
FFN_CONV = 3
EPS = 1e-6

LANE = 128
SUBLANE = 8
SAMPLE_PAD = 8
HG_SAFE_LOG = -80.0

PRE_TM = 256
S5_T = 256
S5_CH = 8
S5_R = S5_T // S5_CH
S5_LC = 512
MIX_T = 128
SSD_Q = 128
HG_Q = 64
MIX_NB_SAMPLE = 8
VMEM_LIMIT = 56 * 1024 * 1024


def _const_spec(shape):
    nd = len(shape)
    return pl.BlockSpec(shape, lambda *_: (0,) * nd, pipeline_mode=pl.Buffered(1))


def _params(sem):
    return pltpu.CompilerParams(dimension_semantics=sem, vmem_limit_bytes=VMEM_LIMIT)


def _bdot(a, b):
    return jnp.dot(a.astype(BF16), b.astype(BF16), preferred_element_type=F32)


def _bdot_nt(a, b):
    return lax.dot_general(a.astype(BF16), b.astype(BF16), (((1,), (1,)), ((), ())),
                           preferred_element_type=F32)


def _bdot_tn(a, b):
    return lax.dot_general(a.astype(BF16), b.astype(BF16), (((0,), (0,)), ((), ())),
                           preferred_element_type=F32)


def _split3(x):
    h = x.astype(BF16)
    r = x - h.astype(F32)
    m = r.astype(BF16)
    l = (r - m.astype(F32)).astype(BF16)
    return h, m, l


def _sel_left(m01, x):
    h, m, l = _split3(x)
    d = lambda p: jnp.dot(m01, p, preferred_element_type=F32)
    return (d(h) + d(m)) + d(l)


def _sel_right(x, m01):
    h, m, l = _split3(x)
    d = lambda p: jnp.dot(p, m01, preferred_element_type=F32)
    return (d(h) + d(m)) + d(l)


def _sigmoid(x):
    return 1.0 / (1.0 + jnp.exp(-x))


def _softplus(x):
    return jnp.maximum(x, 0.0) + jnp.log1p(jnp.exp(-jnp.abs(x)))


def _rms(x, w):
    return x * lax.rsqrt(jnp.mean(x * x, axis=-1, keepdims=True) + EPS) * w


def _pre_kernel(x_ref, n1_ref, w_ua, w_xbc, w_dt, w_q, w_f, w_i,
                dtb_ref, loglb_ref, log1mlb_ref, omlb_ref,
                ua_o, xbc_o, dt_o, q_o, lf_o, kk_o, v_o):
    hb = _rms(x_ref[...], n1_ref[...]).astype(BF16)
    dot = lambda w: jnp.dot(hb, w[...], preferred_element_type=F32)
    ua_o[...] = dot(w_ua)
    xbc_o[...] = dot(w_xbc)
    dt_o[...] = _softplus(dot(w_dt) + dtb_ref[...])
    q_o[...] = dot(w_q)
    zf = dot(w_f)
    log_sig = jnp.minimum(zf, 0.0) - jnp.log1p(jnp.exp(-jnp.abs(zf)))
    a = loglb_ref[...]
    b = log1mlb_ref[...] + log_sig
    lf_o[...] = jnp.maximum(a, b) + jnp.log1p(jnp.exp(-jnp.abs(a - b)))
    kk_o[...] = omlb_ref[...] * _sigmoid(-zf)
    v_o[...] = dot(w_i)


def _pre_call(x2d, lp):
    rows = x2d.shape[0]
    tm = PRE_TM
    row = lambda c: pl.BlockSpec((tm, c), lambda i: (i, 0))
    widths = (S5_WIDTH, SSD_CONV_DIM, LANE, HG_WIDTH, HG_WIDTH, HG_WIDTH, HG_WIDTH)
    consts = (lp['n1'], lp['w_ua'], lp['w_xbc'], lp['w_dt'], lp['w_q'], lp['w_f'], lp['w_i'],
              lp['dtb'], lp['loglb'], lp['log1mlb'], lp['omlb'])
    return pl.pallas_call(
        _pre_kernel,
        grid=(rows // tm,),
        in_specs=[row(D_MODEL)] + [_const_spec(c.shape) for c in consts],
        out_specs=[row(c) for c in widths],
        out_shape=[jax.ShapeDtypeStruct((rows, c), F32) for c in widths],
        compiler_params=_params(("parallel",)),
        name="pre_proj",
    )(x2d, *consts)


def _cmul_add(ar, ai, xr, xi, br, bi):
    return ar * xr - ai * xi + br, ar * xi + ai * xr + bi


def _s5_prompt_kernel(ua_ref, perm_ref, permt_ref, bb_ref, cc_ref, apr_ref, api_ref, d_ref,
                      ya_ref, sre_ref, sim_ref, bu_ref, xb_ref):
    @pl.when(pl.program_id(1) == 0)
    def _():
        sre_ref[...] = jnp.zeros_like(sre_ref)
        sim_ref[...] = jnp.zeros_like(sim_ref)

    ua = ua_ref[...]
    up = jnp.dot(perm_ref[...], ua.astype(BF16), preferred_element_type=F32).astype(BF16)
    bu_ref[...] = jnp.dot(up, bb_ref[...], preferred_element_type=F32)
    rowid = lax.broadcasted_iota(jnp.int32, (S5_CH, S5_LC), 0)

    for lc in range(S5_LANES // S5_LC):
        re = slice(lc * S5_LC, (lc + 1) * S5_LC)
        im = slice(S5_LANES + lc * S5_LC, S5_LANES + (lc + 1) * S5_LC)
        bcast = lambda ref, k: jnp.broadcast_to(ref[k:k + 1, re], (S5_CH, S5_LC))
        a_r, a_i = bcast(apr_ref, 0), bcast(api_ref, 0)
        sr = jnp.zeros((S5_CH, S5_LC), F32)
        si = jnp.zeros((S5_CH, S5_LC), F32)
        for r in range(S5_R):
            rows = slice(r * S5_CH, (r + 1) * S5_CH)
            sr, si = _cmul_add(a_r, a_i, sr, si, bu_ref[rows, re], bu_ref[rows, im])
            bu_ref[rows, re] = sr
            bu_ref[rows, im] = si
        pr = jnp.broadcast_to(sre_ref[0, :, re], (S5_CH, S5_LC))
        pi = jnp.broadcast_to(sim_ref[0, :, re], (S5_CH, S5_LC))
        vr = jnp.where(rowid == 0, pr, pltpu.roll(sr, 1, axis=0))
        vi = jnp.where(rowid == 0, pi, pltpu.roll(si, 1, axis=0))
        for d, k in ((1, S5_R - 1), (2, S5_R), (4, S5_R + 1)):
            m_r, m_i = bcast(apr_ref, k), bcast(api_ref, k)
            keep = rowid >= d
            tr = jnp.where(keep, pltpu.roll(vr, d, axis=0), 0.0)
            ti = jnp.where(keep, pltpu.roll(vi, d, axis=0), 0.0)
            vr, vi = _cmul_add(m_r, m_i, tr, ti, vr, vi)
        for r in range(S5_R):
            rows = slice(r * S5_CH, (r + 1) * S5_CH)
            p_r, p_i = bcast(apr_ref, r), bcast(api_ref, r)
            xr, xi = _cmul_add(p_r, p_i, vr, vi, bu_ref[rows, re], bu_ref[rows, im])
            xb_ref[rows, re] = xr.astype(BF16)
            xb_ref[rows, im] = xi.astype(BF16)
        sre_ref[0, :, re] = xr[S5_CH - 1:S5_CH, :]
        sim_ref[0, :, re] = xi[S5_CH - 1:S5_CH, :]

    yp = jnp.dot(xb_ref[...], cc_ref[...], preferred_element_type=F32)
    ya_ref[...] = _sel_left(permt_ref[...], yp) + d_ref[...] * ua


def _s5_perm():
    p = np.zeros((S5_T, S5_T), np.float32)
    for i in range(S5_T):
        p[i, (i % S5_CH) * S5_R + i // S5_CH] = 1.0
    return jnp.asarray(p, BF16), jnp.asarray(p.T, BF16)


def _s5_prompt_call(ua, lp, nseq, seqlen):
    nchunk = seqlen // S5_T
    perm, permt = _s5_perm()
    consts = (perm, permt, lp['s5_bb'], lp['s5_cc'], lp['s5_apr'], lp['s5_api'], lp['s5_d'])
    st_spec = pl.BlockSpec((1, 1, S5_LANES), lambda b, c: (b, 0, 0))
    st_shape = jax.ShapeDtypeStruct((nseq, 1, S5_LANES), F32)
    return pl.pallas_call(
        _s5_prompt_kernel,
        grid=(nseq, nchunk),
        in_specs=[pl.BlockSpec((S5_T, S5_WIDTH), lambda b, c: (b * nchunk + c, 0))]
        + [_const_spec(c.shape) for c in consts],
        out_specs=[pl.BlockSpec((S5_T, S5_WIDTH), lambda b, c: (b * nchunk + c, 0)), st_spec, st_spec],
        out_shape=[jax.ShapeDtypeStruct(ua.shape, F32), st_shape, st_shape],
        scratch_shapes=[pltpu.VMEM((S5_T, 2 * S5_LANES), F32), pltpu.VMEM((S5_T, 2 * S5_LANES), BF16)],
        compiler_params=_params(("parallel", "arbitrary")),
        name="s5_prompt",
    )(ua, *consts)


def _s5_sample_kernel(nstep, nseq, ua_ref, sre0_ref, sim0_ref, bb_ref, cc_ref, apr_ref, api_ref,
                      d_ref, ya_ref, sre_ref, sim_ref, bu_ref, xb_ref):
    ua = ua_ref[...]
    bu_ref[...] = jnp.dot(ua.astype(BF16), bb_ref[...], preferred_element_type=F32)
    for lc in range(S5_LANES // S5_LC):
        re = slice(lc * S5_LC, (lc + 1) * S5_LC)
        im = slice(S5_LANES + lc * S5_LC, S5_LANES + (lc + 1) * S5_LC)
        a_r = apr_ref[0:1, re]
        a_i = api_ref[0:1, re]
        sr = sre0_ref[:, re]
        si = sim0_ref[:, re]
        for t in range(nstep):
            rows = slice(t * nseq, (t + 1) * nseq)
            sr, si = _cmul_add(a_r, a_i, sr, si, bu_ref[rows, re], bu_ref[rows, im])
            xb_ref[rows, re] = sr.astype(BF16)
            xb_ref[rows, im] = si.astype(BF16)
        sre_ref[:, re] = sr
        sim_ref[:, re] = si
    ya_ref[...] = jnp.dot(xb_ref[...], cc_ref[...], preferred_element_type=F32) + d_ref[...] * ua


def _s5_sample_call(ua_tm, sre0, sim0, lp, nstep, nseq):
    rows = nstep * nseq
    args = (ua_tm, sre0, sim0, lp['s5_bb'], lp['s5_cc'], lp['s5_apr'], lp['s5_api'], lp['s5_d'])
    st_shape = jax.ShapeDtypeStruct((nseq, S5_LANES), F32)
    full = lambda s: pl.BlockSpec(s, lambda i: (0,) * len(s))
    return pl.pallas_call(
        functools.partial(_s5_sample_kernel, nstep, nseq),
        grid=(1,),
        in_specs=[full(a.shape) for a in args],
        out_specs=[full((rows, S5_WIDTH)), full((nseq, S5_LANES)), full((nseq, S5_LANES))],
        out_shape=[jax.ShapeDtypeStruct((rows, S5_WIDTH), F32), st_shape, st_shape],
        scratch_shapes=[pltpu.VMEM((rows, 2 * S5_LANES), F32), pltpu.VMEM((rows, 2 * S5_LANES), BF16)],
        compiler_params=_params(("arbitrary",)),
        name="s5_sample",
    )(*args)


def _mix_kernel(t_rows, n_valid, nb, ssd_q, hg_q, n_alias,
                xbc_ref, dt_ref, q_ref, lf_ref, kk_ref, v_ref, ssd0_ref, conv0_ref, hg0_ref,
                cw_ref, cb_ref, arow_ref, dexp_ref, expand_ref, ltri_s_ref, ltri_h_ref, *rest):
    (yb_ref, oc_ref, ssd_ref, conv_ref, hg_ref,
     xpad_ref, xc_ref, hb_ref, hk_ref, ho_ref) = rest[n_alias:]

    @pl.when(pl.program_id(1) == 0)
    def _():
        ssd_ref[...] = ssd0_ref[...]
        conv_ref[...] = conv0_ref[...]
        hg_ref[...] = hg0_ref[...]

    if n_valid < t_rows:
        valid = (lax.broadcasted_iota(jnp.int32, (t_rows, 1), 0) < n_valid).astype(F32)
    else:
        valid = None
    expand = expand_ref[...]
    tail = SSD_CONV - 1

    def conv_stage(j):
        xpad_ref[SUBLANE - tail:SUBLANE, :] = conv_ref[j]
        xpad_ref[SUBLANE:SUBLANE + t_rows, :] = xbc_ref[j]
        xc = cb_ref[...]
        for k in range(SSD_CONV):
            xc = xc + cw_ref[k:k + 1, :] * xpad_ref[SUBLANE - tail + k:SUBLANE - tail + k + t_rows, :]
        xc_ref[...] = xc * _sigmoid(xc)
        conv_ref[j] = xpad_ref[SUBLANE + n_valid - tail:SUBLANE + n_valid, :]

    def ssd_chunk(j, r0):
        q_ = ssd_q
        rows = slice(r0, r0 + q_)
        xs = xc_ref[rows, :SSD_INNER]
        bm = xc_ref[rows, SSD_INNER:SSD_INNER + SSD_GROUPS * SSD_STATE]
        cm = xc_ref[rows, SSD_INNER + SSD_GROUPS * SSD_STATE:]
        dt = dt_ref[j, rows, :]
        if valid is not None:
            dt = dt * valid[rows, :]
        acum = _sel_left(ltri_s_ref[...], dt * arow_ref[...])
        if q_ < LANE:
            acum_sq = jnp.concatenate([acum, jnp.zeros((LANE - q_, LANE), F32)], axis=0)
        else:
            acum_sq = acum
        acum_t = acum_sq.T
        alast = acum[q_ - 1:q_, :]
        xdt = xs * _sel_right(dt, expand)
        eac = _sel_right(jnp.exp(acum), expand)
        xw = xdt * _sel_right(jnp.exp(alast - acum), expand)
        ela_t = jnp.exp(jnp.broadcast_to(acum_t[:, q_ - 1:q_], (LANE, LANE)))
        tri = lax.broadcasted_iota(jnp.int32, (q_, q_), 0) >= lax.broadcasted_iota(jnp.int32, (q_, q_), 1)
        hpg = SSD_HEADS // SSD_GROUPS
        gw = hpg * SSD_HEAD_DIM
        for g in range(SSD_GROUPS):
            cg = cm[:, g * SSD_STATE:(g + 1) * SSD_STATE]
            bg = bm[:, g * SSD_STATE:(g + 1) * SSD_STATE]
            cb = _bdot_nt(cg, bg)
            sg = ssd_ref[j, g * gw:(g + 1) * gw, :]
            yoff = _bdot_nt(cg, sg)
            scale = []
            for r in range(hpg):
                h = g * hpg + r
                hl = slice(h * SSD_HEAD_DIM, (h + 1) * SSD_HEAD_DIM)
                seg = acum[:, h:h + 1] - acum_t[h:h + 1, :q_]
                dec = jnp.exp(jnp.where(tri, seg, -jnp.inf))
                yd = _bdot(cb * dec, xdt[:, hl])
                yb_ref[j, rows, hl] = (yd + yoff[:, r * SSD_HEAD_DIM:(r + 1) * SSD_HEAD_DIM] * eac[:, hl]
                                       + dexp_ref[:, hl] * xs[:, hl])
                scale.append(jnp.broadcast_to(ela_t[h:h + 1, :], (SSD_HEAD_DIM, SSD_STATE)))
            upd = _bdot_tn(xw[:, g * gw:(g + 1) * gw], bg)
            ssd_ref[j, g * gw:(g + 1) * gw, :] = sg * jnp.concatenate(scale, axis=0) + upd

    def hg_prepare(j):
        lf = lf_ref[j]
        kk = kk_ref[j]
        if valid is not None:
            lf = lf * valid
            kk = kk * valid
        bc = _sel_left(ltri_h_ref[...], lf)
        hb_ref[...] = bc
        hk_ref[...] = kk
        safe = jnp.min(bc) >= HG_SAFE_LOG

        @pl.when(safe)
        def _():
            ho_ref[...] = jnp.zeros((t_rows, HG_WIDTH), F32)

        @pl.when(jnp.logical_not(safe))
        def _():
            ho_ref[...] = jnp.zeros((t_rows, HG_WIDTH), F32)
            rid = lax.broadcasted_iota(jnp.int32, (t_rows, 1), 0)

            def body(s, c):
                brow = hb_ref[pl.ds(s, 1), :]
                krow = hk_ref[pl.ds(s, 1), :]
                vrow = v_ref[j, pl.ds(s, 1), :]
                w = q_ref[j] * krow * jnp.exp(jnp.minimum(hb_ref[...] - brow, 0.0))
                same_chunk_end = (s // hg_q + 1) * hg_q
                w = jnp.where(jnp.logical_and(rid >= s, rid < same_chunk_end), w, 0.0)
                for h in range(HG_HEADS):
                    hl = slice(h * HG_KDIM, (h + 1) * HG_KDIM)
                    a = jnp.sum(w[:, hl], axis=-1, keepdims=True)
                    ho_ref[:, hl] = ho_ref[:, hl] + a * vrow[:, hl]
                return c

            lax.fori_loop(0, t_rows, body, 0)

        return safe

    def hg_chunk(j, r0, safe):
        q_ = hg_q
        rows = slice(r0, r0 + q_)
        qq = q_ref[j, rows, :]
        vv = v_ref[j, rows, :]
        bc = hb_ref[rows, :]
        kk = hk_ref[rows, :]
        blast = bc[q_ - 1:q_, :]
        qe = qq * jnp.exp(bc)
        kd = kk * jnp.exp(blast - bc)
        ke = kk * jnp.exp(-bc)
        tri = lax.broadcasted_iota(jnp.int32, (q_, q_), 0) >= lax.broadcasted_iota(jnp.int32, (q_, q_), 1)
        eb = jnp.exp(blast)
        for h in range(HG_HEADS):
            hl = slice(h * HG_KDIM, (h + 1) * HG_KDIM)
            att = jnp.where(tri, _bdot_nt(qe[:, hl], ke[:, hl]), 0.0)
            intra = jnp.where(safe, _bdot(att, vv[:, hl]), ho_ref[rows, hl])
            sh = hg_ref[j, hl, :]
            oc_ref[j, rows, hl] = intra + _bdot(qe[:, hl], sh)
            eb_t = jnp.broadcast_to(eb[:, hl], (HG_KDIM, HG_KDIM)).T
            hg_ref[j, hl, :] = sh * eb_t + _bdot_tn(kd[:, hl], vv[:, hl])

    def seq_body(j, carry):
        conv_stage(j)
        safe = hg_prepare(j)
        for c in range(t_rows // ssd_q):
            ssd_chunk(j, c * ssd_q)
        for c in range(t_rows // hg_q):
            hg_chunk(j, c * hg_q, safe)
        return carry

    if nb == 1:
        seq_body(0, 0)
    else:
        lax.fori_loop(0, nb, seq_body, 0)


def _ltri(n, q):
    return jnp.asarray(np.kron(np.eye(n // q, dtype=np.float32), np.tril(np.ones((q, q), np.float32))), BF16)


def _mix_call(pre, states, l_in, l_out, prev, lp, nseq, seqlen, t_rows, n_valid, nb, ssd_q, hg_q):
    nchunk = seqlen // t_rows
    view = lambda a: a.reshape(nseq, seqlen, a.shape[-1])
    seq_in = [view(pre[k]) for k in ('xbc', 'dt', 'q', 'lf', 'kk', 'v')]
    consts = (lp['conv_w'], lp['conv_b'], lp['ssd_arow'], lp['ssd_dexp'], lp['ssd_expand'],
              _ltri(ssd_q, ssd_q), _ltri(t_rows, hg_q))
    tile = lambda c: pl.BlockSpec((nb, t_rows, c), lambda i, k: (i, k, 0))
    st = lambda a, l: pl.BlockSpec((None, nb) + a.shape[2:], lambda i, k: (l, i, 0, 0))
    prev = () if prev is None else tuple(prev)
    n_in = len(seq_in) + len(states) + len(consts)
    kern = functools.partial(_mix_kernel, t_rows, n_valid, nb, ssd_q, hg_q, len(prev))
    return pl.pallas_call(
        kern,
        grid=(nseq // nb, nchunk),
        in_specs=[tile(a.shape[-1]) for a in seq_in] + [st(a, l_in) for a in states]
        + [_const_spec(c.shape) for c in consts]
        + [pl.BlockSpec(memory_space=pl.ANY) for _ in prev],
        out_specs=[tile(SSD_INNER), tile(HG_WIDTH)] + [st(a, l_out) for a in states],
        out_shape=[jax.ShapeDtypeStruct((nseq, seqlen, SSD_INNER), F32),
                   jax.ShapeDtypeStruct((nseq, seqlen, HG_WIDTH), F32)]
        + [jax.ShapeDtypeStruct((DEPTH,) + a.shape[1:], F32) for a in states],
        input_output_aliases={n_in + k: 2 + k for k in range(len(prev))},
        scratch_shapes=[pltpu.VMEM((t_rows + SUBLANE, SSD_CONV_DIM), F32),
                        pltpu.VMEM((t_rows, SSD_CONV_DIM), F32)]
        + [pltpu.VMEM((t_rows, HG_WIDTH), F32) for _ in range(3)],
        compiler_params=_params(("parallel", "arbitrary")),
        name="ssd_hgrn_mix",
    )(*seq_in, *states, *consts, *prev)


def _post_kernel(x_ref, ya_ref, yb_ref, oc_ref, n1_ref, wz_ref, wg_ref, wgt_ref,
                 wglu_ref, ssdnw_ref, wssd_ref, hgnw_ref, whg_ref, wo_ref, x1_ref):
    x = x_ref[...]
    hb = _rms(x, n1_ref[...]).astype(BF16)
    proj = lambda w: jnp.dot(hb, w, preferred_element_type=F32)
    ya = ya_ref[...]
    cdf = 0.5 * (1.0 + jnp.tanh(math.sqrt(2.0 / math.pi) * (ya + 0.044715 * (ya * ya * ya))))
    glu = _bdot(ya * cdf, wglu_ref[...])
    out_a = glu[:, :D_MODEL] * _sigmoid(glu[:, D_MODEL:])
    z = proj(wz_ref[...])
    out_b = _bdot(_rms(yb_ref[...] * (z * _sigmoid(z)), ssdnw_ref[...]), wssd_ref[...])
    oc = oc_ref[...]
    normed = []
    for h in range(HG_HEADS):
        hl = slice(h * HG_VDIM, (h + 1) * HG_VDIM)
        normed.append(_rms(oc[:, hl], hgnw_ref[:, hl]))
    g = proj(wg_ref[...])
    out_c = _bdot(jnp.concatenate(normed, axis=-1) * (g * _sigmoid(g)), whg_ref[...])
    gate = lambda k: _sigmoid(proj(wgt_ref[:, k * D_MODEL:(k + 1) * D_MODEL]))
    mixed = gate(0) * out_a + gate(1) * out_b + gate(2) * out_c
    x1_ref[...] = x + _bdot(mixed, wo_ref[...])


def _post_call(x2d, ya, yb, oc, lp):
    rows = x2d.shape[0]
    tm = PRE_TM
    seq_in = (x2d, ya, yb, oc)
    consts = (lp['n1'], lp['w_z'], lp['w_g'], lp['w_gt'],
              lp['w_glu'], lp['ssd_nw'], lp['w_ssd'], lp['hg_nw'], lp['w_hg'], lp['w_o'])
    row = lambda c: pl.BlockSpec((tm, c), lambda i: (i, 0))
    return pl.pallas_call(
        _post_kernel,
        grid=(rows // tm,),
        in_specs=[row(a.shape[-1]) for a in seq_in] + [_const_spec(c.shape) for c in consts],
        out_specs=row(D_MODEL),
        out_shape=jax.ShapeDtypeStruct((rows, D_MODEL), F32),
        compiler_params=_params(("parallel",)),
        name="post_merge",
    )(*seq_in, *consts)


def _ffn_kernel(final, seq_rows, tm, x_ref, *rest):
    if seq_rows:
        (sp1_ref, sp2_ref, n2_ref, wup_ref, cw_ref, cb_ref, wdn_ref, nf_ref,
         out_ref, aff_ref, apad_ref) = rest
    else:
        (n2_ref, wup_ref, cw_ref, cb_ref, wdn_ref, nf_ref, out_ref, aff_ref, apad_ref) = rest
    tail = FFN_CONV - 1
    x = x_ref[...]
    up = _bdot(_rms(x, n2_ref[...]), wup_ref[...])
    a = up[:, :D_FF]
    g = up[:, D_FF:]
    if seq_rows:
        apad_ref[0:SUBLANE, :] = jnp.zeros((SUBLANE, D_FF), F32)
    else:
        @pl.when(pl.program_id(1) == 0)
        def _():
            apad_ref[0:SUBLANE, :] = jnp.zeros((SUBLANE, D_FF), F32)
    apad_ref[SUBLANE:SUBLANE + tm, :] = a
    p1 = apad_ref[SUBLANE - 1:SUBLANE - 1 + tm, :]
    p2 = apad_ref[SUBLANE - 2:SUBLANE - 2 + tm, :]
    if seq_rows:
        t = lax.broadcasted_iota(jnp.int32, (tm, 1), 0) % seq_rows
        p1 = jnp.where(t < 1, sp1_ref[...], p1)
        p2 = jnp.where(t < 2, sp2_ref[...], p2)
        aff_ref[...] = a
    else:
        apad_ref[SUBLANE - tail:SUBLANE, :] = a[tm - tail:tm, :]
        aff_ref[0] = a[tm - tail:tm, :]
    ac = cw_ref[0:1, :] * p2 + cw_ref[1:2, :] * p1 + cw_ref[2:3, :] * a + cb_ref[...]
    y = x + _bdot(ac * _sigmoid(ac) * g, wdn_ref[...])
    if final:
        y = _rms(y, nf_ref[...])
    out_ref[...] = y


def _ffn_prompt_call(x1, lp, nf, final, nseq, seqlen):
    tm = PRE_TM
    nchunk = seqlen // tm
    consts = (lp['n2'], lp['w_up'], lp['ffn_cw'], lp['ffn_cb'], lp['w_dn'], nf)
    row = pl.BlockSpec((tm, D_MODEL), lambda b, c: (b * nchunk + c, 0))
    return pl.pallas_call(
        functools.partial(_ffn_kernel, final, 0, tm),
        grid=(nseq, nchunk),
        in_specs=[row] + [_const_spec(c.shape) for c in consts],
        out_specs=[row, pl.BlockSpec((1, FFN_CONV - 1, D_FF), lambda b, c: (b, 0, 0))],
        out_shape=[jax.ShapeDtypeStruct(x1.shape, F32),
                   jax.ShapeDtypeStruct((nseq, FFN_CONV - 1, D_FF), F32)],
        scratch_shapes=[pltpu.VMEM((tm + SUBLANE, D_FF), F32)],
        compiler_params=_params(("parallel", "arbitrary")),
        name="ffn_prompt",
    )(x1, *consts)


def _ffn_sample_call(x1, sp1, sp2, lp, nf, final):
    rows = x1.shape[0]
    tm = PRE_TM
    consts = (lp['n2'], lp['w_up'], lp['ffn_cw'], lp['ffn_cb'], lp['w_dn'], nf)
    row = lambda c: pl.BlockSpec((tm, c), lambda i: (i, 0))
    return pl.pallas_call(
        functools.partial(_ffn_kernel, final, SAMPLE_PAD, tm),
        grid=(rows // tm,),
        in_specs=[row(D_MODEL), row(D_FF), row(D_FF)] + [_const_spec(c.shape) for c in consts],
        out_specs=[row(D_MODEL), row(D_FF)],
        out_shape=[jax.ShapeDtypeStruct(x1.shape, F32), jax.ShapeDtypeStruct((rows, D_FF), F32)],
        scratch_shapes=[pltpu.VMEM((tm + SUBLANE, D_FF), F32)],
        compiler_params=_params(("arbitrary",)),
        name="ffn_sample",
    )(x1, sp1, sp2, *consts)


def _layer_params(l, lb, norm1_w, w_in, s5_log_dt, s5_lambda_re, s5_lambda_im, s5_b_re, s5_b_im,
                  s5_c_re, s5_c_im, s5_d, s5_w_glu, ssd_conv_w, ssd_conv_b, ssd_dt_bias, ssd_a_log,
                  ssd_d, ssd_norm_w, ssd_w_out, hg_norm_w, hg_w_out, w_o, norm2_w, ffn_w_up,
                  ffn_conv_w, ffn_conv_b, ffn_w_down):
    row = lambda v: v.astype(F32).reshape(1, -1)
    lp = {}
    lp['n1'] = row(norm1_w[l])
    widths = (S5_WIDTH, SSD_INNER, SSD_CONV_DIM, SSD_HEADS, HG_WIDTH, HG_WIDTH, HG_WIDTH, HG_WIDTH,
              3 * D_MODEL)
    names = ('w_ua', 'w_z', 'w_xbc', 'w_dt', 'w_q', 'w_f', 'w_i', 'w_g', 'w_gt')
    off = 0
    for name, w in zip(names, widths):
        lp[name] = w_in[l][:, off:off + w].astype(BF16)
        off += w
    lp['w_dt'] = jnp.pad(lp['w_dt'], ((0, 0), (0, LANE - SSD_HEADS)))
    lp['dtb'] = jnp.pad(row(ssd_dt_bias[l]), ((0, 0), (0, LANE - SSD_HEADS)))
    lp['loglb'] = jnp.log(row(lb))
    lp['log1mlb'] = jnp.log1p(-row(lb))
    lp['omlb'] = 1.0 - row(lb)

    delta = jnp.exp(s5_log_dt[l].astype(F32))[:, None]
    lr = s5_lambda_re[l].astype(F32)
    li = s5_lambda_im[l].astype(F32)
    mag = jnp.exp(lr * delta)
    ab_re = mag * jnp.cos(li * delta)
    ab_im = mag * jnp.sin(li * delta)
    den = lr * lr + li * li
    nr = ab_re - 1.0
    co_re = (nr * lr + ab_im * li) / den
    co_im = (ab_im * lr - nr * li) / den
    br = s5_b_re[l].astype(F32)
    bi = s5_b_im[l].astype(F32)
    bb_re = co_re[..., None] * br - co_im[..., None] * bi
    bb_im = co_re[..., None] * bi + co_im[..., None] * br
    eye = jnp.eye(S5_GROUPS, dtype=F32)
    bd_in = lambda m: jnp.einsum('gnj,gh->gjhn', m, eye).reshape(S5_WIDTH, S5_LANES)
    bd_out = lambda m: jnp.einsum('gjn,gh->gnhj', m, eye).reshape(S5_LANES, S5_WIDTH)
    lp['s5_bb'] = jnp.concatenate([bd_in(bb_re), bd_in(bb_im)], axis=1).astype(BF16)
    lp['s5_cc'] = jnp.concatenate([bd_out(s5_c_re[l].astype(F32)),
                                   -bd_out(s5_c_im[l].astype(F32))], axis=0).astype(BF16)
    pw = jnp.asarray(list(range(1, S5_R + 1)) + [2 * S5_R, 4 * S5_R], F32)[:, None, None]
    pm = jnp.exp(lr * delta * pw)
    ph = li * delta * pw
    pad = ((0, 40 - (S5_R + 2)), (0, 0))
    lp['s5_apr'] = jnp.pad((pm * jnp.cos(ph)).reshape(-1, S5_LANES), pad)
    lp['s5_api'] = jnp.pad((pm * jnp.sin(ph)).reshape(-1, S5_LANES), pad)
    lp['s5_d'] = row(s5_d[l])
    lp['w_glu'] = s5_w_glu[l].astype(BF16)

    lp['conv_w'] = ssd_conv_w[l].astype(F32)
    lp['conv_b'] = row(ssd_conv_b[l])
    lp['ssd_arow'] = jnp.pad(-jnp.exp(row(ssd_a_log[l])), ((0, 0), (0, LANE - SSD_HEADS)))
    lp['ssd_dexp'] = jnp.repeat(row(ssd_d[l]), SSD_HEAD_DIM, axis=1)
    ex = np.zeros((LANE, SSD_INNER), np.float32)
    for h in range(SSD_HEADS):
        ex[h, h * SSD_HEAD_DIM:(h + 1) * SSD_HEAD_DIM] = 1.0
    lp['ssd_expand'] = jnp.asarray(ex, BF16)
    lp['ssd_nw'] = row(ssd_norm_w[l])
    lp['w_ssd'] = ssd_w_out[l].astype(BF16)
    lp['hg_nw'] = row(hg_norm_w[l])
    lp['w_hg'] = hg_w_out[l].astype(BF16)
    lp['w_o'] = w_o[l].astype(BF16)
    lp['n2'] = row(norm2_w[l])
    lp['w_up'] = ffn_w_up[l].astype(BF16)
    lp['ffn_cw'] = ffn_conv_w[l].astype(F32)
    lp['ffn_cb'] = row(ffn_conv_b[l])
    lp['w_dn'] = ffn_w_down[l].astype(BF16)
    return lp


_PRE_KEYS = ('ua', 'xbc', 'dt', 'q', 'lf', 'kk', 'v')


def kernel(x_prompt, x_sample, state_s5_re, state_s5_im, state_ssd, state_ssd_conv, state_hgrn, state_ffn_conv, norm1_w, w_in, s5_log_dt, s5_lambda_re, s5_lambda_im, s5_b_re, s5_b_im, s5_c_re, s5_c_im, s5_d, s5_w_glu, ssd_conv_w, ssd_conv_b, ssd_dt_bias, ssd_a_log, ssd_d, ssd_norm_w, ssd_w_out, hg_lb_logits, hg_norm_w, hg_w_out, w_o, norm2_w, ffn_w_up, ffn_conv_w, ffn_conv_b, ffn_w_down, norm_f_w):
    nb_p, len_p, _ = x_prompt.shape
    nb_s, len_s, _ = x_sample.shape
    assert len_p % S5_T == 0 and len_p % MIX_T == 0 and len_p % PRE_TM == 0
    assert len_s <= SAMPLE_PAD and len_s >= SSD_CONV - 1 and (nb_s * SAMPLE_PAD) % PRE_TM == 0
    lb_cum = jnp.cumsum(jax.nn.softmax(hg_lb_logits.astype(F32), axis=0), axis=0)
    lb_all = lb_cum - lb_cum[0:1]
    nf = norm_f_w.astype(F32).reshape(1, -1)

    xp = x_prompt.astype(F32).reshape(nb_p * len_p, D_MODEL)
    xs = jnp.pad(x_sample.astype(F32), ((0, 0), (0, SAMPLE_PAD - len_s), (0, 0)))
    xs = xs.reshape(nb_s * SAMPLE_PAD, D_MODEL)
    zeros_p = (jnp.zeros((1, nb_p, SSD_INNER, SSD_STATE), F32),
               jnp.zeros((1, nb_p, SSD_CONV - 1, SSD_CONV_DIM), F32),
               jnp.zeros((1, nb_p, HG_WIDTH, HG_VDIM), F32))
    states_s = (state_ssd.astype(F32).reshape(DEPTH, nb_s, SSD_INNER, SSD_STATE),
                state_ssd_conv.astype(F32),
                state_hgrn.astype(F32).reshape(DEPTH, nb_s, HG_WIDTH, HG_VDIM))
    mix_p = None
    mix_s = None
    new_p = []
    new_s = []
    for l in range(DEPTH):
        lp = _layer_params(l, lb_all[l], norm1_w, w_in, s5_log_dt, s5_lambda_re, s5_lambda_im,
                           s5_b_re, s5_b_im, s5_c_re, s5_c_im, s5_d, s5_w_glu, ssd_conv_w,
                           ssd_conv_b, ssd_dt_bias, ssd_a_log, ssd_d, ssd_norm_w, ssd_w_out,
                           hg_norm_w, hg_w_out, w_o, norm2_w, ffn_w_up, ffn_conv_w, ffn_conv_b,
                           ffn_w_down)
        final = l == DEPTH - 1

        pre = dict(zip(_PRE_KEYS, _pre_call(xp, lp)))
        ya, p_re, p_im = _s5_prompt_call(pre['ua'], lp, nb_p, len_p)
        yb, oc, *mix_p = _mix_call(pre, zeros_p, 0, l, mix_p, lp, nb_p, len_p, MIX_T, MIX_T, 1,
                                   SSD_Q, HG_Q)
        x1 = _post_call(xp, ya, yb.reshape(-1, SSD_INNER), oc.reshape(-1, HG_WIDTH), lp)
        xp, p_ffn = _ffn_prompt_call(x1, lp, nf, final, nb_p, len_p)
        new_p.append((p_re.reshape(nb_p, S5_GROUPS, S5_STATE), p_im.reshape(nb_p, S5_GROUPS, S5_STATE),
                      p_ffn))

        pre = dict(zip(_PRE_KEYS, _pre_call(xs, lp)))
        ua_tm = pre['ua'].reshape(nb_s, SAMPLE_PAD, S5_WIDTH)[:, :len_s].transpose(1, 0, 2)
        ya_tm, s_re, s_im = _s5_sample_call(
            ua_tm.reshape(len_s * nb_s, S5_WIDTH),
            state_s5_re[l].astype(F32).reshape(nb_s, S5_LANES),
            state_s5_im[l].astype(F32).reshape(nb_s, S5_LANES), lp, len_s, nb_s)
        ya = jnp.pad(ya_tm.reshape(len_s, nb_s, S5_WIDTH).transpose(1, 0, 2),
                     ((0, 0), (0, SAMPLE_PAD - len_s), (0, 0))).reshape(nb_s * SAMPLE_PAD, S5_WIDTH)
        yb, oc, *mix_s = _mix_call(pre, states_s, l, l, mix_s, lp, nb_s, SAMPLE_PAD, SAMPLE_PAD, len_s,
                                   MIX_NB_SAMPLE, SAMPLE_PAD, SAMPLE_PAD)
        x1 = _post_call(xs, ya, yb.reshape(-1, SSD_INNER), oc.reshape(-1, HG_WIDTH), lp)
        st = state_ffn_conv[l].astype(F32)
        zrow = jnp.zeros((nb_s, SAMPLE_PAD - 2, D_FF), F32)
        sp1 = jnp.concatenate([st[:, 1:2], st[:, 1:2], zrow], axis=1).reshape(-1, D_FF)
        sp2 = jnp.concatenate([st[:, 0:1], st[:, 1:2], zrow], axis=1).reshape(-1, D_FF)
        xs, aff = _ffn_sample_call(x1, sp1, sp2, lp, nf, final)
        s_ffn = aff.reshape(nb_s, SAMPLE_PAD, D_FF)[:, len_s - (FFN_CONV - 1):len_s]
        new_s.append((s_re.reshape(nb_s, S5_GROUPS, S5_STATE), s_im.reshape(nb_s, S5_GROUPS, S5_STATE),
                      s_ffn))

    stk = lambda lst, i: jnp.stack([s[i] for s in lst])
    y_prompt = xp.reshape(nb_p, len_p, D_MODEL)
    y_sample = xs.reshape(nb_s, SAMPLE_PAD, D_MODEL)[:, :len_s]

    def mix_states(m, n):
        ssd, conv, hg = m
        return (ssd.reshape(DEPTH, n, SSD_HEADS, SSD_HEAD_DIM, SSD_STATE), conv,
                hg.reshape(DEPTH, n, HG_HEADS, HG_KDIM, HG_VDIM))

    p_ssd, p_conv, p_hg = mix_states(mix_p, nb_p)
    s_ssd, s_conv, s_hg = mix_states(mix_s, nb_s)
    return (y_prompt, y_sample,
            stk(new_p, 0), stk(new_p, 1), p_ssd, p_conv, p_hg, stk(new_p, 2),
            stk(new_s, 0), stk(new_s, 1), s_ssd, s_conv, s_hg, stk(new_s, 2))
```

```python
import functools
import math

import jax
import jax.numpy as jnp
import numpy as np
from jax import lax
from jax.experimental import pallas as pl
from jax.experimental.pallas import tpu as pltpu

F32 = jnp.float32
BF16 = jnp.bfloat16

D_MODEL = 1024
DEPTH = 2
S5_WIDTH = 512
S5_GROUP = 16
S5_GROUPS = 32
S5_STATE = 64
S5_LANES = S5_GROUPS * S5_STATE
SSD_INNER = 1024
SSD_HEAD_DIM = 64
SSD_HEADS = 16
SSD_GROUPS = 4
SSD_STATE = 128
SSD_CONV = 4
SSD_CONV_DIM = 2048
HG_WIDTH = 512
HG_HEADS = 4
HG_KDIM = 128
HG_VDIM = 128
D_FF = 2816
FFN_CONV = 3
EPS = 1e-6

LANE = 128
SUBLANE = 8
SAMPLE_PAD = 8
HG_SAFE_LOG = -80.0

PRE_TM = 256
FFN_TM = 512
FFN_BLK = 256
S5_T = 256
S5_CH = 8
S5_R = S5_T // S5_CH
S5_LC = 512
MIX_T = 128
SSD_Q = 128
HG_Q = 64
MIX_NB_SAMPLE = 8
VMEM_LIMIT = 56 * 1024 * 1024


def _const_spec(shape):
    nd = len(shape)
    return pl.BlockSpec(shape, lambda *_: (0,) * nd, pipeline_mode=pl.Buffered(1))


def _params(sem):
    return pltpu.CompilerParams(dimension_semantics=sem, vmem_limit_bytes=VMEM_LIMIT)


def _bdot(a, b):
    return jnp.dot(a.astype(BF16), b.astype(BF16), preferred_element_type=F32)


def _bdot_nt(a, b):
    return lax.dot_general(a.astype(BF16), b.astype(BF16), (((1,), (1,)), ((), ())),
                           preferred_element_type=F32)


def _bdot_tn(a, b):
    return lax.dot_general(a.astype(BF16), b.astype(BF16), (((0,), (0,)), ((), ())),
                           preferred_element_type=F32)


def _split3(x):
    h = x.astype(BF16)
    r = x - h.astype(F32)
    m = r.astype(BF16)
    l = (r - m.astype(F32)).astype(BF16)
    return h, m, l


def _sel_left(m01, x):
    h, m, l = _split3(x)
    d = lambda p: jnp.dot(m01, p, preferred_element_type=F32)
    return (d(h) + d(m)) + d(l)


def _sel_right(x, m01):
    h, m, l = _split3(x)
    d = lambda p: jnp.dot(p, m01, preferred_element_type=F32)
    return (d(h) + d(m)) + d(l)


def _sigmoid(x):
    return 1.0 / (1.0 + jnp.exp(-x))


def _softplus(x):
    return jnp.maximum(x, 0.0) + jnp.log1p(jnp.exp(-jnp.abs(x)))


def _rms(x, w):
    return x * lax.rsqrt(jnp.mean(x * x, axis=-1, keepdims=True) + EPS) * w


def _pre_kernel(x_ref, n1_ref, w_ua, w_xbc, w_dt, w_q, w_f, w_i,
                dtb_ref, loglb_ref, log1mlb_ref, omlb_ref,
                ua_o, xbc_o, dt_o, q_o, lf_o, kk_o, v_o):
    hb = _rms(x_ref[...], n1_ref[...]).astype(BF16)
    dot = lambda w: jnp.dot(hb, w[...], preferred_element_type=F32)
    ua_o[...] = dot(w_ua)
    xbc_o[...] = dot(w_xbc)
    dt_o[...] = _softplus(dot(w_dt) + dtb_ref[...])
    q_o[...] = dot(w_q)
    zf = dot(w_f)
    log_sig = jnp.minimum(zf, 0.0) - jnp.log1p(jnp.exp(-jnp.abs(zf)))
    a = loglb_ref[...]
    b = log1mlb_ref[...] + log_sig
    lf_o[...] = jnp.maximum(a, b) + jnp.log1p(jnp.exp(-jnp.abs(a - b)))
    kk_o[...] = omlb_ref[...] * _sigmoid(-zf)
    v_o[...] = dot(w_i)


def _pre_call(x2d, lp):
    rows = x2d.shape[0]
    tm = PRE_TM
    row = lambda c: pl.BlockSpec((tm, c), lambda i: (i, 0))
    widths = (S5_WIDTH, SSD_CONV_DIM, LANE, HG_WIDTH, HG_WIDTH, HG_WIDTH, HG_WIDTH)
    consts = (lp['n1'], lp['w_ua'], lp['w_xbc'], lp['w_dt'], lp['w_q'], lp['w_f'], lp['w_i'],
              lp['dtb'], lp['loglb'], lp['log1mlb'], lp['omlb'])
    return pl.pallas_call(
        _pre_kernel,
        grid=(rows // tm,),
        in_specs=[row(D_MODEL)] + [_const_spec(c.shape) for c in consts],
        out_specs=[row(c) for c in widths],
        out_shape=[jax.ShapeDtypeStruct((rows, c), F32) for c in widths],
        compiler_params=_params(("parallel",)),
        name="pre_proj",
    )(x2d, *consts)


def _cmul_add(ar, ai, xr, xi, br, bi):
    return ar * xr - ai * xi + br, ar * xi + ai * xr + bi


def _s5_prompt_kernel(ua_ref, perm_ref, permt_ref, bbc_ref, ccc_ref, apr_ref, api_ref, d_ref,
                      ya_ref, st_ref, bu_ref):
    @pl.when(pl.program_id(1) == 0)
    def _():
        st_ref[...] = jnp.zeros_like(st_ref)

    ua = ua_ref[...]
    up = jnp.dot(perm_ref[...], ua.astype(BF16), preferred_element_type=F32).astype(BF16)
    rowid = lax.broadcasted_iota(jnp.int32, (S5_CH, LANE), 0)
    tab = lambda ref, k, gp: ref[k * S5_CH:(k + 1) * S5_CH, gp * LANE:(gp + 1) * LANE]
    kw = 2 * LANE
    y_acc = [None] * (S5_WIDTH // kw)

    for lc in range(2 * S5_LANES // S5_LC):
        kt = lc * S5_LC // (2 * S5_LANES // (S5_WIDTH // kw))
        cols = slice(lc * S5_LC, (lc + 1) * S5_LC)
        bu_ref[:, cols] = jnp.dot(up[:, kt * kw:(kt + 1) * kw], bbc_ref[lc], preferred_element_type=F32)
        for gp in range(lc * S5_LC // kw, (lc + 1) * S5_LC // kw):
            re = slice(gp * kw, gp * kw + LANE)
            im = slice(gp * kw + LANE, (gp + 1) * kw)
            a_r, a_i = tab(apr_ref, 0, gp), tab(api_ref, 0, gp)
            sr = jnp.zeros((S5_CH, LANE), F32)
            si = jnp.zeros((S5_CH, LANE), F32)
            for r in range(S5_R):
                rows = slice(r * S5_CH, (r + 1) * S5_CH)
                sr, si = _cmul_add(a_r, a_i, sr, si, bu_ref[rows, re], bu_ref[rows, im])
                bu_ref[rows, re] = sr
                bu_ref[rows, im] = si
            pr = jnp.broadcast_to(st_ref[0, :, re], (S5_CH, LANE))
            pi = jnp.broadcast_to(st_ref[0, :, im], (S5_CH, LANE))
            vr = jnp.where(rowid == 0, pr, pltpu.roll(sr, 1, axis=0))
            vi = jnp.where(rowid == 0, pi, pltpu.roll(si, 1, axis=0))
            for d, k in ((1, S5_R - 1), (2, S5_R), (4, S5_R + 1)):
                keep = rowid >= d
                tr = jnp.where(keep, pltpu.roll(vr, d, axis=0), 0.0)
                ti = jnp.where(keep, pltpu.roll(vi, d, axis=0), 0.0)
                vr, vi = _cmul_add(tab(apr_ref, k, gp), tab(api_ref, k, gp), tr, ti, vr, vi)
            for r in range(S5_R):
                rows = slice(r * S5_CH, (r + 1) * S5_CH)
                xr, xi = _cmul_add(tab(apr_ref, r, gp), tab(api_ref, r, gp), vr, vi,
                                   bu_ref[rows, re], bu_ref[rows, im])
                bu_ref[rows, re] = xr
                bu_ref[rows, im] = xi
            st_ref[0, :, re] = xr[S5_CH - 1:S5_CH, :]
            st_ref[0, :, im] = xi[S5_CH - 1:S5_CH, :]
        part = _bdot(bu_ref[:, cols], ccc_ref[lc])
        y_acc[kt] = part if y_acc[kt] is None else y_acc[kt] + part

    yp = jnp.concatenate(y_acc, axis=1)
    ya_ref[...] = _sel_left(permt_ref[...], yp) + d_ref[...] * ua


def _s5_perm():
    p = np.zeros((S5_T, S5_T), np.float32)
    for i in range(S5_T):
        p[i, (i % S5_CH) * S5_R + i // S5_CH] = 1.0
    return jnp.asarray(p, BF16), jnp.asarray(p.T, BF16)


def _s5_prompt_call(ua, lp, nseq, seqlen):
    nchunk = seqlen // S5_T
    perm, permt = _s5_perm()
    consts = (perm, permt, lp['s5_bbc'], lp['s5_ccc'], lp['s5_apr8'], lp['s5_api8'], lp['s5_d'])
    st_spec = pl.BlockSpec((1, 1, 2 * S5_LANES), lambda b, c: (b, 0, 0))
    st_shape = jax.ShapeDtypeStruct((nseq, 1, 2 * S5_LANES), F32)
    ya, st = pl.pallas_call(
        _s5_prompt_kernel,
        grid=(nseq, nchunk),
        in_specs=[pl.BlockSpec((S5_T, S5_WIDTH), lambda b, c: (b * nchunk + c, 0))]
        + [_const_spec(c.shape) for c in consts],
        out_specs=[pl.BlockSpec((S5_T, S5_WIDTH), lambda b, c: (b * nchunk + c, 0)), st_spec],
        out_shape=[jax.ShapeDtypeStruct(ua.shape, F32), st_shape],
        scratch_shapes=[pltpu.VMEM((S5_T, 2 * S5_LANES), F32)],
        compiler_params=_params(("parallel", "arbitrary")),
        name="s5_prompt",
    )(ua, *consts)
    st = st.reshape(nseq, S5_LANES // LANE, 2, LANE)
    return ya, st[:, :, 0].reshape(nseq, S5_LANES), st[:, :, 1].reshape(nseq, S5_LANES)


def _s5_sample_kernel(nstep, nseq, ua_ref, sre0_ref, sim0_ref, bb_ref, cc_ref, apr_ref, api_ref,
                      d_ref, ya_ref, sre_ref, sim_ref, bu_ref, xb_ref):
    ua = ua_ref[...]
    bu_ref[...] = jnp.dot(ua.astype(BF16), bb_ref[...], preferred_element_type=F32)
    for lc in range(S5_LANES // S5_LC):
        re = slice(lc * S5_LC, (lc + 1) * S5_LC)
        im = slice(S5_LANES + lc * S5_LC, S5_LANES + (lc + 1) * S5_LC)
        a_r = apr_ref[0:1, re]
        a_i = api_ref[0:1, re]
        sr = sre0_ref[:, re]
        si = sim0_ref[:, re]
        for t in range(nstep):
            rows = slice(t * nseq, (t + 1) * nseq)
            sr, si = _cmul_add(a_r, a_i, sr, si, bu_ref[rows, re], bu_ref[rows, im])
            xb_ref[rows, re] = sr.astype(BF16)
            xb_ref[rows, im] = si.astype(BF16)
        sre_ref[:, re] = sr
        sim_ref[:, re] = si
    ya_ref[...] = jnp.dot(xb_ref[...], cc_ref[...], preferred_element_type=F32) + d_ref[...] * ua


def _s5_sample_call(ua_tm, sre0, sim0, lp, nstep, nseq):
    rows = nstep * nseq
    args = (ua_tm, sre0, sim0, lp['s5_bb'], lp['s5_cc'], lp['s5_apr'], lp['s5_api'], lp['s5_d'])
    st_shape = jax.ShapeDtypeStruct((nseq, S5_LANES), F32)
    full = lambda s: pl.BlockSpec(s, lambda i: (0,) * len(s))
    return pl.pallas_call(
        functools.partial(_s5_sample_kernel, nstep, nseq),
        grid=(1,),
        in_specs=[full(a.shape) for a in args],
        out_specs=[full((rows, S5_WIDTH)), full((nseq, S5_LANES)), full((nseq, S5_LANES))],
        out_shape=[jax.ShapeDtypeStruct((rows, S5_WIDTH), F32), st_shape, st_shape],
        scratch_shapes=[pltpu.VMEM((rows, 2 * S5_LANES), F32), pltpu.VMEM((rows, 2 * S5_LANES), BF16)],
        compiler_params=_params(("arbitrary",)),
        name="s5_sample",
    )(*args)


def _mix_kernel(t_rows, n_valid, nb, ssd_q, hg_q, n_alias,
                xbc_ref, dt_ref, q_ref, lf_ref, kk_ref, v_ref, ssd0_ref, conv0_ref, hg0_ref,
                cw_ref, cb_ref, arow_ref, dexp_ref, expand_ref, ltri_s_ref, ltri_h_ref, *rest):
    (yb_ref, oc_ref, ssd_ref, conv_ref, hg_ref,
     xpad_ref, xc_ref, hb_ref, hk_ref, ho_ref) = rest[n_alias:]

    @pl.when(pl.program_id(1) == 0)
    def _():
        ssd_ref[...] = ssd0_ref[...]
        conv_ref[...] = conv0_ref[...]
        hg_ref[...] = hg0_ref[...]

    if n_valid < t_rows:
        valid = (lax.broadcasted_iota(jnp.int32, (t_rows, 1), 0) < n_valid).astype(F32)
    else:
        valid = None
    expand = expand_ref[...]
    tail = SSD_CONV - 1

    def conv_stage(j):
        xpad_ref[SUBLANE - tail:SUBLANE, :] = conv_ref[j]
        xpad_ref[SUBLANE:SUBLANE + t_rows, :] = xbc_ref[j]
        xc = cb_ref[...]
        for k in range(SSD_CONV):
            xc = xc + cw_ref[k:k + 1, :] * xpad_ref[SUBLANE - tail + k:SUBLANE - tail + k + t_rows, :]
        xc_ref[...] = xc * _sigmoid(xc)
        conv_ref[j] = xpad_ref[SUBLANE + n_valid - tail:SUBLANE + n_valid, :]

    def ssd_chunk(j, r0):
        q_ = ssd_q
        rows = slice(r0, r0 + q_)
        xs = xc_ref[rows, :SSD_INNER]
        bm = xc_ref[rows, SSD_INNER:SSD_INNER + SSD_GROUPS * SSD_STATE]
        cm = xc_ref[rows, SSD_INNER + SSD_GROUPS * SSD_STATE:]
        dt = dt_ref[j, rows, :]
        if valid is not None:
            dt = dt * valid[rows, :]
        acum = _sel_left(ltri_s_ref[...], dt * arow_ref[...])
        if q_ < LANE:
            acum_sq = jnp.concatenate([acum, jnp.zeros((LANE - q_, LANE), F32)], axis=0)
        else:
            acum_sq = acum
        acum_t = acum_sq.T
        alast = acum[q_ - 1:q_, :]
        xdt = xs * _sel_right(dt, expand)
        eac = _sel_right(jnp.exp(acum), expand)
        xw = xdt * _sel_right(jnp.exp(alast - acum), expand)
        ela_t = jnp.exp(jnp.broadcast_to(acum_t[:, q_ - 1:q_], (LANE, LANE)))
        tri = lax.broadcasted_iota(jnp.int32, (q_, q_), 0) >= lax.broadcasted_iota(jnp.int32, (q_, q_), 1)
        hpg = SSD_HEADS // SSD_GROUPS
        gw = hpg * SSD_HEAD_DIM
        for g in range(SSD_GROUPS):
            cg = cm[:, g * SSD_STATE:(g + 1) * SSD_STATE]
            bg = bm[:, g * SSD_STATE:(g + 1) * SSD_STATE]
            cb = _bdot_nt(cg, bg)
            sg = ssd_ref[j, g * gw:(g + 1) * gw, :]
            yoff = _bdot_nt(cg, sg)
            scale = []
            for r in range(hpg):
                h = g * hpg + r
                hl = slice(h * SSD_HEAD_DIM, (h + 1) * SSD_HEAD_DIM)
                seg = acum[:, h:h + 1] - acum_t[h:h + 1, :q_]
                dec = jnp.exp(jnp.where(tri, seg, -jnp.inf))
                yd = _bdot(cb * dec, xdt[:, hl])
                yb_ref[j, rows, hl] = (yd + yoff[:, r * SSD_HEAD_DIM:(r + 1) * SSD_HEAD_DIM] * eac[:, hl]
                                       + dexp_ref[:, hl] * xs[:, hl])
                scale.append(jnp.broadcast_to(ela_t[h:h + 1, :], (SSD_HEAD_DIM, SSD_STATE)))
            upd = _bdot_tn(xw[:, g * gw:(g + 1) * gw], bg)
            ssd_ref[j, g * gw:(g + 1) * gw, :] = sg * jnp.concatenate(scale, axis=0) + upd

    def hg_prepare(j):
        lf = lf_ref[j]
        kk = kk_ref[j]
        if valid is not None:
            lf = lf * valid
            kk = kk * valid
        bc = _sel_left(ltri_h_ref[...], lf)
        hb_ref[...] = bc
        hk_ref[...] = kk
        safe = jnp.min(bc) >= HG_SAFE_LOG

        @pl.when(safe)
        def _():
            ho_ref[...] = jnp.zeros((t_rows, HG_WIDTH), F32)

        @pl.when(jnp.logical_not(safe))
        def _():
            ho_ref[...] = jnp.zeros((t_rows, HG_WIDTH), F32)
            rid = lax.broadcasted_iota(jnp.int32, (t_rows, 1), 0)

            def body(s, c):
                brow = hb_ref[pl.ds(s, 1), :]
                krow = hk_ref[pl.ds(s, 1), :]
                vrow = v_ref[j, pl.ds(s, 1), :]
                w = q_ref[j] * krow * jnp.exp(jnp.minimum(hb_ref[...] - brow, 0.0))
                same_chunk_end = (s // hg_q + 1) * hg_q
                w = jnp.where(jnp.logical_and(rid >= s, rid < same_chunk_end), w, 0.0)
                for h in range(HG_HEADS):
                    hl = slice(h * HG_KDIM, (h + 1) * HG_KDIM)
                    a = jnp.sum(w[:, hl], axis=-1, keepdims=True)
                    ho_ref[:, hl] = ho_ref[:, hl] + a * vrow[:, hl]
                return c

            lax.fori_loop(0, t_rows, body, 0)

        return safe

    def hg_chunk(j, r0, safe):
        q_ = hg_q
        rows = slice(r0, r0 + q_)
        qq = q_ref[j, rows, :]
        vv = v_ref[j, rows, :]
        bc = hb_ref[rows, :]
        kk = hk_ref[rows, :]
        blast = bc[q_ - 1:q_, :]
        qe = qq * jnp.exp(bc)
        kd = kk * jnp.exp(blast - bc)
        ke = kk * jnp.exp(-bc)
        tri = lax.broadcasted_iota(jnp.int32, (q_, q_), 0) >= lax.broadcasted_iota(jnp.int32, (q_, q_), 1)
        eb = jnp.exp(blast)
        for h in range(HG_HEADS):
            hl = slice(h * HG_KDIM, (h + 1) * HG_KDIM)
            att = jnp.where(tri, _bdot_nt(qe[:, hl], ke[:, hl]), 0.0)
            intra = jnp.where(safe, _bdot(att, vv[:, hl]), ho_ref[rows, hl])
            sh = hg_ref[j, hl, :]
            oc_ref[j, rows, hl] = intra + _bdot(qe[:, hl], sh)
            eb_t = jnp.broadcast_to(eb[:, hl], (HG_KDIM, HG_KDIM)).T
            hg_ref[j, hl, :] = sh * eb_t + _bdot_tn(kd[:, hl], vv[:, hl])

    def seq_body(j, carry):
        conv_stage(j)
        safe = hg_prepare(j)
        for c in range(t_rows // ssd_q):
            ssd_chunk(j, c * ssd_q)
        for c in range(t_rows // hg_q):
            hg_chunk(j, c * hg_q, safe)
        return carry

    if nb == 1:
        seq_body(0, 0)
    else:
        lax.fori_loop(0, nb, seq_body, 0)


def _ltri(n, q):
    return jnp.asarray(np.kron(np.eye(n // q, dtype=np.float32), np.tril(np.ones((q, q), np.float32))), BF16)


def _mix_call(pre, states, l_in, l_out, prev, lp, nseq, seqlen, t_rows, n_valid, nb, ssd_q, hg_q):
    nchunk = seqlen // t_rows
    view = lambda a: a.reshape(nseq, seqlen, a.shape[-1])
    seq_in = [view(pre[k]) for k in ('xbc', 'dt', 'q', 'lf', 'kk', 'v')]
    consts = (lp['conv_w'], lp['conv_b'], lp['ssd_arow'], lp['ssd_dexp'], lp['ssd_expand'],
              _ltri(ssd_q, ssd_q), _ltri(t_rows, hg_q))
    tile = lambda c: pl.BlockSpec((nb, t_rows, c), lambda i, k: (i, k, 0))
    st = lambda a, l: pl.BlockSpec((None, nb) + a.shape[2:], lambda i, k: (l, i, 0, 0))
    prev = () if prev is None else tuple(prev)
    n_in = len(seq_in) + len(states) + len(consts)
    kern = functools.partial(_mix_kernel, t_rows, n_valid, nb, ssd_q, hg_q, len(prev))
    return pl.pallas_call(
        kern,
        grid=(nseq // nb, nchunk),
        in_specs=[tile(a.shape[-1]) for a in seq_in] + [st(a, l_in) for a in states]
        + [_const_spec(c.shape) for c in consts]
        + [pl.BlockSpec(memory_space=pl.ANY) for _ in prev],
        out_specs=[tile(SSD_INNER), tile(HG_WIDTH)] + [st(a, l_out) for a in states],
        out_shape=[jax.ShapeDtypeStruct((nseq, seqlen, SSD_INNER), F32),
                   jax.ShapeDtypeStruct((nseq, seqlen, HG_WIDTH), F32)]
        + [jax.ShapeDtypeStruct((DEPTH,) + a.shape[1:], F32) for a in states],
        input_output_aliases={n_in + k: 2 + k for k in range(len(prev))},
        scratch_shapes=[pltpu.VMEM((t_rows + SUBLANE, SSD_CONV_DIM), F32),
                        pltpu.VMEM((t_rows, SSD_CONV_DIM), F32)]
        + [pltpu.VMEM((t_rows, HG_WIDTH), F32) for _ in range(3)],
        compiler_params=_params(("parallel", "arbitrary")),
        name="ssd_hgrn_mix",
    )(*seq_in, *states, *consts, *prev)


def _post_kernel(x_ref, ya_ref, yb_ref, oc_ref, n1_ref, wz_ref, wg_ref, wgt_ref,
                 wglu_ref, ssdnw_ref, wssd_ref, hgnw_ref, whg_ref, wo_ref, x1_ref):
    x = x_ref[...]
    hb = _rms(x, n1_ref[...]).astype(BF16)
    proj = lambda w: jnp.dot(hb, w, preferred_element_type=F32)
    ya = ya_ref[...]
    cdf = 0.5 * (1.0 + jnp.tanh(math.sqrt(2.0 / math.pi) * (ya + 0.044715 * (ya * ya * ya))))
    glu = _bdot(ya * cdf, wglu_ref[...])
    out_a = glu[:, :D_MODEL] * _sigmoid(glu[:, D_MODEL:])
    z = proj(wz_ref[...])
    out_b = _bdot(_rms(yb_ref[...] * (z * _sigmoid(z)), ssdnw_ref[...]), wssd_ref[...])
    oc = oc_ref[...]
    normed = []
    for h in range(HG_HEADS):
        hl = slice(h * HG_VDIM, (h + 1) * HG_VDIM)
        normed.append(_rms(oc[:, hl], hgnw_ref[:, hl]))
    g = proj(wg_ref[...])
    out_c = _bdot(jnp.concatenate(normed, axis=-1) * (g * _sigmoid(g)), whg_ref[...])
    gate = lambda k: _sigmoid(proj(wgt_ref[:, k * D_MODEL:(k + 1) * D_MODEL]))
    mixed = gate(0) * out_a + gate(1) * out_b + gate(2) * out_c
    x1_ref[...] = x + _bdot(mixed, wo_ref[...])


def _post_call(x2d, ya, yb, oc, lp):
    rows = x2d.shape[0]
    tm = PRE_TM
    seq_in = (x2d, ya, yb, oc)
    consts = (lp['n1'], lp['w_z'], lp['w_g'], lp['w_gt'],
              lp['w_glu'], lp['ssd_nw'], lp['w_ssd'], lp['hg_nw'], lp['w_hg'], lp['w_o'])
    row = lambda c: pl.BlockSpec((tm, c), lambda i: (i, 0))
    return pl.pallas_call(
        _post_kernel,
        grid=(rows // tm,),
        in_specs=[row(a.shape[-1]) for a in seq_in] + [_const_spec(c.shape) for c in consts],
        out_specs=row(D_MODEL),
        out_shape=jax.ShapeDtypeStruct((rows, D_MODEL), F32),
        compiler_params=_params(("parallel",)),
        name="post_merge",
    )(*seq_in, *consts)


def _ffn_kernel(final, seq_rows, tm, x_ref, *rest):
    if seq_rows:
        (sp1_ref, sp2_ref, n2_ref, wup_ref, cw_ref, cb_ref, wdn_ref, nf_ref,
         out_ref, aff_ref, hm_ref) = rest
        t = lax.broadcasted_iota(jnp.int32, (tm, 1), 0) % seq_rows
    else:
        (n2_ref, wup_ref, cw_ref, cb_ref, wdn_ref, nf_ref, out_ref, aff_ref, hm_ref, carry_ref) = rest

        @pl.when(pl.program_id(1) == 0)
        def _():
            carry_ref[...] = jnp.zeros_like(carry_ref)
    tail = FFN_CONV - 1
    x = x_ref[...]
    hb = _rms(x, n2_ref[...]).astype(BF16)
    for j in range(D_FF // FFN_BLK):
        cols = slice(j * FFN_BLK, (j + 1) * FFN_BLK)
        a = jnp.dot(hb, wup_ref[:, cols], preferred_element_type=F32)
        g = jnp.dot(hb, wup_ref[:, D_FF + j * FFN_BLK:D_FF + (j + 1) * FFN_BLK],
                    preferred_element_type=F32)
        p1 = pltpu.roll(a, 1, axis=0)
        p2 = pltpu.roll(a, 2, axis=0)
        if seq_rows:
            p1 = jnp.where(t < 1, sp1_ref[:, cols], p1)
            p2 = jnp.where(t < 2, sp2_ref[:, cols], p2)
            aff_ref[:, cols] = a
        else:
            head = jnp.concatenate([carry_ref[:, cols], a[0:SUBLANE, :]], axis=0)
            p1 = jnp.concatenate([pltpu.roll(head, 1, axis=0)[SUBLANE:], p1[SUBLANE:, :]], axis=0)
            p2 = jnp.concatenate([pltpu.roll(head, 2, axis=0)[SUBLANE:], p2[SUBLANE:, :]], axis=0)
            carry_ref[:, cols] = a[tm - SUBLANE:tm, :]
            aff_ref[0, :, cols] = a[tm - tail:tm, :]
        ac = (cw_ref[0:1, cols] * p2 + cw_ref[1:2, cols] * p1 + cw_ref[2:3, cols] * a
              + cb_ref[:, cols])
        hm_ref[:, cols] = (ac * _sigmoid(ac) * g).astype(BF16)
    y = x + jnp.dot(hm_ref[...], wdn_ref[...], preferred_element_type=F32)
    if final:
        y = _rms(y, nf_ref[...])
    out_ref[...] = y


def _ffn_prompt_call(x1, lp, nf, final, nseq, seqlen):
    tm = FFN_TM
    nchunk = seqlen // tm
    consts = (lp['n2'], lp['w_up'], lp['ffn_cw'], lp['ffn_cb'], lp['w_dn'], nf)
    row = pl.BlockSpec((tm, D_MODEL), lambda b, c: (b * nchunk + c, 0))
    return pl.pallas_call(
        functools.partial(_ffn_kernel, final, 0, tm),
        grid=(nseq, nchunk),
        in_specs=[row] + [_const_spec(c.shape) for c in consts],
        out_specs=[row, pl.BlockSpec((1, FFN_CONV - 1, D_FF), lambda b, c: (b, 0, 0))],
        out_shape=[jax.ShapeDtypeStruct(x1.shape, F32),
                   jax.ShapeDtypeStruct((nseq, FFN_CONV - 1, D_FF), F32)],
        scratch_shapes=[pltpu.VMEM((tm, D_FF), BF16), pltpu.VMEM((SUBLANE, D_FF), F32)],
        compiler_params=_params(("parallel", "arbitrary")),
        name="ffn_prompt",
    )(x1, *consts)


def _ffn_sample_call(x1, sp1, sp2, lp, nf, final):
    rows = x1.shape[0]
    tm = PRE_TM
    consts = (lp['n2'], lp['w_up'], lp['ffn_cw'], lp['ffn_cb'], lp['w_dn'], nf)
    row = lambda c: pl.BlockSpec((tm, c), lambda i: (i, 0))
    return pl.pallas_call(
        functools.partial(_ffn_kernel, final, SAMPLE_PAD, tm),
        grid=(rows // tm,),
        in_specs=[row(D_MODEL), row(D_FF), row(D_FF)] + [_const_spec(c.shape) for c in consts],
        out_specs=[row(D_MODEL), row(D_FF)],
        out_shape=[jax.ShapeDtypeStruct(x1.shape, F32), jax.ShapeDtypeStruct((rows, D_FF), F32)],
        scratch_shapes=[pltpu.VMEM((tm, D_FF), BF16)],
        compiler_params=_params(("arbitrary",)),
        name="ffn_sample",
    )(x1, sp1, sp2, *consts)


def _layer_params(l, lb, norm1_w, w_in, s5_log_dt, s5_lambda_re, s5_lambda_im, s5_b_re, s5_b_im,
                  s5_c_re, s5_c_im, s5_d, s5_w_glu, ssd_conv_w, ssd_conv_b, ssd_dt_bias, ssd_a_log,
                  ssd_d, ssd_norm_w, ssd_w_out, hg_norm_w, hg_w_out, w_o, norm2_w, ffn_w_up,
                  ffn_conv_w, ffn_conv_b, ffn_w_down):
    row = lambda v: v.astype(F32).reshape(1, -1)
    lp = {}
    lp['n1'] = row(norm1_w[l])
    widths = (S5_WIDTH, SSD_INNER, SSD_CONV_DIM, SSD_HEADS, HG_WIDTH, HG_WIDTH, HG_WIDTH, HG_WIDTH,
              3 * D_MODEL)
    names = ('w_ua', 'w_z', 'w_xbc', 'w_dt', 'w_q', 'w_f', 'w_i', 'w_g', 'w_gt')
    off = 0
    for name, w in zip(names, widths):
        lp[name] = w_in[l][:, off:off + w].astype(BF16)
        off += w
    lp['w_dt'] = jnp.pad(lp['w_dt'], ((0, 0), (0, LANE - SSD_HEADS)))
    lp['dtb'] = jnp.pad(row(ssd_dt_bias[l]), ((0, 0), (0, LANE - SSD_HEADS)))
    lp['loglb'] = jnp.log(row(lb))
    lp['log1mlb'] = jnp.log1p(-row(lb))
    lp['omlb'] = 1.0 - row(lb)

    delta = jnp.exp(s5_log_dt[l].astype(F32))[:, None]
    lr = s5_lambda_re[l].astype(F32)
    li = s5_lambda_im[l].astype(F32)
    mag = jnp.exp(lr * delta)
    ab_re = mag * jnp.cos(li * delta)
    ab_im = mag * jnp.sin(li * delta)
    den = lr * lr + li * li
    nr = ab_re - 1.0
    co_re = (nr * lr + ab_im * li) / den
    co_im = (ab_im * lr - nr * li) / den
    br = s5_b_re[l].astype(F32)
    bi = s5_b_im[l].astype(F32)
    bb_re = co_re[..., None] * br - co_im[..., None] * bi
    bb_im = co_re[..., None] * bi + co_im[..., None] * br
    eye = jnp.eye(S5_GROUPS, dtype=F32)
    bd_in = lambda m: jnp.einsum('gnj,gh->gjhn', m, eye).reshape(S5_WIDTH, S5_LANES)
    bd_out = lambda m: jnp.einsum('gjn,gh->gnhj', m, eye).reshape(S5_LANES, S5_WIDTH)
    lp['s5_bb'] = jnp.concatenate([bd_in(bb_re), bd_in(bb_im)], axis=1).astype(BF16)
    lp['s5_cc'] = jnp.concatenate([bd_out(s5_c_re[l].astype(F32)),
                                   -bd_out(s5_c_im[l].astype(F32))], axis=0).astype(BF16)
    pw = jnp.asarray(list(range(1, S5_R + 1)) + [2 * S5_R, 4 * S5_R], F32)[:, None, None]
    pm = jnp.exp(lr * delta * pw)
    ph = li * delta * pw
    pad = ((0, 40 - (S5_R + 2)), (0, 0))
    lp['s5_apr'] = jnp.pad((pm * jnp.cos(ph)).reshape(-1, S5_LANES), pad)
    lp['s5_api'] = jnp.pad((pm * jnp.sin(ph)).reshape(-1, S5_LANES), pad)
    lp['s5_apr8'] = jnp.repeat(lp['s5_apr'][:S5_R + 2], S5_CH, axis=0)
    lp['s5_api8'] = jnp.repeat(lp['s5_api'][:S5_R + 2], S5_CH, axis=0)
    npair = S5_LANES // LANE
    pair_cols = lambda m_re, m_im: jnp.stack(
        [m_re.reshape(-1, npair, LANE), m_im.reshape(-1, npair, LANE)], axis=2).reshape(-1, 2 * S5_LANES)
    bb_int = pair_cols(bd_in(bb_re), bd_in(bb_im)).astype(BF16)
    cc_int = pair_cols(bd_out(s5_c_re[l].astype(F32)).T, -bd_out(s5_c_im[l].astype(F32)).T).T.astype(BF16)
    kw = 2 * LANE
    nlc = 2 * S5_LANES // S5_LC
    ktile = lambda lc: lc * S5_LC // (2 * S5_LANES // (S5_WIDTH // kw))
    lp['s5_bbc'] = jnp.stack([bb_int[ktile(lc) * kw:(ktile(lc) + 1) * kw, lc * S5_LC:(lc + 1) * S5_LC]
                              for lc in range(nlc)])
    lp['s5_ccc'] = jnp.stack([cc_int[lc * S5_LC:(lc + 1) * S5_LC, ktile(lc) * kw:(ktile(lc) + 1) * kw]
                              for lc in range(nlc)])
    lp['s5_d'] = row(s5_d[l])
    lp['w_glu'] = s5_w_glu[l].astype(BF16)

    lp['conv_w'] = ssd_conv_w[l].astype(F32)
    lp['conv_b'] = row(ssd_conv_b[l])
    lp['ssd_arow'] = jnp.pad(-jnp.exp(row(ssd_a_log[l])), ((0, 0), (0, LANE - SSD_HEADS)))
    lp['ssd_dexp'] = jnp.repeat(row(ssd_d[l]), SSD_HEAD_DIM, axis=1)
    ex = np.zeros((LANE, SSD_INNER), np.float32)
    for h in range(SSD_HEADS):
        ex[h, h * SSD_HEAD_DIM:(h + 1) * SSD_HEAD_DIM] = 1.0
    lp['ssd_expand'] = jnp.asarray(ex, BF16)
    lp['ssd_nw'] = row(ssd_norm_w[l])
    lp['w_ssd'] = ssd_w_out[l].astype(BF16)
    lp['hg_nw'] = row(hg_norm_w[l])
    lp['w_hg'] = hg_w_out[l].astype(BF16)
    lp['w_o'] = w_o[l].astype(BF16)
    lp['n2'] = row(norm2_w[l])
    lp['w_up'] = ffn_w_up[l].astype(BF16)
    lp['ffn_cw'] = ffn_conv_w[l].astype(F32)
    lp['ffn_cb'] = row(ffn_conv_b[l])
    lp['w_dn'] = ffn_w_down[l].astype(BF16)
    return lp


_PRE_KEYS = ('ua', 'xbc', 'dt', 'q', 'lf', 'kk', 'v')


def kernel(x_prompt, x_sample, state_s5_re, state_s5_im, state_ssd, state_ssd_conv, state_hgrn, state_ffn_conv, norm1_w, w_in, s5_log_dt, s5_lambda_re, s5_lambda_im, s5_b_re, s5_b_im, s5_c_re, s5_c_im, s5_d, s5_w_glu, ssd_conv_w, ssd_conv_b, ssd_dt_bias, ssd_a_log, ssd_d, ssd_norm_w, ssd_w_out, hg_lb_logits, hg_norm_w, hg_w_out, w_o, norm2_w, ffn_w_up, ffn_conv_w, ffn_conv_b, ffn_w_down, norm_f_w):
    nb_p, len_p, _ = x_prompt.shape
    nb_s, len_s, _ = x_sample.shape
    assert len_p % S5_T == 0 and len_p % MIX_T == 0 and len_p % PRE_TM == 0 and len_p % FFN_TM == 0
    assert len_s <= SAMPLE_PAD and len_s >= SSD_CONV - 1
    assert (nb_s * SAMPLE_PAD) % PRE_TM == 0 and (nb_s * SAMPLE_PAD) % FFN_TM == 0
    lb_cum = jnp.cumsum(jax.nn.softmax(hg_lb_logits.astype(F32), axis=0), axis=0)
    lb_all = lb_cum - lb_cum[0:1]
    nf = norm_f_w.astype(F32).reshape(1, -1)

    xp = x_prompt.astype(F32).reshape(nb_p * len_p, D_MODEL)
    xs = jnp.pad(x_sample.astype(F32), ((0, 0), (0, SAMPLE_PAD - len_s), (0, 0)))
    xs = xs.reshape(nb_s * SAMPLE_PAD, D_MODEL)
    zeros_p = (jnp.zeros((1, nb_p, SSD_INNER, SSD_STATE), F32),
               jnp.zeros((1, nb_p, SSD_CONV - 1, SSD_CONV_DIM), F32),
               jnp.zeros((1, nb_p, HG_WIDTH, HG_VDIM), F32))
    states_s = (state_ssd.astype(F32).reshape(DEPTH, nb_s, SSD_INNER, SSD_STATE),
                state_ssd_conv.astype(F32),
                state_hgrn.astype(F32).reshape(DEPTH, nb_s, HG_WIDTH, HG_VDIM))
    mix_p = None
    mix_s = None
    new_p = []
    new_s = []
    for l in range(DEPTH):
        lp = _layer_params(l, lb_all[l], norm1_w, w_in, s5_log_dt, s5_lambda_re, s5_lambda_im,
                           s5_b_re, s5_b_im, s5_c_re, s5_c_im, s5_d, s5_w_glu, ssd_conv_w,
                           ssd_conv_b, ssd_dt_bias, ssd_a_log, ssd_d, ssd_norm_w, ssd_w_out,
                           hg_norm_w, hg_w_out, w_o, norm2_w, ffn_w_up, ffn_conv_w, ffn_conv_b,
                           ffn_w_down)
        final = l == DEPTH - 1

        pre = dict(zip(_PRE_KEYS, _pre_call(xp, lp)))
        ya, p_re, p_im = _s5_prompt_call(pre['ua'], lp, nb_p, len_p)
        yb, oc, *mix_p = _mix_call(pre, zeros_p, 0, l, mix_p, lp, nb_p, len_p, MIX_T, MIX_T, 1,
                                   SSD_Q, HG_Q)
        x1 = _post_call(xp, ya, yb.reshape(-1, SSD_INNER), oc.reshape(-1, HG_WIDTH), lp)
        xp, p_ffn = _ffn_prompt_call(x1, lp, nf, final, nb_p, len_p)
        new_p.append((p_re.reshape(nb_p, S5_GROUPS, S5_STATE), p_im.reshape(nb_p, S5_GROUPS, S5_STATE),
                      p_ffn))

        pre = dict(zip(_PRE_KEYS, _pre_call(xs, lp)))
        ua_tm = pre['ua'].reshape(nb_s, SAMPLE_PAD, S5_WIDTH)[:, :len_s].transpose(1, 0, 2)
        ya_tm, s_re, s_im = _s5_sample_call(
            ua_tm.reshape(len_s * nb_s, S5_WIDTH),
            state_s5_re[l].astype(F32).reshape(nb_s, S5_LANES),
            state_s5_im[l].astype(F32).reshape(nb_s, S5_LANES), lp, len_s, nb_s)
        ya = jnp.pad(ya_tm.reshape(len_s, nb_s, S5_WIDTH).transpose(1, 0, 2),
                     ((0, 0), (0, SAMPLE_PAD - len_s), (0, 0))).reshape(nb_s * SAMPLE_PAD, S5_WIDTH)
        yb, oc, *mix_s = _mix_call(pre, states_s, l, l, mix_s, lp, nb_s, SAMPLE_PAD, SAMPLE_PAD, len_s,
                                   MIX_NB_SAMPLE, SAMPLE_PAD, SAMPLE_PAD)
        x1 = _post_call(xs, ya, yb.reshape(-1, SSD_INNER), oc.reshape(-1, HG_WIDTH), lp)
        st = state_ffn_conv[l].astype(F32)
        zrow = jnp.zeros((nb_s, SAMPLE_PAD - 2, D_FF), F32)
        sp1 = jnp.concatenate([st[:, 1:2], st[:, 1:2], zrow], axis=1).reshape(-1, D_FF)
        sp2 = jnp.concatenate([st[:, 0:1], st[:, 1:2], zrow], axis=1).reshape(-1, D_FF)
        xs, aff = _ffn_sample_call(x1, sp1, sp2, lp, nf, final)
        s_ffn = aff.reshape(nb_s, SAMPLE_PAD, D_FF)[:, len_s - (FFN_CONV - 1):len_s]
        new_s.append((s_re.reshape(nb_s, S5_GROUPS, S5_STATE), s_im.reshape(nb_s, S5_GROUPS, S5_STATE),
                      s_ffn))

    stk = lambda lst, i: jnp.stack([s[i] for s in lst])
    y_prompt = xp.reshape(nb_p, len_p, D_MODEL)
    y_sample = xs.reshape(nb_s, SAMPLE_PAD, D_MODEL)[:, :len_s]

    def mix_states(m, n):
        ssd, conv, hg = m
        return (ssd.reshape(DEPTH, n, SSD_HEADS, SSD_HEAD_DIM, SSD_STATE), conv,
                hg.reshape(DEPTH, n, HG_HEADS, HG_KDIM, HG_VDIM))

    p_ssd, p_conv, p_hg = mix_states(mix_p, nb_p)
    s_ssd, s_conv, s_hg = mix_states(mix_s, nb_s)
    return (y_prompt, y_sample,
            stk(new_p, 0), stk(new_p, 1), p_ssd, p_conv, p_hg, stk(new_p, 2),
            stk(new_s, 0), stk(new_s, 1), s_ssd, s_conv, s_hg, stk(new_s, 2))
```

```python
import functools
import math

import jax
import jax.numpy as jnp
import numpy as np
from jax import lax
from jax.experimental import pallas as pl
from jax.experimental.pallas import tpu as pltpu

F32 = jnp.float32
BF16 = jnp.bfloat16

D_MODEL = 1024
DEPTH = 2
S5_WIDTH = 512
S5_GROUP = 16
S5_GROUPS = 32
S5_STATE = 64
S5_LANES = S5_GROUPS * S5_STATE
SSD_INNER = 1024
SSD_HEAD_DIM = 64
SSD_HEADS = 16
SSD_GROUPS = 4
SSD_STATE = 128
SSD_CONV = 4
SSD_CONV_DIM = 2048
HG_WIDTH = 512
HG_HEADS = 4
HG_KDIM = 128
HG_VDIM = 128
D_FF = 2816
FFN_CONV = 3
EPS = 1e-6

LANE = 128
SUBLANE = 8
SAMPLE_PAD = 8
HG_SAFE_LOG = -80.0

PRE_TM = 256
FFN_TM = 512
FFN_BLK = 256
S5_T = 256
S5_CH = 8
S5_R = S5_T // S5_CH
S5_LC = 512
MIX_T = 128
SSD_Q = 128
HG_Q = 64
MIX_NB_PROMPT = 2
MIX_NB_SAMPLE = 8
VMEM_LIMIT = 56 * 1024 * 1024


def _const_spec(shape):
    nd = len(shape)
    return pl.BlockSpec(shape, lambda *_: (0,) * nd, pipeline_mode=pl.Buffered(1))


def _params(sem):
    return pltpu.CompilerParams(dimension_semantics=sem, vmem_limit_bytes=VMEM_LIMIT)


def _bdot(a, b):
    return jnp.dot(a.astype(BF16), b.astype(BF16), preferred_element_type=F32)


def _bdot_nt(a, b):
    return lax.dot_general(a.astype(BF16), b.astype(BF16), (((1,), (1,)), ((), ())),
                           preferred_element_type=F32)


def _bdot_tn(a, b):
    return lax.dot_general(a.astype(BF16), b.astype(BF16), (((0,), (0,)), ((), ())),
                           preferred_element_type=F32)


def _split3(x):
    h = x.astype(BF16)
    r = x - h.astype(F32)
    m = r.astype(BF16)
    l = (r - m.astype(F32)).astype(BF16)
    return h, m, l


def _sel_left(m01, x):
    h, m, l = _split3(x)
    d = lambda p: jnp.dot(m01, p, preferred_element_type=F32)
    return (d(h) + d(m)) + d(l)


def _sel_right(x, m01):
    h, m, l = _split3(x)
    d = lambda p: jnp.dot(p, m01, preferred_element_type=F32)
    return (d(h) + d(m)) + d(l)


def _sigmoid(x):
    return 1.0 / (1.0 + jnp.exp(-x))


def _softplus(x):
    return jnp.maximum(x, 0.0) + jnp.log1p(jnp.exp(-jnp.abs(x)))


def _rms(x, w):
    return x * lax.rsqrt(jnp.mean(x * x, axis=-1, keepdims=True) + EPS) * w


def _pre_kernel(x_ref, n1_ref, w_ua, w_xbc, w_dt, w_q, w_f, w_i,
                dtb_ref, loglb_ref, log1mlb_ref, omlb_ref,
                ua_o, xbc_o, dt_o, q_o, lf_o, kk_o, v_o):
    hb = _rms(x_ref[...], n1_ref[...]).astype(BF16)
    dot = lambda w: jnp.dot(hb, w[...], preferred_element_type=F32)
    ua_o[...] = dot(w_ua)
    xbc_o[...] = dot(w_xbc)
    dt_o[...] = _softplus(dot(w_dt) + dtb_ref[...])
    q_o[...] = dot(w_q)
    zf = dot(w_f)
    log_sig = jnp.minimum(zf, 0.0) - jnp.log1p(jnp.exp(-jnp.abs(zf)))
    a = loglb_ref[...]
    b = log1mlb_ref[...] + log_sig
    lf_o[...] = jnp.maximum(a, b) + jnp.log1p(jnp.exp(-jnp.abs(a - b)))
    kk_o[...] = omlb_ref[...] * _sigmoid(-zf)
    v_o[...] = dot(w_i)


def _pre_call(x2d, lp):
    rows = x2d.shape[0]
    tm = PRE_TM
    row = lambda c: pl.BlockSpec((tm, c), lambda i: (i, 0))
    widths = (S5_WIDTH, SSD_CONV_DIM, LANE, HG_WIDTH, HG_WIDTH, HG_WIDTH, HG_WIDTH)
    consts = (lp['n1'], lp['w_ua'], lp['w_xbc'], lp['w_dt'], lp['w_q'], lp['w_f'], lp['w_i'],
              lp['dtb'], lp['loglb'], lp['log1mlb'], lp['omlb'])
    return pl.pallas_call(
        _pre_kernel,
        grid=(rows // tm,),
        in_specs=[row(D_MODEL)] + [_const_spec(c.shape) for c in consts],
        out_specs=[row(c) for c in widths],
        out_shape=[jax.ShapeDtypeStruct((rows, c), F32) for c in widths],
        compiler_params=_params(("parallel",)),
        name="pre_proj",
    )(x2d, *consts)


def _cmul_add(ar, ai, xr, xi, br, bi):
    return ar * xr - ai * xi + br, ar * xi + ai * xr + bi


def _s5_prompt_kernel(ua_ref, perm_ref, permt_ref, bbc_ref, ccc_ref, apr_ref, api_ref, d_ref,
                      ya_ref, st_ref, bu_ref):
    @pl.when(pl.program_id(1) == 0)
    def _():
        st_ref[...] = jnp.zeros_like(st_ref)

    ua = ua_ref[...]
    up = jnp.dot(perm_ref[...], ua.astype(BF16), preferred_element_type=F32).astype(BF16)
    rowid = lax.broadcasted_iota(jnp.int32, (S5_CH, LANE), 0)
    tab = lambda ref, k, gp: ref[k * S5_CH:(k + 1) * S5_CH, gp * LANE:(gp + 1) * LANE]
    kw = 2 * LANE
    y_acc = [None] * (S5_WIDTH // kw)

    for lc in range(2 * S5_LANES // S5_LC):
        kt = lc * S5_LC // (2 * S5_LANES // (S5_WIDTH // kw))
        cols = slice(lc * S5_LC, (lc + 1) * S5_LC)
        bu_ref[:, cols] = jnp.dot(up[:, kt * kw:(kt + 1) * kw], bbc_ref[lc], preferred_element_type=F32)
        for gp in range(lc * S5_LC // kw, (lc + 1) * S5_LC // kw):
            re = slice(gp * kw, gp * kw + LANE)
            im = slice(gp * kw + LANE, (gp + 1) * kw)
            a_r, a_i = tab(apr_ref, 0, gp), tab(api_ref, 0, gp)
            sr = jnp.zeros((S5_CH, LANE), F32)
            si = jnp.zeros((S5_CH, LANE), F32)
            for r in range(S5_R):
                rows = slice(r * S5_CH, (r + 1) * S5_CH)
                sr, si = _cmul_add(a_r, a_i, sr, si, bu_ref[rows, re], bu_ref[rows, im])
                bu_ref[rows, re] = sr
                bu_ref[rows, im] = si
            pr = jnp.broadcast_to(st_ref[0, :, re], (S5_CH, LANE))
            pi = jnp.broadcast_to(st_ref[0, :, im], (S5_CH, LANE))
            vr = jnp.where(rowid == 0, pr, pltpu.roll(sr, 1, axis=0))
            vi = jnp.where(rowid == 0, pi, pltpu.roll(si, 1, axis=0))
            for d, k in ((1, S5_R - 1), (2, S5_R), (4, S5_R + 1)):
                keep = rowid >= d
                tr = jnp.where(keep, pltpu.roll(vr, d, axis=0), 0.0)
                ti = jnp.where(keep, pltpu.roll(vi, d, axis=0), 0.0)
                vr, vi = _cmul_add(tab(apr_ref, k, gp), tab(api_ref, k, gp), tr, ti, vr, vi)
            for r in range(S5_R):
                rows = slice(r * S5_CH, (r + 1) * S5_CH)
                xr, xi = _cmul_add(tab(apr_ref, r, gp), tab(api_ref, r, gp), vr, vi,
                                   bu_ref[rows, re], bu_ref[rows, im])
                bu_ref[rows, re] = xr
                bu_ref[rows, im] = xi
            st_ref[0, :, re] = xr[S5_CH - 1:S5_CH, :]
            st_ref[0, :, im] = xi[S5_CH - 1:S5_CH, :]
        part = _bdot(bu_ref[:, cols], ccc_ref[lc])
        y_acc[kt] = part if y_acc[kt] is None else y_acc[kt] + part

    yp = jnp.concatenate(y_acc, axis=1)
    ya_ref[...] = _sel_left(permt_ref[...], yp) + d_ref[...] * ua


def _s5_perm():
    p = np.zeros((S5_T, S5_T), np.float32)
    for i in range(S5_T):
        p[i, (i % S5_CH) * S5_R + i // S5_CH] = 1.0
    return jnp.asarray(p, BF16), jnp.asarray(p.T, BF16)


def _s5_prompt_call(ua, lp, nseq, seqlen):
    nchunk = seqlen // S5_T
    perm, permt = _s5_perm()
    consts = (perm, permt, lp['s5_bbc'], lp['s5_ccc'], lp['s5_apr8'], lp['s5_api8'], lp['s5_d'])
    st_spec = pl.BlockSpec((1, 1, 2 * S5_LANES), lambda b, c: (b, 0, 0))
    st_shape = jax.ShapeDtypeStruct((nseq, 1, 2 * S5_LANES), F32)
    ya, st = pl.pallas_call(
        _s5_prompt_kernel,
        grid=(nseq, nchunk),
        in_specs=[pl.BlockSpec((S5_T, S5_WIDTH), lambda b, c: (b * nchunk + c, 0))]
        + [_const_spec(c.shape) for c in consts],
        out_specs=[pl.BlockSpec((S5_T, S5_WIDTH), lambda b, c: (b * nchunk + c, 0)), st_spec],
        out_shape=[jax.ShapeDtypeStruct(ua.shape, F32), st_shape],
        scratch_shapes=[pltpu.VMEM((S5_T, 2 * S5_LANES), F32)],
        compiler_params=_params(("parallel", "arbitrary")),
        name="s5_prompt",
    )(ua, *consts)
    st = st.reshape(nseq, S5_LANES // LANE, 2, LANE)
    return ya, st[:, :, 0].reshape(nseq, S5_LANES), st[:, :, 1].reshape(nseq, S5_LANES)


def _s5_sample_kernel(nstep, nseq, ua_ref, sre0_ref, sim0_ref, bb_ref, cc_ref, apr_ref, api_ref,
                      d_ref, ya_ref, sre_ref, sim_ref, bu_ref, xb_ref):
    ua = ua_ref[...]
    bu_ref[...] = jnp.dot(ua.astype(BF16), bb_ref[...], preferred_element_type=F32)
    for lc in range(S5_LANES // S5_LC):
        re = slice(lc * S5_LC, (lc + 1) * S5_LC)
        im = slice(S5_LANES + lc * S5_LC, S5_LANES + (lc + 1) * S5_LC)
        a_r = apr_ref[0:1, re]
        a_i = api_ref[0:1, re]
        sr = sre0_ref[:, re]
        si = sim0_ref[:, re]
        for t in range(nstep):
            rows = slice(t * nseq, (t + 1) * nseq)
            sr, si = _cmul_add(a_r, a_i, sr, si, bu_ref[rows, re], bu_ref[rows, im])
            xb_ref[rows, re] = sr.astype(BF16)
            xb_ref[rows, im] = si.astype(BF16)
        sre_ref[:, re] = sr
        sim_ref[:, re] = si
    ya_ref[...] = jnp.dot(xb_ref[...], cc_ref[...], preferred_element_type=F32) + d_ref[...] * ua


def _s5_sample_call(ua_tm, sre0, sim0, lp, nstep, nseq):
    rows = nstep * nseq
    args = (ua_tm, sre0, sim0, lp['s5_bb'], lp['s5_cc'], lp['s5_apr'], lp['s5_api'], lp['s5_d'])
    st_shape = jax.ShapeDtypeStruct((nseq, S5_LANES), F32)
    full = lambda s: pl.BlockSpec(s, lambda i: (0,) * len(s))
    return pl.pallas_call(
        functools.partial(_s5_sample_kernel, nstep, nseq),
        grid=(1,),
        in_specs=[full(a.shape) for a in args],
        out_specs=[full((rows, S5_WIDTH)), full((nseq, S5_LANES)), full((nseq, S5_LANES))],
        out_shape=[jax.ShapeDtypeStruct((rows, S5_WIDTH), F32), st_shape, st_shape],
        scratch_shapes=[pltpu.VMEM((rows, 2 * S5_LANES), F32), pltpu.VMEM((rows, 2 * S5_LANES), BF16)],
        compiler_params=_params(("arbitrary",)),
        name="s5_sample",
    )(*args)


def _mix_kernel(t_rows, n_valid, nb, ssd_q, hg_q, n_alias,
                xbc_ref, dt_ref, q_ref, lf_ref, kk_ref, v_ref, ssd0_ref, conv0_ref, hg0_ref,
                cw_ref, cb_ref, arow_ref, dexp_ref, ltri_s_ref, ltri_h_ref, *rest):
    (yb_ref, oc_ref, ssd_ref, conv_ref, hg_ref,
     xpad_ref, xc_ref, hb_ref, hk_ref, ho_ref, hq_ref, hv_ref) = rest[n_alias:]

    @pl.when(pl.program_id(1) == 0)
    def _():
        ssd_ref[...] = ssd0_ref[...]
        conv_ref[...] = conv0_ref[...]
        hg_ref[...] = hg0_ref[...]

    n_rows = nb * t_rows
    if n_valid < t_rows:
        valid_all = (lax.broadcasted_iota(jnp.int32, (n_rows, 1), 0) % t_rows < n_valid).astype(F32)
        valid = valid_all[0:t_rows, :]
    else:
        valid_all = valid = None
    tail = SSD_CONV - 1
    sq = lambda m: m if m.shape[0] == LANE else jnp.concatenate(
        [m, jnp.zeros((LANE - m.shape[0], LANE), F32)], axis=0)

    def conv_stage(j):
        xpad_ref[j] = jnp.zeros((SUBLANE, SSD_CONV_DIM), F32)
        xpad_ref[j, SUBLANE - tail:SUBLANE, :] = conv_ref[j]
        for cblk in range(SSD_CONV_DIM // LANE):
            cols = slice(cblk * LANE, (cblk + 1) * LANE)
            x = jnp.concatenate([xpad_ref[j, :, cols], xbc_ref[j, :, cols]], axis=0)
            acc = cw_ref[0:1, cols] * x
            for k in range(1, SSD_CONV):
                acc = pltpu.roll(acc, 1, axis=0) + cw_ref[k:k + 1, cols] * x
            acc = acc[SUBLANE:, :] + cb_ref[:, cols]
            xc_ref[j, :, cols] = acc * _sigmoid(acc)
        conv_ref[j] = xbc_ref[j, n_valid - tail:n_valid, :]

    def ssd_chunk(j, r0):
        q_ = ssd_q
        rows = slice(r0, r0 + q_)
        dt = dt_ref[j, rows, :]
        if valid is not None:
            dt = dt * valid[rows, :]
        acum = _sel_left(ltri_s_ref[...], dt * arow_ref[...])
        acum_t = sq(acum).T
        dt_t = sq(dt).T
        alast = acum[q_ - 1:q_, :]
        eacm = jnp.exp(acum)
        wcol = dt * jnp.exp(alast - acum)
        ela_t = jnp.exp(jnp.broadcast_to(acum_t[:, q_ - 1:q_], (LANE, LANE)))
        tri = lax.broadcasted_iota(jnp.int32, (q_, q_), 0) >= lax.broadcasted_iota(jnp.int32, (q_, q_), 1)
        first_head = lax.broadcasted_iota(jnp.int32, (q_, 2 * SSD_HEAD_DIM), 1) < SSD_HEAD_DIM
        hpg = SSD_HEADS // SSD_GROUPS
        gw = hpg * SSD_HEAD_DIM
        b0 = SSD_INNER
        c0 = SSD_INNER + SSD_GROUPS * SSD_STATE
        for g in range(SSD_GROUPS):
            cg = xc_ref[j, rows, c0 + g * SSD_STATE:c0 + (g + 1) * SSD_STATE]
            bgb = xc_ref[j, rows, b0 + g * SSD_STATE:b0 + (g + 1) * SSD_STATE].astype(BF16)
            cb = _bdot_nt(cg, bgb)
            sg = ssd_ref[j, g * gw:(g + 1) * gw, :]
            sgb = sg.astype(BF16)
            scale = []
            xw = []
            for r in range(hpg):
                h = g * hpg + r
                hl = slice(h * SSD_HEAD_DIM, (h + 1) * SSD_HEAD_DIM)
                xs_h = xc_ref[j, rows, hl]
                seg = acum[:, h:h + 1] - acum_t[h:h + 1, :q_]
                dec = jnp.exp(jnp.where(tri, seg, -jnp.inf))
                gmat = cb * dec * dt_t[h:h + 1, :q_]
                ce = cg * eacm[:, h:h + 1]
                yb_ref[j, rows, hl] = (_bdot(gmat, xs_h)
                                       + _bdot_nt(ce, sgb[r * SSD_HEAD_DIM:(r + 1) * SSD_HEAD_DIM, :])
                                       + dexp_ref[:, hl] * xs_h)
                scale.append(jnp.broadcast_to(ela_t[h:h + 1, :], (SSD_HEAD_DIM, SSD_STATE)))
                if r % 2 == 0:
                    pl_ = slice(h * SSD_HEAD_DIM, (h + 2) * SSD_HEAD_DIM)
                    xw.append(xc_ref[j, rows, pl_] * jnp.where(first_head, wcol[:, h:h + 1], wcol[:, h + 1:h + 2]))
            upd = _bdot_tn(jnp.concatenate(xw, axis=1), bgb)
            ssd_ref[j, g * gw:(g + 1) * gw, :] = sg * jnp.concatenate(scale, axis=0) + upd
            yield

    def hg_prepare():
        flat = lambda ref: ref[...].reshape(n_rows, HG_WIDTH)
        lf = flat(lf_ref)
        kk = flat(kk_ref)
        if valid_all is not None:
            lf = lf * valid_all
            kk = kk * valid_all
        bc = _sel_left(ltri_h_ref[...], lf)
        hb_ref[...] = bc
        hk_ref[...] = kk
        safe = jnp.min(bc) >= HG_SAFE_LOG

        @pl.when(safe)
        def _():
            ho_ref[...] = jnp.zeros((n_rows, HG_WIDTH), F32)

        @pl.when(jnp.logical_not(safe))
        def _():
            ho_ref[...] = jnp.zeros((n_rows, HG_WIDTH), F32)
            hq_ref[...] = flat(q_ref)
            hv_ref[...] = flat(v_ref)
            rid = lax.broadcasted_iota(jnp.int32, (n_rows, 1), 0)

            def body(s, c):
                brow = hb_ref[pl.ds(s, 1), :]
                krow = hk_ref[pl.ds(s, 1), :]
                vrow = hv_ref[pl.ds(s, 1), :]
                w = hq_ref[...] * krow * jnp.exp(jnp.minimum(hb_ref[...] - brow, 0.0))
                same_chunk_end = (s // hg_q + 1) * hg_q
                w = jnp.where(jnp.logical_and(rid >= s, rid < same_chunk_end), w, 0.0)
                for h in range(HG_HEADS):
                    hl = slice(h * HG_KDIM, (h + 1) * HG_KDIM)
                    a = jnp.sum(w[:, hl], axis=-1, keepdims=True)
                    ho_ref[:, hl] = ho_ref[:, hl] + a * vrow[:, hl]
                return c

            lax.fori_loop(0, n_rows, body, 0)

        return safe

    def hg_chunk(j, r0, safe):
        q_ = hg_q
        rows = slice(r0, r0 + q_)
        brows = slice(j * t_rows + r0, j * t_rows + r0 + q_)
        qq = q_ref[j, rows, :]
        vv = v_ref[j, rows, :]
        bc = hb_ref[brows, :]
        kk = hk_ref[brows, :]
        blast = bc[q_ - 1:q_, :]
        qe = qq * jnp.exp(bc)
        kd = kk * jnp.exp(blast - bc)
        ke = kk * jnp.exp(-bc)
        tri = lax.broadcasted_iota(jnp.int32, (q_, q_), 0) >= lax.broadcasted_iota(jnp.int32, (q_, q_), 1)
        eb = jnp.exp(blast)
        for h in range(HG_HEADS):
            hl = slice(h * HG_KDIM, (h + 1) * HG_KDIM)
            att = jnp.where(tri, _bdot_nt(qe[:, hl], ke[:, hl]), 0.0)
            intra = jnp.where(safe, _bdot(att, vv[:, hl]), ho_ref[brows, hl])
            sh = hg_ref[j, hl, :]
            oc_ref[j, rows, hl] = intra + _bdot(qe[:, hl], sh)
            eb_t = jnp.broadcast_to(eb[:, hl], (HG_KDIM, HG_KDIM)).T
            hg_ref[j, hl, :] = sh * eb_t + _bdot_tn(kd[:, hl], vv[:, hl])
            yield

    def ssd_seq(j):
        for c in range(t_rows // ssd_q):
            yield from ssd_chunk(j, c * ssd_q)

    def hg_seq(j, safe):
        for c in range(t_rows // hg_q):
            yield from hg_chunk(j, c * hg_q, safe)

    safe = hg_prepare()
    for j in range(nb):
        conv_stage(j)
    tasks = [t for j in range(nb) for t in (ssd_seq(j), hg_seq(j, safe))]
    while tasks:
        for t in list(tasks):
            try:
                next(t)
            except StopIteration:
                tasks.remove(t)


def _ltri(n, q):
    return jnp.asarray(np.kron(np.eye(n // q, dtype=np.float32), np.tril(np.ones((q, q), np.float32))), BF16)


def _mix_call(pre, states, l_in, l_out, prev, lp, nseq, seqlen, t_rows, n_valid, nb, ssd_q, hg_q):
    nchunk = seqlen // t_rows
    view = lambda a: a.reshape(nseq, seqlen, a.shape[-1])
    seq_in = [view(pre[k]) for k in ('xbc', 'dt', 'q', 'lf', 'kk', 'v')]
    consts = (lp['conv_w'], lp['conv_b'], lp['ssd_arow'], lp['ssd_dexp'],
              _ltri(ssd_q, ssd_q), _ltri(nb * t_rows, hg_q))
    tile = lambda c: pl.BlockSpec((nb, t_rows, c), lambda i, k: (i, k, 0))
    st = lambda a, l: pl.BlockSpec((None, nb) + a.shape[2:], lambda i, k: (l, i, 0, 0))
    prev = () if prev is None else tuple(prev)
    n_in = len(seq_in) + len(states) + len(consts)
    kern = functools.partial(_mix_kernel, t_rows, n_valid, nb, ssd_q, hg_q, len(prev))
    return pl.pallas_call(
        kern,
        grid=(nseq // nb, nchunk),
        in_specs=[tile(a.shape[-1]) for a in seq_in] + [st(a, l_in) for a in states]
        + [_const_spec(c.shape) for c in consts]
        + [pl.BlockSpec(memory_space=pl.ANY) for _ in prev],
        out_specs=[tile(SSD_INNER), tile(HG_WIDTH)] + [st(a, l_out) for a in states],
        out_shape=[jax.ShapeDtypeStruct((nseq, seqlen, SSD_INNER), F32),
                   jax.ShapeDtypeStruct((nseq, seqlen, HG_WIDTH), F32)]
        + [jax.ShapeDtypeStruct((DEPTH,) + a.shape[1:], F32) for a in states],
        input_output_aliases={n_in + k: 2 + k for k in range(len(prev))},
        scratch_shapes=[pltpu.VMEM((nb, SUBLANE, SSD_CONV_DIM), F32),
                        pltpu.VMEM((nb, t_rows, SSD_CONV_DIM), F32)]
        + [pltpu.VMEM((nb * t_rows, HG_WIDTH), F32) for _ in range(5)],
        compiler_params=_params(("parallel", "arbitrary")),
        name="ssd_hgrn_mix",
    )(*seq_in, *states, *consts, *prev)


def _post_kernel(x_ref, ya_ref, yb_ref, oc_ref, n1_ref, wz_ref, wg_ref, wgt_ref,
                 wglu_ref, ssdnw_ref, wssd_ref, hgnw_ref, whg_ref, wo_ref, x1_ref):
    x = x_ref[...]
    hb = _rms(x, n1_ref[...]).astype(BF16)
    proj = lambda w: jnp.dot(hb, w, preferred_element_type=F32)
    ya = ya_ref[...]
    cdf = 0.5 * (1.0 + jnp.tanh(math.sqrt(2.0 / math.pi) * (ya + 0.044715 * (ya * ya * ya))))
    glu = _bdot(ya * cdf, wglu_ref[...])
    out_a = glu[:, :D_MODEL] * _sigmoid(glu[:, D_MODEL:])
    z = proj(wz_ref[...])
    out_b = _bdot(_rms(yb_ref[...] * (z * _sigmoid(z)), ssdnw_ref[...]), wssd_ref[...])
    oc = oc_ref[...]
    normed = []
    for h in range(HG_HEADS):
        hl = slice(h * HG_VDIM, (h + 1) * HG_VDIM)
        normed.append(_rms(oc[:, hl], hgnw_ref[:, hl]))
    g = proj(wg_ref[...])
    out_c = _bdot(jnp.concatenate(normed, axis=-1) * (g * _sigmoid(g)), whg_ref[...])
    gate = lambda k: _sigmoid(proj(wgt_ref[:, k * D_MODEL:(k + 1) * D_MODEL]))
    mixed = gate(0) * out_a + gate(1) * out_b + gate(2) * out_c
    x1_ref[...] = x + _bdot(mixed, wo_ref[...])


def _post_call(x2d, ya, yb, oc, lp):
    rows = x2d.shape[0]
    tm = PRE_TM
    seq_in = (x2d, ya, yb, oc)
    consts = (lp['n1'], lp['w_z'], lp['w_g'], lp['w_gt'],
              lp['w_glu'], lp['ssd_nw'], lp['w_ssd'], lp['hg_nw'], lp['w_hg'], lp['w_o'])
    row = lambda c: pl.BlockSpec((tm, c), lambda i: (i, 0))
    return pl.pallas_call(
        _post_kernel,
        grid=(rows // tm,),
        in_specs=[row(a.shape[-1]) for a in seq_in] + [_const_spec(c.shape) for c in consts],
        out_specs=row(D_MODEL),
        out_shape=jax.ShapeDtypeStruct((rows, D_MODEL), F32),
        compiler_params=_params(("parallel",)),
        name="post_merge",
    )(*seq_in, *consts)


def _ffn_kernel(final, seq_rows, tm, x_ref, *rest):
    if seq_rows:
        (sp1_ref, sp2_ref, n2_ref, wup_ref, cw_ref, cb_ref, wdn_ref, nf_ref,
         out_ref, aff_ref, hm_ref) = rest
        t = lax.broadcasted_iota(jnp.int32, (tm, 1), 0) % seq_rows
    else:
        (n2_ref, wup_ref, cw_ref, cb_ref, wdn_ref, nf_ref, out_ref, aff_ref, hm_ref, carry_ref) = rest

        @pl.when(pl.program_id(1) == 0)
        def _():
            carry_ref[...] = jnp.zeros_like(carry_ref)
    tail = FFN_CONV - 1
    x = x_ref[...]
    hb = _rms(x, n2_ref[...]).astype(BF16)
    for j in range(D_FF // FFN_BLK):
        cols = slice(j * FFN_BLK, (j + 1) * FFN_BLK)
        a = jnp.dot(hb, wup_ref[:, cols], preferred_element_type=F32)
        g = jnp.dot(hb, wup_ref[:, D_FF + j * FFN_BLK:D_FF + (j + 1) * FFN_BLK],
                    preferred_element_type=F32)
        p1 = pltpu.roll(a, 1, axis=0)
        p2 = pltpu.roll(a, 2, axis=0)
        if seq_rows:
            p1 = jnp.where(t < 1, sp1_ref[:, cols], p1)
            p2 = jnp.where(t < 2, sp2_ref[:, cols], p2)
            aff_ref[:, cols] = a
        else:
            head = jnp.concatenate([carry_ref[:, cols], a[0:SUBLANE, :]], axis=0)
            p1 = jnp.concatenate([pltpu.roll(head, 1, axis=0)[SUBLANE:], p1[SUBLANE:, :]], axis=0)
            p2 = jnp.concatenate([pltpu.roll(head, 2, axis=0)[SUBLANE:], p2[SUBLANE:, :]], axis=0)
            carry_ref[:, cols] = a[tm - SUBLANE:tm, :]
            aff_ref[0, :, cols] = a[tm - tail:tm, :]
        ac = (cw_ref[0:1, cols] * p2 + cw_ref[1:2, cols] * p1 + cw_ref[2:3, cols] * a
              + cb_ref[:, cols])
        hm_ref[:, cols] = (ac * _sigmoid(ac) * g).astype(BF16)
    y = x + jnp.dot(hm_ref[...], wdn_ref[...], preferred_element_type=F32)
    if final:
        y = _rms(y, nf_ref[...])
    out_ref[...] = y


def _ffn_prompt_call(x1, lp, nf, final, nseq, seqlen):
    tm = FFN_TM
    nchunk = seqlen // tm
    consts = (lp['n2'], lp['w_up'], lp['ffn_cw'], lp['ffn_cb'], lp['w_dn'], nf)
    row = pl.BlockSpec((tm, D_MODEL), lambda b, c: (b * nchunk + c, 0))
    return pl.pallas_call(
        functools.partial(_ffn_kernel, final, 0, tm),
        grid=(nseq, nchunk),
        in_specs=[row] + [_const_spec(c.shape) for c in consts],
        out_specs=[row, pl.BlockSpec((1, FFN_CONV - 1, D_FF), lambda b, c: (b, 0, 0))],
        out_shape=[jax.ShapeDtypeStruct(x1.shape, F32),
                   jax.ShapeDtypeStruct((nseq, FFN_CONV - 1, D_FF), F32)],
        scratch_shapes=[pltpu.VMEM((tm, D_FF), BF16), pltpu.VMEM((SUBLANE, D_FF), F32)],
        compiler_params=_params(("parallel", "arbitrary")),
        name="ffn_prompt",
    )(x1, *consts)


def _ffn_sample_call(x1, sp1, sp2, lp, nf, final):
    rows = x1.shape[0]
    tm = PRE_TM
    consts = (lp['n2'], lp['w_up'], lp['ffn_cw'], lp['ffn_cb'], lp['w_dn'], nf)
    row = lambda c: pl.BlockSpec((tm, c), lambda i: (i, 0))
    return pl.pallas_call(
        functools.partial(_ffn_kernel, final, SAMPLE_PAD, tm),
        grid=(rows // tm,),
        in_specs=[row(D_MODEL), row(D_FF), row(D_FF)] + [_const_spec(c.shape) for c in consts],
        out_specs=[row(D_MODEL), row(D_FF)],
        out_shape=[jax.ShapeDtypeStruct(x1.shape, F32), jax.ShapeDtypeStruct((rows, D_FF), F32)],
        scratch_shapes=[pltpu.VMEM((tm, D_FF), BF16)],
        compiler_params=_params(("arbitrary",)),
        name="ffn_sample",
    )(x1, sp1, sp2, *consts)


def _layer_params(l, lb, norm1_w, w_in, s5_log_dt, s5_lambda_re, s5_lambda_im, s5_b_re, s5_b_im,
                  s5_c_re, s5_c_im, s5_d, s5_w_glu, ssd_conv_w, ssd_conv_b, ssd_dt_bias, ssd_a_log,
                  ssd_d, ssd_norm_w, ssd_w_out, hg_norm_w, hg_w_out, w_o, norm2_w, ffn_w_up,
                  ffn_conv_w, ffn_conv_b, ffn_w_down):
    row = lambda v: v.astype(F32).reshape(1, -1)
    lp = {}
    lp['n1'] = row(norm1_w[l])
    widths = (S5_WIDTH, SSD_INNER, SSD_CONV_DIM, SSD_HEADS, HG_WIDTH, HG_WIDTH, HG_WIDTH, HG_WIDTH,
              3 * D_MODEL)
    names = ('w_ua', 'w_z', 'w_xbc', 'w_dt', 'w_q', 'w_f', 'w_i', 'w_g', 'w_gt')
    off = 0
    for name, w in zip(names, widths):
        lp[name] = w_in[l][:, off:off + w].astype(BF16)
        off += w
    lp['w_dt'] = jnp.pad(lp['w_dt'], ((0, 0), (0, LANE - SSD_HEADS)))
    lp['dtb'] = jnp.pad(row(ssd_dt_bias[l]), ((0, 0), (0, LANE - SSD_HEADS)))
    lp['loglb'] = jnp.log(row(lb))
    lp['log1mlb'] = jnp.log1p(-row(lb))
    lp['omlb'] = 1.0 - row(lb)

    delta = jnp.exp(s5_log_dt[l].astype(F32))[:, None]
    lr = s5_lambda_re[l].astype(F32)
    li = s5_lambda_im[l].astype(F32)
    mag = jnp.exp(lr * delta)
    ab_re = mag * jnp.cos(li * delta)
    ab_im = mag * jnp.sin(li * delta)
    den = lr * lr + li * li
    nr = ab_re - 1.0
    co_re = (nr * lr + ab_im * li) / den
    co_im = (ab_im * lr - nr * li) / den
    br = s5_b_re[l].astype(F32)
    bi = s5_b_im[l].astype(F32)
    bb_re = co_re[..., None] * br - co_im[..., None] * bi
    bb_im = co_re[..., None] * bi + co_im[..., None] * br
    eye = jnp.eye(S5_GROUPS, dtype=F32)
    bd_in = lambda m: jnp.einsum('gnj,gh->gjhn', m, eye).reshape(S5_WIDTH, S5_LANES)
    bd_out = lambda m: jnp.einsum('gjn,gh->gnhj', m, eye).reshape(S5_LANES, S5_WIDTH)
    lp['s5_bb'] = jnp.concatenate([bd_in(bb_re), bd_in(bb_im)], axis=1).astype(BF16)
    lp['s5_cc'] = jnp.concatenate([bd_out(s5_c_re[l].astype(F32)),
                                   -bd_out(s5_c_im[l].astype(F32))], axis=0).astype(BF16)
    pw = jnp.asarray(list(range(1, S5_R + 1)) + [2 * S5_R, 4 * S5_R], F32)[:, None, None]
    pm = jnp.exp(lr * delta * pw)
    ph = li * delta * pw
    pad = ((0, 40 - (S5_R + 2)), (0, 0))
    lp['s5_apr'] = jnp.pad((pm * jnp.cos(ph)).reshape(-1, S5_LANES), pad)
    lp['s5_api'] = jnp.pad((pm * jnp.sin(ph)).reshape(-1, S5_LANES), pad)
    lp['s5_apr8'] = jnp.repeat(lp['s5_apr'][:S5_R + 2], S5_CH, axis=0)
    lp['s5_api8'] = jnp.repeat(lp['s5_api'][:S5_R + 2], S5_CH, axis=0)
    npair = S5_LANES // LANE
    pair_cols = lambda m_re, m_im: jnp.stack(
        [m_re.reshape(-1, npair, LANE), m_im.reshape(-1, npair, LANE)], axis=2).reshape(-1, 2 * S5_LANES)
    bb_int = pair_cols(bd_in(bb_re), bd_in(bb_im)).astype(BF16)
    cc_int = pair_cols(bd_out(s5_c_re[l].astype(F32)).T, -bd_out(s5_c_im[l].astype(F32)).T).T.astype(BF16)
    kw = 2 * LANE
    nlc = 2 * S5_LANES // S5_LC
    ktile = lambda lc: lc * S5_LC // (2 * S5_LANES // (S5_WIDTH // kw))
    lp['s5_bbc'] = jnp.stack([bb_int[ktile(lc) * kw:(ktile(lc) + 1) * kw, lc * S5_LC:(lc + 1) * S5_LC]
                              for lc in range(nlc)])
    lp['s5_ccc'] = jnp.stack([cc_int[lc * S5_LC:(lc + 1) * S5_LC, ktile(lc) * kw:(ktile(lc) + 1) * kw]
                              for lc in range(nlc)])
    lp['s5_d'] = row(s5_d[l])
    lp['w_glu'] = s5_w_glu[l].astype(BF16)

    lp['conv_w'] = ssd_conv_w[l].astype(F32)
    lp['conv_b'] = row(ssd_conv_b[l])
    lp['ssd_arow'] = jnp.pad(-jnp.exp(row(ssd_a_log[l])), ((0, 0), (0, LANE - SSD_HEADS)))
    lp['ssd_dexp'] = jnp.repeat(row(ssd_d[l]), SSD_HEAD_DIM, axis=1)
    lp['ssd_nw'] = row(ssd_norm_w[l])
    lp['w_ssd'] = ssd_w_out[l].astype(BF16)
    lp['hg_nw'] = row(hg_norm_w[l])
    lp['w_hg'] = hg_w_out[l].astype(BF16)
    lp['w_o'] = w_o[l].astype(BF16)
    lp['n2'] = row(norm2_w[l])
    lp['w_up'] = ffn_w_up[l].astype(BF16)
    lp['ffn_cw'] = ffn_conv_w[l].astype(F32)
    lp['ffn_cb'] = row(ffn_conv_b[l])
    lp['w_dn'] = ffn_w_down[l].astype(BF16)
    return lp


_PRE_KEYS = ('ua', 'xbc', 'dt', 'q', 'lf', 'kk', 'v')


def kernel(x_prompt, x_sample, state_s5_re, state_s5_im, state_ssd, state_ssd_conv, state_hgrn, state_ffn_conv, norm1_w, w_in, s5_log_dt, s5_lambda_re, s5_lambda_im, s5_b_re, s5_b_im, s5_c_re, s5_c_im, s5_d, s5_w_glu, ssd_conv_w, ssd_conv_b, ssd_dt_bias, ssd_a_log, ssd_d, ssd_norm_w, ssd_w_out, hg_lb_logits, hg_norm_w, hg_w_out, w_o, norm2_w, ffn_w_up, ffn_conv_w, ffn_conv_b, ffn_w_down, norm_f_w):
    nb_p, len_p, _ = x_prompt.shape
    nb_s, len_s, _ = x_sample.shape
    assert len_p % S5_T == 0 and len_p % MIX_T == 0 and len_p % PRE_TM == 0 and len_p % FFN_TM == 0
    assert len_s <= SAMPLE_PAD and len_s >= SSD_CONV - 1
    assert (nb_s * SAMPLE_PAD) % PRE_TM == 0 and (nb_s * SAMPLE_PAD) % FFN_TM == 0
    lb_cum = jnp.cumsum(jax.nn.softmax(hg_lb_logits.astype(F32), axis=0), axis=0)
    lb_all = lb_cum - lb_cum[0:1]
    nf = norm_f_w.astype(F32).reshape(1, -1)

    xp = x_prompt.astype(F32).reshape(nb_p * len_p, D_MODEL)
    xs = jnp.pad(x_sample.astype(F32), ((0, 0), (0, SAMPLE_PAD - len_s), (0, 0)))
    xs = xs.reshape(nb_s * SAMPLE_PAD, D_MODEL)
    zeros_p = (jnp.zeros((1, nb_p, SSD_INNER, SSD_STATE), F32),
               jnp.zeros((1, nb_p, SSD_CONV - 1, SSD_CONV_DIM), F32),
               jnp.zeros((1, nb_p, HG_WIDTH, HG_VDIM), F32))
    states_s = (state_ssd.astype(F32).reshape(DEPTH, nb_s, SSD_INNER, SSD_STATE),
                state_ssd_conv.astype(F32),
                state_hgrn.astype(F32).reshape(DEPTH, nb_s, HG_WIDTH, HG_VDIM))
    mix_p = None
    mix_s = None
    new_p = []
    new_s = []
    for l in range(DEPTH):
        lp = _layer_params(l, lb_all[l], norm1_w, w_in, s5_log_dt, s5_lambda_re, s5_lambda_im,
                           s5_b_re, s5_b_im, s5_c_re, s5_c_im, s5_d, s5_w_glu, ssd_conv_w,
                           ssd_conv_b, ssd_dt_bias, ssd_a_log, ssd_d, ssd_norm_w, ssd_w_out,
                           hg_norm_w, hg_w_out, w_o, norm2_w, ffn_w_up, ffn_conv_w, ffn_conv_b,
                           ffn_w_down)
        final = l == DEPTH - 1

        pre = dict(zip(_PRE_KEYS, _pre_call(xp, lp)))
        ya, p_re, p_im = _s5_prompt_call(pre['ua'], lp, nb_p, len_p)
        yb, oc, *mix_p = _mix_call(pre, zeros_p, 0, l, mix_p, lp, nb_p, len_p, MIX_T, MIX_T,
                                   MIX_NB_PROMPT, SSD_Q, HG_Q)
        x1 = _post_call(xp, ya, yb.reshape(-1, SSD_INNER), oc.reshape(-1, HG_WIDTH), lp)
        xp, p_ffn = _ffn_prompt_call(x1, lp, nf, final, nb_p, len_p)
        new_p.append((p_re.reshape(nb_p, S5_GROUPS, S5_STATE), p_im.reshape(nb_p, S5_GROUPS, S5_STATE),
                      p_ffn))

        pre = dict(zip(_PRE_KEYS, _pre_call(xs, lp)))
        ua_tm = pre['ua'].reshape(nb_s, SAMPLE_PAD, S5_WIDTH)[:, :len_s].transpose(1, 0, 2)
        ya_tm, s_re, s_im = _s5_sample_call(
            ua_tm.reshape(len_s * nb_s, S5_WIDTH),
            state_s5_re[l].astype(F32).reshape(nb_s, S5_LANES),
            state_s5_im[l].astype(F32).reshape(nb_s, S5_LANES), lp, len_s, nb_s)
        ya = jnp.pad(ya_tm.reshape(len_s, nb_s, S5_WIDTH).transpose(1, 0, 2),
                     ((0, 0), (0, SAMPLE_PAD - len_s), (0, 0))).reshape(nb_s * SAMPLE_PAD, S5_WIDTH)
        yb, oc, *mix_s = _mix_call(pre, states_s, l, l, mix_s, lp, nb_s, SAMPLE_PAD, SAMPLE_PAD, len_s,
                                   MIX_NB_SAMPLE, SAMPLE_PAD, SAMPLE_PAD)
        x1 = _post_call(xs, ya, yb.reshape(-1, SSD_INNER), oc.reshape(-1, HG_WIDTH), lp)
        st = state_ffn_conv[l].astype(F32)
        zrow = jnp.zeros((nb_s, SAMPLE_PAD - 2, D_FF), F32)
        sp1 = jnp.concatenate([st[:, 1:2], st[:, 1:2], zrow], axis=1).reshape(-1, D_FF)
        sp2 = jnp.concatenate([st[:, 0:1], st[:, 1:2], zrow], axis=1).reshape(-1, D_FF)
        xs, aff = _ffn_sample_call(x1, sp1, sp2, lp, nf, final)
        s_ffn = aff.reshape(nb_s, SAMPLE_PAD, D_FF)[:, len_s - (FFN_CONV - 1):len_s]
        new_s.append((s_re.reshape(nb_s, S5_GROUPS, S5_STATE), s_im.reshape(nb_s, S5_GROUPS, S5_STATE),
                      s_ffn))

    stk = lambda lst, i: jnp.stack([s[i] for s in lst])
    y_prompt = xp.reshape(nb_p, len_p, D_MODEL)
    y_sample = xs.reshape(nb_s, SAMPLE_PAD, D_MODEL)[:, :len_s]

    def mix_states(m, n):
        ssd, conv, hg = m
        return (ssd.reshape(DEPTH, n, SSD_HEADS, SSD_HEAD_DIM, SSD_STATE), conv,
                hg.reshape(DEPTH, n, HG_HEADS, HG_KDIM, HG_VDIM))

    p_ssd, p_conv, p_hg = mix_states(mix_p, nb_p)
    s_ssd, s_conv, s_hg = mix_states(mix_s, nb_s)
    return (y_prompt, y_sample,
            stk(new_p, 0), stk(new_p, 1), p_ssd, p_conv, p_hg, stk(new_p, 2),
            stk(new_s, 0), stk(new_s, 1), s_ssd, s_conv, s_hg, stk(new_s, 2))
```

```python
import functools
import math

import jax
import jax.numpy as jnp
import numpy as np
from jax import lax
from jax.experimental import pallas as pl
from jax.experimental.pallas import tpu as pltpu

F32 = jnp.float32
BF16 = jnp.bfloat16

D_MODEL = 1024
DEPTH = 2
S5_WIDTH = 512
S5_GROUP = 16
S5_GROUPS = 32
S5_STATE = 64
S5_LANES = S5_GROUPS * S5_STATE
SSD_INNER = 1024
SSD_HEAD_DIM = 64
SSD_HEADS = 16
SSD_GROUPS = 4
SSD_STATE = 128
SSD_CONV = 4
SSD_CONV_DIM = 2048
HG_WIDTH = 512
HG_HEADS = 4
HG_KDIM = 128
HG_VDIM = 128
D_FF = 2816
FFN_CONV = 3
EPS = 1e-6

LANE = 128
SUBLANE = 8
SAMPLE_PAD = 8
HG_SAFE_LOG = -80.0

BIG_TM = 512
SMALL_TM = 256
PRE_BLK = 256
POST_BLK = 256
FFN_BLK = 256
S5_T = 256
S5_CH = 8
S5_R = S5_T // S5_CH
S5_LC = 512
MIX_T = 128
SSD_Q = 128
HG_Q = 64
MIX_NB_PROMPT = 2
MIX_NB_SAMPLE = 8
VMEM_LIMIT = 56 * 1024 * 1024


def _const_spec(shape):
    nd = len(shape)
    return pl.BlockSpec(shape, lambda *_: (0,) * nd, pipeline_mode=pl.Buffered(1))


def _params(sem):
    return pltpu.CompilerParams(dimension_semantics=sem, vmem_limit_bytes=VMEM_LIMIT)


def _bdot(a, b):
    return jnp.dot(a.astype(BF16), b.astype(BF16), preferred_element_type=F32)


def _bdot_nt(a, b):
    return lax.dot_general(a.astype(BF16), b.astype(BF16), (((1,), (1,)), ((), ())),
                           preferred_element_type=F32)


def _bdot_tn(a, b):
    return lax.dot_general(a.astype(BF16), b.astype(BF16), (((0,), (0,)), ((), ())),
                           preferred_element_type=F32)


def _split3(x):
    h = x.astype(BF16)
    r = x - h.astype(F32)
    m = r.astype(BF16)
    l = (r - m.astype(F32)).astype(BF16)
    return h, m, l


def _sel_left(m01, x):
    h, m, l = _split3(x)
    d = lambda p: jnp.dot(m01, p, preferred_element_type=F32)
    return (d(h) + d(m)) + d(l)


def _sel_right(x, m01):
    h, m, l = _split3(x)
    d = lambda p: jnp.dot(p, m01, preferred_element_type=F32)
    return (d(h) + d(m)) + d(l)


def _sigmoid(x):
    return 1.0 / (1.0 + jnp.exp(-x))


def _softplus(x):
    return jnp.maximum(x, 0.0) + jnp.log1p(jnp.exp(-jnp.abs(x)))


def _rms(x, w):
    return x * lax.rsqrt(jnp.mean(x * x, axis=-1, keepdims=True) + EPS) * w


def _pre_kernel(seq_rows, tm, x_ref, *rest):
    if seq_rows:
        (s1_ref, s2_ref, s3_ref, n1_ref, w_ua, w_xbc, w_dt, w_q, w_f, w_i, cw_ref, cb_ref,
         dtb_ref, loglb_ref, log1mlb_ref, omlb_ref,
         ua_o, xc_o, dt_o, q_o, lf_o, kk_o, v_o, raw_o) = rest
        t = lax.broadcasted_iota(jnp.int32, (tm, 1), 0) % seq_rows
    else:
        (n1_ref, w_ua, w_xbc, w_dt, w_q, w_f, w_i, cw_ref, cb_ref,
         dtb_ref, loglb_ref, log1mlb_ref, omlb_ref,
         ua_o, xc_o, dt_o, q_o, lf_o, kk_o, v_o, cst_o, carry_ref) = rest

        @pl.when(pl.program_id(1) == 0)
        def _():
            carry_ref[...] = jnp.zeros_like(carry_ref)
    tail = SSD_CONV - 1
    hb = _rms(x_ref[...], n1_ref[...]).astype(BF16)
    dot = lambda w: jnp.dot(hb, w, preferred_element_type=F32)
    ua_o[...] = dot(w_ua[...])
    for j in range(SSD_CONV_DIM // PRE_BLK):
        cols = slice(j * PRE_BLK, (j + 1) * PRE_BLK)
        raw = dot(w_xbc[:, cols])
        if seq_rows:
            acc = cw_ref[tail:tail + 1, cols] * raw + cb_ref[:, cols]
            for k, s_ref in ((1, s1_ref), (2, s2_ref), (3, s3_ref)):
                prev = jnp.where(t < k, s_ref[:, cols], pltpu.roll(raw, k, axis=0))
                acc = acc + cw_ref[tail - k:tail - k + 1, cols] * prev
            raw_o[:, cols] = raw
        else:
            xx = jnp.concatenate([carry_ref[:, cols], raw], axis=0)
            acc = cw_ref[0:1, cols] * xx
            for k in range(1, SSD_CONV):
                acc = pltpu.roll(acc, 1, axis=0) + cw_ref[k:k + 1, cols] * xx
            acc = acc[SUBLANE:, :] + cb_ref[:, cols]
            carry_ref[:, cols] = raw[tm - SUBLANE:tm, :]
            cst_o[0, :, cols] = raw[tm - tail:tm, :]
        xc_o[:, cols] = acc * _sigmoid(acc)
    dt_o[...] = _softplus(dot(w_dt[...]) + dtb_ref[...])
    q_o[...] = dot(w_q[...])
    zf = dot(w_f[...])
    log_sig = jnp.minimum(zf, 0.0) - jnp.log1p(jnp.exp(-jnp.abs(zf)))
    a = loglb_ref[...]
    b = log1mlb_ref[...] + log_sig
    lf_o[...] = jnp.maximum(a, b) + jnp.log1p(jnp.exp(-jnp.abs(a - b)))
    kk_o[...] = omlb_ref[...] * _sigmoid(-zf)
    v_o[...] = dot(w_i[...])


_PRE_WIDTHS = (S5_WIDTH, SSD_CONV_DIM, LANE, HG_WIDTH, HG_WIDTH, HG_WIDTH, HG_WIDTH)


def _pre_consts(lp):
    return (lp['n1'], lp['w_ua'], lp['w_xbc'], lp['w_dt'], lp['w_q'], lp['w_f'], lp['w_i'],
            lp['conv_w'], lp['conv_b'], lp['dtb'], lp['loglb'], lp['log1mlb'], lp['omlb'])


def _pre_prompt_call(x2d, lp, nseq, seqlen):
    tm = BIG_TM
    nchunk = seqlen // tm
    consts = _pre_consts(lp)
    row = lambda c: pl.BlockSpec((tm, c), lambda b, k: (b * nchunk + k, 0))
    tail = SSD_CONV - 1
    return pl.pallas_call(
        functools.partial(_pre_kernel, 0, tm),
        grid=(nseq, nchunk),
        in_specs=[row(D_MODEL)] + [_const_spec(c.shape) for c in consts],
        out_specs=[row(c) for c in _PRE_WIDTHS]
        + [pl.BlockSpec((1, tail, SSD_CONV_DIM), lambda b, k: (b, 0, 0))],
        out_shape=[jax.ShapeDtypeStruct((x2d.shape[0], c), F32) for c in _PRE_WIDTHS]
        + [jax.ShapeDtypeStruct((nseq, tail, SSD_CONV_DIM), F32)],
        scratch_shapes=[pltpu.VMEM((SUBLANE, SSD_CONV_DIM), F32)],
        compiler_params=_params(("parallel", "arbitrary")),
        name="pre_proj_prompt",
    )(x2d, *consts)


def _pre_sample_call(x2d, s1, s2, s3, lp):
    rows = x2d.shape[0]
    tm = SMALL_TM
    consts = _pre_consts(lp)
    row = lambda c: pl.BlockSpec((tm, c), lambda i: (i, 0))
    return pl.pallas_call(
        functools.partial(_pre_kernel, SAMPLE_PAD, tm),
        grid=(rows // tm,),
        in_specs=[row(D_MODEL)] + [row(SSD_CONV_DIM)] * 3 + [_const_spec(c.shape) for c in consts],
        out_specs=[row(c) for c in _PRE_WIDTHS] + [row(SSD_CONV_DIM)],
        out_shape=[jax.ShapeDtypeStruct((rows, c), F32) for c in _PRE_WIDTHS]
        + [jax.ShapeDtypeStruct((rows, SSD_CONV_DIM), F32)],
        compiler_params=_params(("parallel",)),
        name="pre_proj_sample",
    )(x2d, s1, s2, s3, *consts)


def _cmul_add(ar, ai, xr, xi, br, bi):
    return ar * xr - ai * xi + br, ar * xi + ai * xr + bi


def _s5_prompt_kernel(ua_ref, perm_ref, permt_ref, bbc_ref, ccc_ref, apr_ref, api_ref, d_ref,
                      ya_ref, st_ref, bu_ref):
    @pl.when(pl.program_id(1) == 0)
    def _():
        st_ref[...] = jnp.zeros_like(st_ref)

    ua = ua_ref[...]
    up = jnp.dot(perm_ref[...], ua.astype(BF16), preferred_element_type=F32).astype(BF16)
    rowid = lax.broadcasted_iota(jnp.int32, (S5_CH, LANE), 0)
    tab = lambda ref, k, gp: ref[k * S5_CH:(k + 1) * S5_CH, gp * LANE:(gp + 1) * LANE]
    kw = 2 * LANE
    y_acc = [None] * (S5_WIDTH // kw)

    for lc in range(2 * S5_LANES // S5_LC):
        kt = lc * S5_LC // (2 * S5_LANES // (S5_WIDTH // kw))
        cols = slice(lc * S5_LC, (lc + 1) * S5_LC)
        bu_ref[:, cols] = jnp.dot(up[:, kt * kw:(kt + 1) * kw], bbc_ref[lc], preferred_element_type=F32)
        for gp in range(lc * S5_LC // kw, (lc + 1) * S5_LC // kw):
            re = slice(gp * kw, gp * kw + LANE)
            im = slice(gp * kw + LANE, (gp + 1) * kw)
            a_r, a_i = tab(apr_ref, 0, gp), tab(api_ref, 0, gp)
            sr = jnp.zeros((S5_CH, LANE), F32)
            si = jnp.zeros((S5_CH, LANE), F32)
            for r in range(S5_R):
                rows = slice(r * S5_CH, (r + 1) * S5_CH)
                sr, si = _cmul_add(a_r, a_i, sr, si, bu_ref[rows, re], bu_ref[rows, im])
                bu_ref[rows, re] = sr
                bu_ref[rows, im] = si
            pr = jnp.broadcast_to(st_ref[0, :, re], (S5_CH, LANE))
            pi = jnp.broadcast_to(st_ref[0, :, im], (S5_CH, LANE))
            vr = jnp.where(rowid == 0, pr, pltpu.roll(sr, 1, axis=0))
            vi = jnp.where(rowid == 0, pi, pltpu.roll(si, 1, axis=0))
            for d, k in ((1, S5_R - 1), (2, S5_R), (4, S5_R + 1)):
                keep = rowid >= d
                tr = jnp.where(keep, pltpu.roll(vr, d, axis=0), 0.0)
                ti = jnp.where(keep, pltpu.roll(vi, d, axis=0), 0.0)
                vr, vi = _cmul_add(tab(apr_ref, k, gp), tab(api_ref, k, gp), tr, ti, vr, vi)
            for r in range(S5_R):
                rows = slice(r * S5_CH, (r + 1) * S5_CH)
                xr, xi = _cmul_add(tab(apr_ref, r, gp), tab(api_ref, r, gp), vr, vi,
                                   bu_ref[rows, re], bu_ref[rows, im])
                bu_ref[rows, re] = xr
                bu_ref[rows, im] = xi
            st_ref[0, :, re] = xr[S5_CH - 1:S5_CH, :]
            st_ref[0, :, im] = xi[S5_CH - 1:S5_CH, :]
        part = _bdot(bu_ref[:, cols], ccc_ref[lc])
        y_acc[kt] = part if y_acc[kt] is None else y_acc[kt] + part

    yp = jnp.concatenate(y_acc, axis=1)
    ya_ref[...] = _sel_left(permt_ref[...], yp) + d_ref[...] * ua


def _s5_perm():
    p = np.zeros((S5_T, S5_T), np.float32)
    for i in range(S5_T):
        p[i, (i % S5_CH) * S5_R + i // S5_CH] = 1.0
    return jnp.asarray(p, BF16), jnp.asarray(p.T, BF16)


def _s5_prompt_call(ua, lp, nseq, seqlen):
    nchunk = seqlen // S5_T
    perm, permt = _s5_perm()
    consts = (perm, permt, lp['s5_bbc'], lp['s5_ccc'], lp['s5_apr8'], lp['s5_api8'], lp['s5_d'])
    st_spec = pl.BlockSpec((1, 1, 2 * S5_LANES), lambda b, c: (b, 0, 0))
    st_shape = jax.ShapeDtypeStruct((nseq, 1, 2 * S5_LANES), F32)
    ya, st = pl.pallas_call(
        _s5_prompt_kernel,
        grid=(nseq, nchunk),
        in_specs=[pl.BlockSpec((S5_T, S5_WIDTH), lambda b, c: (b * nchunk + c, 0))]
        + [_const_spec(c.shape) for c in consts],
        out_specs=[pl.BlockSpec((S5_T, S5_WIDTH), lambda b, c: (b * nchunk + c, 0)), st_spec],
        out_shape=[jax.ShapeDtypeStruct(ua.shape, F32), st_shape],
        scratch_shapes=[pltpu.VMEM((S5_T, 2 * S5_LANES), F32)],
        compiler_params=_params(("parallel", "arbitrary")),
        name="s5_prompt",
    )(ua, *consts)
    st = st.reshape(nseq, S5_LANES // LANE, 2, LANE)
    return ya, st[:, :, 0].reshape(nseq, S5_LANES), st[:, :, 1].reshape(nseq, S5_LANES)


def _s5_sample_kernel(nstep, nseq, ua_ref, sre0_ref, sim0_ref, bb_ref, cc_ref, apr_ref, api_ref,
                      d_ref, ya_ref, sre_ref, sim_ref, bu_ref, xb_ref):
    ua = ua_ref[...]
    bu_ref[...] = jnp.dot(ua.astype(BF16), bb_ref[...], preferred_element_type=F32)
    for lc in range(S5_LANES // S5_LC):
        re = slice(lc * S5_LC, (lc + 1) * S5_LC)
        im = slice(S5_LANES + lc * S5_LC, S5_LANES + (lc + 1) * S5_LC)
        a_r = apr_ref[0:1, re]
        a_i = api_ref[0:1, re]
        sr = sre0_ref[:, re]
        si = sim0_ref[:, re]
        for t in range(nstep):
            rows = slice(t * nseq, (t + 1) * nseq)
            sr, si = _cmul_add(a_r, a_i, sr, si, bu_ref[rows, re], bu_ref[rows, im])
            xb_ref[rows, re] = sr.astype(BF16)
            xb_ref[rows, im] = si.astype(BF16)
        sre_ref[:, re] = sr
        sim_ref[:, re] = si
    ya_ref[...] = jnp.dot(xb_ref[...], cc_ref[...], preferred_element_type=F32) + d_ref[...] * ua


def _s5_sample_call(ua_tm, sre0, sim0, lp, nstep, nseq):
    rows = nstep * nseq
    args = (ua_tm, sre0, sim0, lp['s5_bb'], lp['s5_cc'], lp['s5_apr'], lp['s5_api'], lp['s5_d'])
    st_shape = jax.ShapeDtypeStruct((nseq, S5_LANES), F32)
    full = lambda s: pl.BlockSpec(s, lambda i: (0,) * len(s))
    return pl.pallas_call(
        functools.partial(_s5_sample_kernel, nstep, nseq),
        grid=(1,),
        in_specs=[full(a.shape) for a in args],
        out_specs=[full((rows, S5_WIDTH)), full((nseq, S5_LANES)), full((nseq, S5_LANES))],
        out_shape=[jax.ShapeDtypeStruct((rows, S5_WIDTH), F32), st_shape, st_shape],
        scratch_shapes=[pltpu.VMEM((rows, 2 * S5_LANES), F32), pltpu.VMEM((rows, 2 * S5_LANES), BF16)],
        compiler_params=_params(("arbitrary",)),
        name="s5_sample",
    )(*args)


def _mix_kernel(t_rows, n_valid, nb, ssd_q, hg_q, n_alias,
                xc_ref, dt_ref, q_ref, lf_ref, kk_ref, v_ref, ssd0_ref, hg0_ref,
                arow_ref, dexp_ref, ltri_s_ref, ltri_h_ref, *rest):
    (yb_ref, oc_ref, ssd_ref, hg_ref, hb_ref, hk_ref, ho_ref, hq_ref, hv_ref) = rest[n_alias:]

    @pl.when(pl.program_id(1) == 0)
    def _():
        ssd_ref[...] = ssd0_ref[...]
        hg_ref[...] = hg0_ref[...]

    n_rows = nb * t_rows
    if n_valid < t_rows:
        valid_all = (lax.broadcasted_iota(jnp.int32, (n_rows, 1), 0) % t_rows < n_valid).astype(F32)
        valid = valid_all[0:t_rows, :]
    else:
        valid_all = valid = None
    sq = lambda m: m if m.shape[0] == LANE else jnp.concatenate(
        [m, jnp.zeros((LANE - m.shape[0], LANE), F32)], axis=0)

    def ssd_chunk(j, r0):
        q_ = ssd_q
        rows = slice(r0, r0 + q_)
        dt = dt_ref[j, rows, :]
        if valid is not None:
            dt = dt * valid[rows, :]
        acum = _sel_left(ltri_s_ref[...], dt * arow_ref[...])
        acum_t = sq(acum).T
        dt_t = sq(dt).T
        alast = acum[q_ - 1:q_, :]
        eacm = jnp.exp(acum)
        wcol = dt * jnp.exp(alast - acum)
        ela_t = jnp.exp(jnp.broadcast_to(acum_t[:, q_ - 1:q_], (LANE, LANE)))
        tri = lax.broadcasted_iota(jnp.int32, (q_, q_), 0) >= lax.broadcasted_iota(jnp.int32, (q_, q_), 1)
        first_head = lax.broadcasted_iota(jnp.int32, (q_, 2 * SSD_HEAD_DIM), 1) < SSD_HEAD_DIM
        hpg = SSD_HEADS // SSD_GROUPS
        gw = hpg * SSD_HEAD_DIM
        b0 = SSD_INNER
        c0 = SSD_INNER + SSD_GROUPS * SSD_STATE
        for g in range(SSD_GROUPS):
            cg = xc_ref[j, rows, c0 + g * SSD_STATE:c0 + (g + 1) * SSD_STATE]
            bgb = xc_ref[j, rows, b0 + g * SSD_STATE:b0 + (g + 1) * SSD_STATE].astype(BF16)
            cb = _bdot_nt(cg, bgb)
            sg = ssd_ref[j, g * gw:(g + 1) * gw, :]
            sgb = sg.astype(BF16)
            scale = []
            xw = []
            for r in range(hpg):
                h = g * hpg + r
                hl = slice(h * SSD_HEAD_DIM, (h + 1) * SSD_HEAD_DIM)
                xs_h = xc_ref[j, rows, hl]
                seg = acum[:, h:h + 1] - acum_t[h:h + 1, :q_]
                dec = jnp.exp(jnp.where(tri, seg, -jnp.inf))
                gmat = cb * dec * dt_t[h:h + 1, :q_]
                ce = cg * eacm[:, h:h + 1]
                yb_ref[j, rows, hl] = (_bdot(gmat, xs_h)
                                       + _bdot_nt(ce, sgb[r * SSD_HEAD_DIM:(r + 1) * SSD_HEAD_DIM, :])
                                       + dexp_ref[:, hl] * xs_h)
                scale.append(jnp.broadcast_to(ela_t[h:h + 1, :], (SSD_HEAD_DIM, SSD_STATE)))
                if r % 2 == 0:
                    pl_ = slice(h * SSD_HEAD_DIM, (h + 2) * SSD_HEAD_DIM)
                    xw.append(xc_ref[j, rows, pl_] * jnp.where(first_head, wcol[:, h:h + 1], wcol[:, h + 1:h + 2]))
            upd = _bdot_tn(jnp.concatenate(xw, axis=1), bgb)
            ssd_ref[j, g * gw:(g + 1) * gw, :] = sg * jnp.concatenate(scale, axis=0) + upd
            yield

    def hg_prepare():
        flat = lambda ref: ref[...].reshape(n_rows, HG_WIDTH)
        lf = flat(lf_ref)
        kk = flat(kk_ref)
        if valid_all is not None:
            lf = lf * valid_all
            kk = kk * valid_all
        bc = _sel_left(ltri_h_ref[...], lf)
        hb_ref[...] = bc
        hk_ref[...] = kk
        safe = jnp.min(bc) >= HG_SAFE_LOG

        @pl.when(safe)
        def _():
            ho_ref[...] = jnp.zeros((n_rows, HG_WIDTH), F32)

        @pl.when(jnp.logical_not(safe))
        def _():
            ho_ref[...] = jnp.zeros((n_rows, HG_WIDTH), F32)
            hq_ref[...] = flat(q_ref)
            hv_ref[...] = flat(v_ref)
            rid = lax.broadcasted_iota(jnp.int32, (n_rows, 1), 0)

            def body(s, c):
                brow = hb_ref[pl.ds(s, 1), :]
                krow = hk_ref[pl.ds(s, 1), :]
                vrow = hv_ref[pl.ds(s, 1), :]
                w = hq_ref[...] * krow * jnp.exp(jnp.minimum(hb_ref[...] - brow, 0.0))
                same_chunk_end = (s // hg_q + 1) * hg_q
                w = jnp.where(jnp.logical_and(rid >= s, rid < same_chunk_end), w, 0.0)
                for h in range(HG_HEADS):
                    hl = slice(h * HG_KDIM, (h + 1) * HG_KDIM)
                    a = jnp.sum(w[:, hl], axis=-1, keepdims=True)
                    ho_ref[:, hl] = ho_ref[:, hl] + a * vrow[:, hl]
                return c

            lax.fori_loop(0, n_rows, body, 0)

        return safe

    def hg_chunk(j, r0, safe):
        q_ = hg_q
        rows = slice(r0, r0 + q_)
        brows = slice(j * t_rows + r0, j * t_rows + r0 + q_)
        qq = q_ref[j, rows, :]
        vv = v_ref[j, rows, :]
        bc = hb_ref[brows, :]
        kk = hk_ref[brows, :]
        blast = bc[q_ - 1:q_, :]
        qe = qq * jnp.exp(bc)
        kd = kk * jnp.exp(blast - bc)
        ke = kk * jnp.exp(-bc)
        tri = lax.broadcasted_iota(jnp.int32, (q_, q_), 0) >= lax.broadcasted_iota(jnp.int32, (q_, q_), 1)
        eb = jnp.exp(blast)
        for h in range(HG_HEADS):
            hl = slice(h * HG_KDIM, (h + 1) * HG_KDIM)
            att = jnp.where(tri, _bdot_nt(qe[:, hl], ke[:, hl]), 0.0)
            intra = jnp.where(safe, _bdot(att, vv[:, hl]), ho_ref[brows, hl])
            sh = hg_ref[j, hl, :]
            oc_ref[j, rows, hl] = intra + _bdot(qe[:, hl], sh)
            eb_t = jnp.broadcast_to(eb[:, hl], (HG_KDIM, HG_KDIM)).T
            hg_ref[j, hl, :] = sh * eb_t + _bdot_tn(kd[:, hl], vv[:, hl])
            yield

    def ssd_seq(j):
        for c in range(t_rows // ssd_q):
            yield from ssd_chunk(j, c * ssd_q)

    def hg_seq(j, safe):
        for c in range(t_rows // hg_q):
            yield from hg_chunk(j, c * hg_q, safe)

    safe = hg_prepare()
    tasks = [t for j in range(nb) for t in (ssd_seq(j), hg_seq(j, safe))]
    while tasks:
        for t in list(tasks):
            try:
                next(t)
            except StopIteration:
                tasks.remove(t)


def _ltri(n, q):
    return jnp.asarray(np.kron(np.eye(n // q, dtype=np.float32), np.tril(np.ones((q, q), np.float32))), BF16)


def _mix_call(pre, states, l_in, l_out, prev, lp, nseq, seqlen, t_rows, n_valid, nb, ssd_q, hg_q):
    nchunk = seqlen // t_rows
    view = lambda a: a.reshape(nseq, seqlen, a.shape[-1])
    seq_in = [view(pre[k]) for k in ('xc', 'dt', 'q', 'lf', 'kk', 'v')]
    consts = (lp['ssd_arow'], lp['ssd_dexp'], _ltri(ssd_q, ssd_q), _ltri(nb * t_rows, hg_q))
    tile = lambda c: pl.BlockSpec((nb, t_rows, c), lambda i, k: (i, k, 0))
    st = lambda a, l: pl.BlockSpec((None, nb) + a.shape[2:], lambda i, k: (l, i, 0, 0))
    prev = () if prev is None else tuple(prev)
    n_in = len(seq_in) + len(states) + len(consts)
    kern = functools.partial(_mix_kernel, t_rows, n_valid, nb, ssd_q, hg_q, len(prev))
    return pl.pallas_call(
        kern,
        grid=(nseq // nb, nchunk),
        in_specs=[tile(a.shape[-1]) for a in seq_in] + [st(a, l_in) for a in states]
        + [_const_spec(c.shape) for c in consts]
        + [pl.BlockSpec(memory_space=pl.ANY) for _ in prev],
        out_specs=[tile(SSD_INNER), tile(HG_WIDTH)] + [st(a, l_out) for a in states],
        out_shape=[jax.ShapeDtypeStruct((nseq, seqlen, SSD_INNER), F32),
                   jax.ShapeDtypeStruct((nseq, seqlen, HG_WIDTH), F32)]
        + [jax.ShapeDtypeStruct((DEPTH,) + a.shape[1:], F32) for a in states],
        input_output_aliases={n_in + k: 2 + k for k in range(len(prev))},
        scratch_shapes=[pltpu.VMEM((nb * t_rows, HG_WIDTH), F32) for _ in range(5)],
        compiler_params=_params(("parallel", "arbitrary")),
        name="ssd_hgrn_mix",
    )(*seq_in, *states, *consts, *prev)


def _post_kernel(x_ref, ya_ref, yb_ref, oc_ref, n1_ref, wz_ref, wg_ref, wgt_ref,
                 wglu_ref, ssdnw_ref, wssd_ref, hgnw_ref, whg_ref, wo_ref, x1_ref, mix_ref):
    x = x_ref[...]
    hb = _rms(x, n1_ref[...]).astype(BF16)
    proj = lambda w: jnp.dot(hb, w, preferred_element_type=F32)
    ya = ya_ref[...]
    cdf = 0.5 * (1.0 + jnp.tanh(math.sqrt(2.0 / math.pi) * (ya + 0.044715 * (ya * ya * ya))))
    ga = (ya * cdf).astype(BF16)
    z = proj(wz_ref[...])
    nb_ = _rms(yb_ref[...] * (z * _sigmoid(z)), ssdnw_ref[...]).astype(BF16)
    oc = oc_ref[...]
    normed = []
    for h in range(HG_HEADS):
        hl = slice(h * HG_VDIM, (h + 1) * HG_VDIM)
        normed.append(_rms(oc[:, hl], hgnw_ref[:, hl]))
    g = proj(wg_ref[...])
    nc = (jnp.concatenate(normed, axis=-1) * (g * _sigmoid(g))).astype(BF16)
    for j in range(D_MODEL // POST_BLK):
        cols = slice(j * POST_BLK, (j + 1) * POST_BLK)
        dot = lambda a, w_ref, off: jnp.dot(a, w_ref[:, off + j * POST_BLK:off + (j + 1) * POST_BLK],
                                            preferred_element_type=F32)
        out_a = dot(ga, wglu_ref, 0) * _sigmoid(dot(ga, wglu_ref, D_MODEL))
        out_b = dot(nb_, wssd_ref, 0)
        out_c = dot(nc, whg_ref, 0)
        mixed = (_sigmoid(dot(hb, wgt_ref, 0)) * out_a + _sigmoid(dot(hb, wgt_ref, D_MODEL)) * out_b
                 + _sigmoid(dot(hb, wgt_ref, 2 * D_MODEL)) * out_c)
        mix_ref[:, cols] = mixed.astype(BF16)
    x1_ref[...] = x + jnp.dot(mix_ref[...], wo_ref[...], preferred_element_type=F32)


def _post_call(x2d, ya, yb, oc, lp):
    rows = x2d.shape[0]
    tm = BIG_TM
    seq_in = (x2d, ya, yb, oc)
    consts = (lp['n1'], lp['w_z'], lp['w_g'], lp['w_gt'],
              lp['w_glu'], lp['ssd_nw'], lp['w_ssd'], lp['hg_nw'], lp['w_hg'], lp['w_o'])
    row = lambda c: pl.BlockSpec((tm, c), lambda i: (i, 0))
    return pl.pallas_call(
        _post_kernel,
        grid=(rows // tm,),
        in_specs=[row(a.shape[-1]) for a in seq_in] + [_const_spec(c.shape) for c in consts],
        out_specs=row(D_MODEL),
        out_shape=jax.ShapeDtypeStruct((rows, D_MODEL), F32),
        scratch_shapes=[pltpu.VMEM((tm, D_MODEL), BF16)],
        compiler_params=_params(("parallel",)),
        name="post_merge",
    )(*seq_in, *consts)


def _ffn_kernel(final, seq_rows, tm, x_ref, *rest):
    if seq_rows:
        (sp1_ref, sp2_ref, n2_ref, wup_ref, cw_ref, cb_ref, wdn_ref, nf_ref,
         out_ref, aff_ref, hm_ref) = rest
        t = lax.broadcasted_iota(jnp.int32, (tm, 1), 0) % seq_rows
    else:
        (n2_ref, wup_ref, cw_ref, cb_ref, wdn_ref, nf_ref, out_ref, aff_ref, hm_ref, carry_ref) = rest

        @pl.when(pl.program_id(1) == 0)
        def _():
            carry_ref[...] = jnp.zeros_like(carry_ref)
    tail = FFN_CONV - 1
    x = x_ref[...]
    hb = _rms(x, n2_ref[...]).astype(BF16)
    for j in range(D_FF // FFN_BLK):
        cols = slice(j * FFN_BLK, (j + 1) * FFN_BLK)
        a = jnp.dot(hb, wup_ref[:, cols], preferred_element_type=F32)
        g = jnp.dot(hb, wup_ref[:, D_FF + j * FFN_BLK:D_FF + (j + 1) * FFN_BLK],
                    preferred_element_type=F32)
        p1 = pltpu.roll(a, 1, axis=0)
        p2 = pltpu.roll(a, 2, axis=0)
        if seq_rows:
            p1 = jnp.where(t < 1, sp1_ref[:, cols], p1)
            p2 = jnp.where(t < 2, sp2_ref[:, cols], p2)
            aff_ref[:, cols] = a
        else:
            head = jnp.concatenate([carry_ref[:, cols], a[0:SUBLANE, :]], axis=0)
            p1 = jnp.concatenate([pltpu.roll(head, 1, axis=0)[SUBLANE:], p1[SUBLANE:, :]], axis=0)
            p2 = jnp.concatenate([pltpu.roll(head, 2, axis=0)[SUBLANE:], p2[SUBLANE:, :]], axis=0)
            carry_ref[:, cols] = a[tm - SUBLANE:tm, :]
            aff_ref[0, :, cols] = a[tm - tail:tm, :]
        ac = (cw_ref[0:1, cols] * p2 + cw_ref[1:2, cols] * p1 + cw_ref[2:3, cols] * a
              + cb_ref[:, cols])
        hm_ref[:, cols] = (ac * _sigmoid(ac) * g).astype(BF16)
    y = x + jnp.dot(hm_ref[...], wdn_ref[...], preferred_element_type=F32)
    if final:
        y = _rms(y, nf_ref[...])
    out_ref[...] = y


def _ffn_prompt_call(x1, lp, nf, final, nseq, seqlen):
    tm = BIG_TM
    nchunk = seqlen // tm
    consts = (lp['n2'], lp['w_up'], lp['ffn_cw'], lp['ffn_cb'], lp['w_dn'], nf)
    row = pl.BlockSpec((tm, D_MODEL), lambda b, c: (b * nchunk + c, 0))
    return pl.pallas_call(
        functools.partial(_ffn_kernel, final, 0, tm),
        grid=(nseq, nchunk),
        in_specs=[row] + [_const_spec(c.shape) for c in consts],
        out_specs=[row, pl.BlockSpec((1, FFN_CONV - 1, D_FF), lambda b, c: (b, 0, 0))],
        out_shape=[jax.ShapeDtypeStruct(x1.shape, F32),
                   jax.ShapeDtypeStruct((nseq, FFN_CONV - 1, D_FF), F32)],
        scratch_shapes=[pltpu.VMEM((tm, D_FF), BF16), pltpu.VMEM((SUBLANE, D_FF), F32)],
        compiler_params=_params(("parallel", "arbitrary")),
        name="ffn_prompt",
    )(x1, *consts)


def _ffn_sample_call(x1, sp1, sp2, lp, nf, final):
    rows = x1.shape[0]
    tm = SMALL_TM
    consts = (lp['n2'], lp['w_up'], lp['ffn_cw'], lp['ffn_cb'], lp['w_dn'], nf)
    row = lambda c: pl.BlockSpec((tm, c), lambda i: (i, 0))
    return pl.pallas_call(
        functools.partial(_ffn_kernel, final, SAMPLE_PAD, tm),
        grid=(rows // tm,),
        in_specs=[row(D_MODEL), row(D_FF), row(D_FF)] + [_const_spec(c.shape) for c in consts],
        out_specs=[row(D_MODEL), row(D_FF)],
        out_shape=[jax.ShapeDtypeStruct(x1.shape, F32), jax.ShapeDtypeStruct((rows, D_FF), F32)],
        scratch_shapes=[pltpu.VMEM((tm, D_FF), BF16)],
        compiler_params=_params(("arbitrary",)),
        name="ffn_sample",
    )(x1, sp1, sp2, *consts)


def _layer_params(l, lb, norm1_w, w_in, s5_log_dt, s5_lambda_re, s5_lambda_im, s5_b_re, s5_b_im,
                  s5_c_re, s5_c_im, s5_d, s5_w_glu, ssd_conv_w, ssd_conv_b, ssd_dt_bias, ssd_a_log,
                  ssd_d, ssd_norm_w, ssd_w_out, hg_norm_w, hg_w_out, w_o, norm2_w, ffn_w_up,
                  ffn_conv_w, ffn_conv_b, ffn_w_down):
    row = lambda v: v.astype(F32).reshape(1, -1)
    lp = {}
    lp['n1'] = row(norm1_w[l])
    widths = (S5_WIDTH, SSD_INNER, SSD_CONV_DIM, SSD_HEADS, HG_WIDTH, HG_WIDTH, HG_WIDTH, HG_WIDTH,
              3 * D_MODEL)
    names = ('w_ua', 'w_z', 'w_xbc', 'w_dt', 'w_q', 'w_f', 'w_i', 'w_g', 'w_gt')
    off = 0
    for name, w in zip(names, widths):
        lp[name] = w_in[l][:, off:off + w].astype(BF16)
        off += w
    lp['w_dt'] = jnp.pad(lp['w_dt'], ((0, 0), (0, LANE - SSD_HEADS)))
    lp['dtb'] = jnp.pad(row(ssd_dt_bias[l]), ((0, 0), (0, LANE - SSD_HEADS)))
    lp['loglb'] = jnp.log(row(lb))
    lp['log1mlb'] = jnp.log1p(-row(lb))
    lp['omlb'] = 1.0 - row(lb)

    delta = jnp.exp(s5_log_dt[l].astype(F32))[:, None]
    lr = s5_lambda_re[l].astype(F32)
    li = s5_lambda_im[l].astype(F32)
    mag = jnp.exp(lr * delta)
    ab_re = mag * jnp.cos(li * delta)
    ab_im = mag * jnp.sin(li * delta)
    den = lr * lr + li * li
    nr = ab_re - 1.0
    co_re = (nr * lr + ab_im * li) / den
    co_im = (ab_im * lr - nr * li) / den
    br = s5_b_re[l].astype(F32)
    bi = s5_b_im[l].astype(F32)
    bb_re = co_re[..., None] * br - co_im[..., None] * bi
    bb_im = co_re[..., None] * bi + co_im[..., None] * br
    eye = jnp.eye(S5_GROUPS, dtype=F32)
    bd_in = lambda m: jnp.einsum('gnj,gh->gjhn', m, eye).reshape(S5_WIDTH, S5_LANES)
    bd_out = lambda m: jnp.einsum('gjn,gh->gnhj', m, eye).reshape(S5_LANES, S5_WIDTH)
    lp['s5_bb'] = jnp.concatenate([bd_in(bb_re), bd_in(bb_im)], axis=1).astype(BF16)
    lp['s5_cc'] = jnp.concatenate([bd_out(s5_c_re[l].astype(F32)),
                                   -bd_out(s5_c_im[l].astype(F32))], axis=0).astype(BF16)
    pw = jnp.asarray(list(range(1, S5_R + 1)) + [2 * S5_R, 4 * S5_R], F32)[:, None, None]
    pm = jnp.exp(lr * delta * pw)
    ph = li * delta * pw
    pad = ((0, 40 - (S5_R + 2)), (0, 0))
    lp['s5_apr'] = jnp.pad((pm * jnp.cos(ph)).reshape(-1, S5_LANES), pad)
    lp['s5_api'] = jnp.pad((pm * jnp.sin(ph)).reshape(-1, S5_LANES), pad)
    lp['s5_apr8'] = jnp.repeat(lp['s5_apr'][:S5_R + 2], S5_CH, axis=0)
    lp['s5_api8'] = jnp.repeat(lp['s5_api'][:S5_R + 2], S5_CH, axis=0)
    npair = S5_LANES // LANE
    pair_cols = lambda m_re, m_im: jnp.stack(
        [m_re.reshape(-1, npair, LANE), m_im.reshape(-1, npair, LANE)], axis=2).reshape(-1, 2 * S5_LANES)
    bb_int = pair_cols(bd_in(bb_re), bd_in(bb_im)).astype(BF16)
    cc_int = pair_cols(bd_out(s5_c_re[l].astype(F32)).T, -bd_out(s5_c_im[l].astype(F32)).T).T.astype(BF16)
    kw = 2 * LANE
    nlc = 2 * S5_LANES // S5_LC
    ktile = lambda lc: lc * S5_LC // (2 * S5_LANES // (S5_WIDTH // kw))
    lp['s5_bbc'] = jnp.stack([bb_int[ktile(lc) * kw:(ktile(lc) + 1) * kw, lc * S5_LC:(lc + 1) * S5_LC]
                              for lc in range(nlc)])
    lp['s5_ccc'] = jnp.stack([cc_int[lc * S5_LC:(lc + 1) * S5_LC, ktile(lc) * kw:(ktile(lc) + 1) * kw]
                              for lc in range(nlc)])
    lp['s5_d'] = row(s5_d[l])
    lp['w_glu'] = s5_w_glu[l].astype(BF16)

    lp['conv_w'] = ssd_conv_w[l].astype(F32)
    lp['conv_b'] = row(ssd_conv_b[l])
    lp['ssd_arow'] = jnp.pad(-jnp.exp(row(ssd_a_log[l])), ((0, 0), (0, LANE - SSD_HEADS)))
    lp['ssd_dexp'] = jnp.repeat(row(ssd_d[l]), SSD_HEAD_DIM, axis=1)
    lp['ssd_nw'] = row(ssd_norm_w[l])
    lp['w_ssd'] = ssd_w_out[l].astype(BF16)
    lp['hg_nw'] = row(hg_norm_w[l])
    lp['w_hg'] = hg_w_out[l].astype(BF16)
    lp['w_o'] = w_o[l].astype(BF16)
    lp['n2'] = row(norm2_w[l])
    lp['w_up'] = ffn_w_up[l].astype(BF16)
    lp['ffn_cw'] = ffn_conv_w[l].astype(F32)
    lp['ffn_cb'] = row(ffn_conv_b[l])
    lp['w_dn'] = ffn_w_down[l].astype(BF16)
    return lp


_PRE_KEYS = ('ua', 'xc', 'dt', 'q', 'lf', 'kk', 'v')


def kernel(x_prompt, x_sample, state_s5_re, state_s5_im, state_ssd, state_ssd_conv, state_hgrn, state_ffn_conv, norm1_w, w_in, s5_log_dt, s5_lambda_re, s5_lambda_im, s5_b_re, s5_b_im, s5_c_re, s5_c_im, s5_d, s5_w_glu, ssd_conv_w, ssd_conv_b, ssd_dt_bias, ssd_a_log, ssd_d, ssd_norm_w, ssd_w_out, hg_lb_logits, hg_norm_w, hg_w_out, w_o, norm2_w, ffn_w_up, ffn_conv_w, ffn_conv_b, ffn_w_down, norm_f_w):
    nb_p, len_p, _ = x_prompt.shape
    nb_s, len_s, _ = x_sample.shape
    assert len_p % S5_T == 0 and len_p % MIX_T == 0 and len_p % SMALL_TM == 0 and len_p % BIG_TM == 0
    assert len_s <= SAMPLE_PAD and len_s >= SSD_CONV - 1
    assert (nb_s * SAMPLE_PAD) % SMALL_TM == 0 and (nb_s * SAMPLE_PAD) % BIG_TM == 0
    lb_cum = jnp.cumsum(jax.nn.softmax(hg_lb_logits.astype(F32), axis=0), axis=0)
    lb_all = lb_cum - lb_cum[0:1]
    nf = norm_f_w.astype(F32).reshape(1, -1)

    xp = x_prompt.astype(F32).reshape(nb_p * len_p, D_MODEL)
    xs = jnp.pad(x_sample.astype(F32), ((0, 0), (0, SAMPLE_PAD - len_s), (0, 0)))
    xs = xs.reshape(nb_s * SAMPLE_PAD, D_MODEL)
    zeros_p = (jnp.zeros((1, nb_p, SSD_INNER, SSD_STATE), F32),
               jnp.zeros((1, nb_p, HG_WIDTH, HG_VDIM), F32))
    states_s = (state_ssd.astype(F32).reshape(DEPTH, nb_s, SSD_INNER, SSD_STATE),
                state_hgrn.astype(F32).reshape(DEPTH, nb_s, HG_WIDTH, HG_VDIM))
    mix_p = None
    mix_s = None
    new_p = []
    new_s = []
    pad_t = lambda rows: jnp.concatenate(
        rows + [jnp.zeros((nb_s, SAMPLE_PAD - len(rows), rows[0].shape[-1]), F32)], axis=1
    ).reshape(nb_s * SAMPLE_PAD, rows[0].shape[-1])
    for l in range(DEPTH):
        lp = _layer_params(l, lb_all[l], norm1_w, w_in, s5_log_dt, s5_lambda_re, s5_lambda_im,
                           s5_b_re, s5_b_im, s5_c_re, s5_c_im, s5_d, s5_w_glu, ssd_conv_w,
                           ssd_conv_b, ssd_dt_bias, ssd_a_log, ssd_d, ssd_norm_w, ssd_w_out,
                           hg_norm_w, hg_w_out, w_o, norm2_w, ffn_w_up, ffn_conv_w, ffn_conv_b,
                           ffn_w_down)
        final = l == DEPTH - 1

        *pre, p_conv = _pre_prompt_call(xp, lp, nb_p, len_p)
        pre = dict(zip(_PRE_KEYS, pre))
        ya, p_re, p_im = _s5_prompt_call(pre['ua'], lp, nb_p, len_p)
        yb, oc, *mix_p = _mix_call(pre, zeros_p, 0, l, mix_p, lp, nb_p, len_p, MIX_T, MIX_T,
                                   MIX_NB_PROMPT, SSD_Q, HG_Q)
        x1 = _post_call(xp, ya, yb.reshape(-1, SSD_INNER), oc.reshape(-1, HG_WIDTH), lp)
        xp, p_ffn = _ffn_prompt_call(x1, lp, nf, final, nb_p, len_p)
        new_p.append((p_re.reshape(nb_p, S5_GROUPS, S5_STATE), p_im.reshape(nb_p, S5_GROUPS, S5_STATE),
                      p_conv, p_ffn))

        cs = state_ssd_conv[l].astype(F32)
        r = lambda i: cs[:, i:i + 1]
        *pre, raw = _pre_sample_call(xs, pad_t([r(2)]), pad_t([r(1), r(2)]), pad_t([r(0), r(1), r(2)]), lp)
        pre = dict(zip(_PRE_KEYS, pre))
        s_conv = raw.reshape(nb_s, SAMPLE_PAD, SSD_CONV_DIM)[:, len_s - (SSD_CONV - 1):len_s]
        ua_tm = pre['ua'].reshape(nb_s, SAMPLE_PAD, S5_WIDTH)[:, :len_s].transpose(1, 0, 2)
        ya_tm, s_re, s_im = _s5_sample_call(
            ua_tm.reshape(len_s * nb_s, S5_WIDTH),
            state_s5_re[l].astype(F32).reshape(nb_s, S5_LANES),
            state_s5_im[l].astype(F32).reshape(nb_s, S5_LANES), lp, len_s, nb_s)
        ya = jnp.pad(ya_tm.reshape(len_s, nb_s, S5_WIDTH).transpose(1, 0, 2),
                     ((0, 0), (0, SAMPLE_PAD - len_s), (0, 0))).reshape(nb_s * SAMPLE_PAD, S5_WIDTH)
        yb, oc, *mix_s = _mix_call(pre, states_s, l, l, mix_s, lp, nb_s, SAMPLE_PAD, SAMPLE_PAD, len_s,
                                   MIX_NB_SAMPLE, SAMPLE_PAD, SAMPLE_PAD)
        x1 = _post_call(xs, ya, yb.reshape(-1, SSD_INNER), oc.reshape(-1, HG_WIDTH), lp)
        st = state_ffn_conv[l].astype(F32)
        f = lambda i: st[:, i:i + 1]
        xs, aff = _ffn_sample_call(x1, pad_t([f(1)]), pad_t([f(0), f(1)]), lp, nf, final)
        s_ffn = aff.reshape(nb_s, SAMPLE_PAD, D_FF)[:, len_s - (FFN_CONV - 1):len_s]
        new_s.append((s_re.reshape(nb_s, S5_GROUPS, S5_STATE), s_im.reshape(nb_s, S5_GROUPS, S5_STATE),
                      s_conv, s_ffn))

    stk = lambda lst, i: jnp.stack([s[i] for s in lst])
    y_prompt = xp.reshape(nb_p, len_p, D_MODEL)
    y_sample = xs.reshape(nb_s, SAMPLE_PAD, D_MODEL)[:, :len_s]

    def mix_states(m, n):
        ssd, hg = m
        return (ssd.reshape(DEPTH, n, SSD_HEADS, SSD_HEAD_DIM, SSD_STATE),
                hg.reshape(DEPTH, n, HG_HEADS, HG_KDIM, HG_VDIM))

    p_ssd, p_hg = mix_states(mix_p, nb_p)
    s_ssd, s_hg = mix_states(mix_s, nb_s)
    return (y_prompt, y_sample,
            stk(new_p, 0), stk(new_p, 1), p_ssd, stk(new_p, 2), p_hg, stk(new_p, 3),
            stk(new_s, 0), stk(new_s, 1), s_ssd, stk(new_s, 2), s_hg, stk(new_s, 3))
```

```python
import functools
import math

import jax
import jax.numpy as jnp
import numpy as np
from jax import lax
from jax.experimental import pallas as pl
from jax.experimental.pallas import tpu as pltpu

F32 = jnp.float32
BF16 = jnp.bfloat16

D_MODEL = 1024
DEPTH = 2
S5_WIDTH = 512
S5_GROUP = 16
S5_GROUPS = 32
S5_STATE = 64
S5_LANES = S5_GROUPS * S5_STATE
SSD_INNER = 1024
SSD_HEAD_DIM = 64
SSD_HEADS = 16
SSD_GROUPS = 4
SSD_STATE = 128
SSD_CONV = 4
SSD_CONV_DIM = 2048
HG_WIDTH = 512
HG_HEADS = 4
HG_KDIM = 128
HG_VDIM = 128
D_FF = 2816
FFN_CONV = 3
EPS = 1e-6

LANE = 128
SUBLANE = 8
SAMPLE_PAD = 8
HG_SAFE_LOG = -80.0

BIG_TM = 512
SMALL_TM = 256
PRE_BLK = 256
POST_BLK = 256
FFN_BLK = 256
S5_T = 256
S5_CH = 8
S5_R = S5_T // S5_CH
S5_LC = 512
MIX_T = 128
SSD_Q = 128
HG_Q = 64
MIX_STAGGER = 3
MIX_STAGGER_MOD = 7
MIX_NB_PROMPT = 2
MIX_NB_SAMPLE = 8
VMEM_LIMIT = 56 * 1024 * 1024


def _const_spec(shape):
    nd = len(shape)
    return pl.BlockSpec(shape, lambda *_: (0,) * nd, pipeline_mode=pl.Buffered(1))


def _params(sem):
    return pltpu.CompilerParams(dimension_semantics=sem, vmem_limit_bytes=VMEM_LIMIT)


def _bdot(a, b):
    return jnp.dot(a.astype(BF16), b.astype(BF16), preferred_element_type=F32)


def _bdot_nt(a, b):
    return lax.dot_general(a.astype(BF16), b.astype(BF16), (((1,), (1,)), ((), ())),
                           preferred_element_type=F32)


def _bdot_tn(a, b):
    return lax.dot_general(a.astype(BF16), b.astype(BF16), (((0,), (0,)), ((), ())),
                           preferred_element_type=F32)


def _split3(x):
    h = x.astype(BF16)
    r = x - h.astype(F32)
    m = r.astype(BF16)
    l = (r - m.astype(F32)).astype(BF16)
    return h, m, l


def _split_cat(x, n):
    return jnp.concatenate(_split3(x)[:n], axis=1)


def _sel_left(m01, x):
    h, m, l = _split3(x)
    d = lambda p: jnp.dot(m01, p, preferred_element_type=F32)
    return (d(h) + d(m)) + d(l)


def _sel_right(x, m01):
    h, m, l = _split3(x)
    d = lambda p: jnp.dot(p, m01, preferred_element_type=F32)
    return (d(h) + d(m)) + d(l)


def _sigmoid(x):
    return 1.0 / (1.0 + jnp.exp(-x))


def _softplus(x):
    return jnp.maximum(x, 0.0) + jnp.log1p(jnp.exp(-jnp.abs(x)))


def _rms(x, w):
    return x * lax.rsqrt(jnp.mean(x * x, axis=-1, keepdims=True) + EPS) * w


def _pre_kernel(seq_rows, tm, x_ref, *rest):
    if seq_rows:
        (s1_ref, s2_ref, s3_ref, n1_ref, w_ua, w_xbc, w_dt, w_q, w_f, w_i, cw_ref, cb_ref,
         dtb_ref, loglb_ref, log1mlb_ref, omlb_ref,
         ua_o, xc_o, dt_o, q_o, lf_o, kk_o, v_o, raw_o) = rest
        t = lax.broadcasted_iota(jnp.int32, (tm, 1), 0) % seq_rows
    else:
        (n1_ref, w_ua, w_xbc, w_dt, w_q, w_f, w_i, cw_ref, cb_ref,
         dtb_ref, loglb_ref, log1mlb_ref, omlb_ref,
         ua_o, xc_o, dt_o, q_o, lf_o, kk_o, v_o, cst_o, carry_ref) = rest

        @pl.when(pl.program_id(1) == 0)
        def _():
            carry_ref[...] = jnp.zeros_like(carry_ref)
    tail = SSD_CONV - 1
    hb = _rms(x_ref[...], n1_ref[...]).astype(BF16)
    dot = lambda w: jnp.dot(hb, w, preferred_element_type=F32)
    ua_o[...] = dot(w_ua[...])
    for j in range(SSD_CONV_DIM // PRE_BLK):
        cols = slice(j * PRE_BLK, (j + 1) * PRE_BLK)
        raw = dot(w_xbc[:, cols])
        if seq_rows:
            acc = cw_ref[tail:tail + 1, cols] * raw + cb_ref[:, cols]
            for k, s_ref in ((1, s1_ref), (2, s2_ref), (3, s3_ref)):
                prev = jnp.where(t < k, s_ref[:, cols], pltpu.roll(raw, k, axis=0))
                acc = acc + cw_ref[tail - k:tail - k + 1, cols] * prev
            raw_o[:, cols] = raw
        else:
            xx = jnp.concatenate([carry_ref[:, cols], raw], axis=0)
            acc = cw_ref[0:1, cols] * xx
            for k in range(1, SSD_CONV):
                acc = pltpu.roll(acc, 1, axis=0) + cw_ref[k:k + 1, cols] * xx
            acc = acc[SUBLANE:, :] + cb_ref[:, cols]
            carry_ref[:, cols] = raw[tm - SUBLANE:tm, :]
            cst_o[0, :, cols] = raw[tm - tail:tm, :]
        xc_o[:, cols] = acc * _sigmoid(acc)
    dt_o[...] = _softplus(dot(w_dt[...]) + dtb_ref[...])
    q_o[...] = dot(w_q[...])
    zf = dot(w_f[...])
    log_sig = jnp.minimum(zf, 0.0) - jnp.log1p(jnp.exp(-jnp.abs(zf)))
    a = loglb_ref[...]
    b = log1mlb_ref[...] + log_sig
    lf_o[...] = jnp.maximum(a, b) + jnp.log1p(jnp.exp(-jnp.abs(a - b)))
    kk_o[...] = omlb_ref[...] * _sigmoid(-zf)
    v_o[...] = dot(w_i[...])


_PRE_WIDTHS = (S5_WIDTH, SSD_CONV_DIM, LANE, HG_WIDTH, HG_WIDTH, HG_WIDTH, HG_WIDTH)


def _pre_consts(lp):
    return (lp['n1'], lp['w_ua'], lp['w_xbc'], lp['w_dt'], lp['w_q'], lp['w_f'], lp['w_i'],
            lp['conv_w'], lp['conv_b'], lp['dtb'], lp['loglb'], lp['log1mlb'], lp['omlb'])


def _pre_prompt_call(x2d, lp, nseq, seqlen):
    tm = BIG_TM
    nchunk = seqlen // tm
    consts = _pre_consts(lp)
    row = lambda c: pl.BlockSpec((tm, c), lambda b, k: (b * nchunk + k, 0))
    tail = SSD_CONV - 1
    return pl.pallas_call(
        functools.partial(_pre_kernel, 0, tm),
        grid=(nseq, nchunk),
        in_specs=[row(D_MODEL)] + [_const_spec(c.shape) for c in consts],
        out_specs=[row(c) for c in _PRE_WIDTHS]
        + [pl.BlockSpec((1, tail, SSD_CONV_DIM), lambda b, k: (b, 0, 0))],
        out_shape=[jax.ShapeDtypeStruct((x2d.shape[0], c), F32) for c in _PRE_WIDTHS]
        + [jax.ShapeDtypeStruct((nseq, tail, SSD_CONV_DIM), F32)],
        scratch_shapes=[pltpu.VMEM((SUBLANE, SSD_CONV_DIM), F32)],
        compiler_params=_params(("parallel", "arbitrary")),
        name="pre_proj_prompt",
    )(x2d, *consts)


def _pre_sample_call(x2d, s1, s2, s3, lp):
    rows = x2d.shape[0]
    tm = SMALL_TM
    consts = _pre_consts(lp)
    row = lambda c: pl.BlockSpec((tm, c), lambda i: (i, 0))
    return pl.pallas_call(
        functools.partial(_pre_kernel, SAMPLE_PAD, tm),
        grid=(rows // tm,),
        in_specs=[row(D_MODEL)] + [row(SSD_CONV_DIM)] * 3 + [_const_spec(c.shape) for c in consts],
        out_specs=[row(c) for c in _PRE_WIDTHS] + [row(SSD_CONV_DIM)],
        out_shape=[jax.ShapeDtypeStruct((rows, c), F32) for c in _PRE_WIDTHS]
        + [jax.ShapeDtypeStruct((rows, SSD_CONV_DIM), F32)],
        compiler_params=_params(("parallel",)),
        name="pre_proj_sample",
    )(x2d, s1, s2, s3, *consts)


def _cmul_add(ar, ai, xr, xi, br, bi):
    return ar * xr - ai * xi + br, ar * xi + ai * xr + bi


def _s5_prompt_kernel(ua_ref, perm_ref, permt_ref, bbc_ref, ccc_ref, apr_ref, api_ref, d_ref,
                      ya_ref, st_ref, bu_ref):
    @pl.when(pl.program_id(1) == 0)
    def _():
        st_ref[...] = jnp.zeros_like(st_ref)

    ua = ua_ref[...]
    up = jnp.dot(perm_ref[...], ua.astype(BF16), preferred_element_type=F32).astype(BF16)
    rowid = lax.broadcasted_iota(jnp.int32, (S5_CH, LANE), 0)
    tab = lambda ref, k, gp: ref[k * S5_CH:(k + 1) * S5_CH, gp * LANE:(gp + 1) * LANE]
    kw = 2 * LANE
    y_acc = [None] * (S5_WIDTH // kw)

    for lc in range(2 * S5_LANES // S5_LC):
        kt = lc * S5_LC // (2 * S5_LANES // (S5_WIDTH // kw))
        cols = slice(lc * S5_LC, (lc + 1) * S5_LC)
        bu_ref[:, cols] = jnp.dot(up[:, kt * kw:(kt + 1) * kw], bbc_ref[lc], preferred_element_type=F32)
        for gp in range(lc * S5_LC // kw, (lc + 1) * S5_LC // kw):
            re = slice(gp * kw, gp * kw + LANE)
            im = slice(gp * kw + LANE, (gp + 1) * kw)
            a_r, a_i = tab(apr_ref, 0, gp), tab(api_ref, 0, gp)
            sr = jnp.zeros((S5_CH, LANE), F32)
            si = jnp.zeros((S5_CH, LANE), F32)
            for r in range(S5_R):
                rows = slice(r * S5_CH, (r + 1) * S5_CH)
                sr, si = _cmul_add(a_r, a_i, sr, si, bu_ref[rows, re], bu_ref[rows, im])
                bu_ref[rows, re] = sr
                bu_ref[rows, im] = si
            pr = jnp.broadcast_to(st_ref[0, :, re], (S5_CH, LANE))
            pi = jnp.broadcast_to(st_ref[0, :, im], (S5_CH, LANE))
            vr = jnp.where(rowid == 0, pr, pltpu.roll(sr, 1, axis=0))
            vi = jnp.where(rowid == 0, pi, pltpu.roll(si, 1, axis=0))
            for d, k in ((1, S5_R - 1), (2, S5_R), (4, S5_R + 1)):
                keep = rowid >= d
                tr = jnp.where(keep, pltpu.roll(vr, d, axis=0), 0.0)
                ti = jnp.where(keep, pltpu.roll(vi, d, axis=0), 0.0)
                vr, vi = _cmul_add(tab(apr_ref, k, gp), tab(api_ref, k, gp), tr, ti, vr, vi)
            for r in range(S5_R):
                rows = slice(r * S5_CH, (r + 1) * S5_CH)
                xr, xi = _cmul_add(tab(apr_ref, r, gp), tab(api_ref, r, gp), vr, vi,
                                   bu_ref[rows, re], bu_ref[rows, im])
                bu_ref[rows, re] = xr
                bu_ref[rows, im] = xi
            st_ref[0, :, re] = xr[S5_CH - 1:S5_CH, :]
            st_ref[0, :, im] = xi[S5_CH - 1:S5_CH, :]
        part = _bdot(bu_ref[:, cols], ccc_ref[lc])
        y_acc[kt] = part if y_acc[kt] is None else y_acc[kt] + part

    yp = jnp.concatenate(y_acc, axis=1)
    ya_ref[...] = _sel_left(permt_ref[...], yp) + d_ref[...] * ua


def _s5_perm():
    p = np.zeros((S5_T, S5_T), np.float32)
    for i in range(S5_T):
        p[i, (i % S5_CH) * S5_R + i // S5_CH] = 1.0
    return jnp.asarray(p, BF16), jnp.asarray(p.T, BF16)


def _s5_prompt_call(ua, lp, nseq, seqlen):
    nchunk = seqlen // S5_T
    perm, permt = _s5_perm()
    consts = (perm, permt, lp['s5_bbc'], lp['s5_ccc'], lp['s5_apr8'], lp['s5_api8'], lp['s5_d'])
    st_spec = pl.BlockSpec((1, 1, 2 * S5_LANES), lambda b, c: (b, 0, 0))
    st_shape = jax.ShapeDtypeStruct((nseq, 1, 2 * S5_LANES), F32)
    ya, st = pl.pallas_call(
        _s5_prompt_kernel,
        grid=(nseq, nchunk),
        in_specs=[pl.BlockSpec((S5_T, S5_WIDTH), lambda b, c: (b * nchunk + c, 0))]
        + [_const_spec(c.shape) for c in consts],
        out_specs=[pl.BlockSpec((S5_T, S5_WIDTH), lambda b, c: (b * nchunk + c, 0)), st_spec],
        out_shape=[jax.ShapeDtypeStruct(ua.shape, F32), st_shape],
        scratch_shapes=[pltpu.VMEM((S5_T, 2 * S5_LANES), F32)],
        compiler_params=_params(("parallel", "arbitrary")),
        name="s5_prompt",
    )(ua, *consts)
    st = st.reshape(nseq, S5_LANES // LANE, 2, LANE)
    return ya, st[:, :, 0].reshape(nseq, S5_LANES), st[:, :, 1].reshape(nseq, S5_LANES)


def _s5_sample_kernel(nstep, nseq, ua_ref, sre0_ref, sim0_ref, bb_ref, cc_ref, apr_ref, api_ref,
                      d_ref, ya_ref, sre_ref, sim_ref, bu_ref, xb_ref):
    ua = ua_ref[...]
    bu_ref[...] = jnp.dot(ua.astype(BF16), bb_ref[...], preferred_element_type=F32)
    for lc in range(S5_LANES // S5_LC):
        re = slice(lc * S5_LC, (lc + 1) * S5_LC)
        im = slice(S5_LANES + lc * S5_LC, S5_LANES + (lc + 1) * S5_LC)
        a_r = apr_ref[0:1, re]
        a_i = api_ref[0:1, re]
        sr = sre0_ref[:, re]
        si = sim0_ref[:, re]
        for t in range(nstep):
            rows = slice(t * nseq, (t + 1) * nseq)
            sr, si = _cmul_add(a_r, a_i, sr, si, bu_ref[rows, re], bu_ref[rows, im])
            xb_ref[rows, re] = sr.astype(BF16)
            xb_ref[rows, im] = si.astype(BF16)
        sre_ref[:, re] = sr
        sim_ref[:, re] = si
    ya_ref[...] = jnp.dot(xb_ref[...], cc_ref[...], preferred_element_type=F32) + d_ref[...] * ua


def _s5_sample_call(ua_tm, sre0, sim0, lp, nstep, nseq):
    rows = nstep * nseq
    args = (ua_tm, sre0, sim0, lp['s5_bb'], lp['s5_cc'], lp['s5_apr'], lp['s5_api'], lp['s5_d'])
    st_shape = jax.ShapeDtypeStruct((nseq, S5_LANES), F32)
    full = lambda s: pl.BlockSpec(s, lambda i: (0,) * len(s))
    return pl.pallas_call(
        functools.partial(_s5_sample_kernel, nstep, nseq),
        grid=(1,),
        in_specs=[full(a.shape) for a in args],
        out_specs=[full((rows, S5_WIDTH)), full((nseq, S5_LANES)), full((nseq, S5_LANES))],
        out_shape=[jax.ShapeDtypeStruct((rows, S5_WIDTH), F32), st_shape, st_shape],
        scratch_shapes=[pltpu.VMEM((rows, 2 * S5_LANES), F32), pltpu.VMEM((rows, 2 * S5_LANES), BF16)],
        compiler_params=_params(("arbitrary",)),
        name="s5_sample",
    )(*args)


def _mix_kernel(t_rows, n_valid, nb, ssd_q, hg_q, n_alias,
                xc_ref, dt_ref, q_ref, lf_ref, kk_ref, v_ref, ssd0_ref, hg0_ref,
                arow_ref, dexp_ref, ltri_s_ref, ltri_h_ref, colsel_ref, headsel_ref, *rest):
    (yb_ref, oc_ref, ssd_ref, hg_ref, hb_ref, hk_ref, ho_ref, hq_ref, hv_ref,
     hqe_ref, hkd_ref, hke_ref, ebt_ref, cb_ref, gm_ref, ce_ref, xw_ref, acb_ref, eab_ref,
     ac_ref, dtm_ref) = rest[n_alias:]
    assert t_rows == ssd_q

    @pl.when(pl.program_id(1) == 0)
    def _():
        ssd_ref[...] = ssd0_ref[...]
        hg_ref[...] = hg0_ref[...]

    n_rows = nb * t_rows
    if n_valid < t_rows:
        valid_all = (lax.broadcasted_iota(jnp.int32, (n_rows, 1), 0) % t_rows < n_valid).astype(F32)
    else:
        valid_all = None
    sq = lambda m: m if m.shape[0] == LANE else jnp.concatenate(
        [m, jnp.zeros((LANE - m.shape[0], LANE), F32)], axis=0)

    def ssd_prepare():
        dt = dt_ref[...].reshape(n_rows, LANE)
        if valid_all is not None:
            dt = dt * valid_all
        acum = _sel_left(ltri_s_ref[...], dt * arow_ref[...])
        alast = jnp.concatenate(
            [jnp.broadcast_to(acum[(b + 1) * t_rows - 1:(b + 1) * t_rows, :], (t_rows, LANE))
             for b in range(nb)], axis=0)
        eacm = jnp.exp(acum)
        wcol = dt * jnp.exp(alast - acum)
        ac_ref[...] = acum
        dtm_ref[...] = dt
        acb_ref[...] = jnp.dot(_split_cat(acum, 3), colsel_ref[...], preferred_element_type=F32)
        eab_ref[...] = jnp.dot(_split_cat(eacm, 2), colsel_ref[0:2 * LANE, :], preferred_element_type=F32)
        wx = jnp.dot(_split_cat(wcol, 2), headsel_ref[...], preferred_element_type=F32)
        xw_ref[...] = (xc_ref[:, :, 0:SSD_INNER].reshape(n_rows, SSD_INNER) * wx).astype(xw_ref.dtype)

    def ssd_chunk(j, r0):
        q_ = ssd_q
        rows = slice(r0, r0 + q_)
        brows = slice(j * t_rows + r0, j * t_rows + r0 + q_)
        acum_t = sq(ac_ref[brows, :]).T
        dt_t = sq(dtm_ref[brows, :]).T
        ela_t = jnp.exp(jnp.broadcast_to(acum_t[:, q_ - 1:q_], (LANE, LANE)))
        tri = lax.broadcasted_iota(jnp.int32, (q_, q_), 0) >= lax.broadcasted_iota(jnp.int32, (q_, q_), 1)
        hpg = SSD_HEADS // SSD_GROUPS
        gw = hpg * SSD_HEAD_DIM
        b0 = SSD_INNER
        c0 = SSD_INNER + SSD_GROUPS * SSD_STATE
        cgs = lambda g: xc_ref[j, rows, c0 + g * SSD_STATE:c0 + (g + 1) * SSD_STATE]
        bgs = lambda g: xc_ref[j, rows, b0 + g * SSD_STATE:b0 + (g + 1) * SSD_STATE]
        for g in range(SSD_GROUPS):
            cb_ref[j, g] = _bdot_nt(cgs(g), bgs(g))
        yield
        for g in range(SSD_GROUPS):
            cg = cgs(g)
            for r in range(hpg):
                h = g * hpg + r
                seg = acb_ref[brows, h * LANE:h * LANE + q_] - acum_t[h:h + 1, :q_]
                dec = jnp.exp(jnp.where(tri, seg, -jnp.inf))
                gm_ref[j, h] = (cb_ref[j, g] * dec * dt_t[h:h + 1, :q_]).astype(gm_ref.dtype)
                ce_ref[j, h] = (cg * eab_ref[brows, h * LANE:(h + 1) * LANE]).astype(ce_ref.dtype)
            yield
        for g in range(SSD_GROUPS):
            bgb = bgs(g).astype(BF16)
            sg = ssd_ref[j, g * gw:(g + 1) * gw, :]
            sgb = sg.astype(BF16)
            scale = []
            for r in range(hpg):
                h = g * hpg + r
                hl = slice(h * SSD_HEAD_DIM, (h + 1) * SSD_HEAD_DIM)
                xs_h = xc_ref[j, rows, hl]
                yb_ref[j, rows, hl] = (_bdot(gm_ref[j, h], xs_h)
                                       + _bdot_nt(ce_ref[j, h], sgb[r * SSD_HEAD_DIM:(r + 1) * SSD_HEAD_DIM, :])
                                       + dexp_ref[:, hl] * xs_h)
                scale.append(jnp.broadcast_to(ela_t[h:h + 1, :], (SSD_HEAD_DIM, SSD_STATE)))
            upd = _bdot_tn(xw_ref[brows, g * gw:(g + 1) * gw], bgb)
            ssd_ref[j, g * gw:(g + 1) * gw, :] = sg * jnp.concatenate(scale, axis=0) + upd
            yield

    def hg_prepare():
        flat = lambda ref: ref[...].reshape(n_rows, HG_WIDTH)
        lf = flat(lf_ref)
        kk = flat(kk_ref)
        if valid_all is not None:
            lf = lf * valid_all
            kk = kk * valid_all
        bc = _sel_left(ltri_h_ref[...], lf)
        hb_ref[...] = bc
        hk_ref[...] = kk
        safe = jnp.min(bc) >= HG_SAFE_LOG

        @pl.when(safe)
        def _():
            ho_ref[...] = jnp.zeros((n_rows, HG_WIDTH), F32)

        @pl.when(jnp.logical_not(safe))
        def _():
            ho_ref[...] = jnp.zeros((n_rows, HG_WIDTH), F32)
            hq_ref[...] = flat(q_ref)
            hv_ref[...] = flat(v_ref)
            rid = lax.broadcasted_iota(jnp.int32, (n_rows, 1), 0)

            def body(s, c):
                brow = hb_ref[pl.ds(s, 1), :]
                krow = hk_ref[pl.ds(s, 1), :]
                vrow = hv_ref[pl.ds(s, 1), :]
                w = hq_ref[...] * krow * jnp.exp(jnp.minimum(hb_ref[...] - brow, 0.0))
                same_chunk_end = (s // hg_q + 1) * hg_q
                w = jnp.where(jnp.logical_and(rid >= s, rid < same_chunk_end), w, 0.0)
                for h in range(HG_HEADS):
                    hl = slice(h * HG_KDIM, (h + 1) * HG_KDIM)
                    a = jnp.sum(w[:, hl], axis=-1, keepdims=True)
                    ho_ref[:, hl] = ho_ref[:, hl] + a * vrow[:, hl]
                return c

            lax.fori_loop(0, n_rows, body, 0)

        return safe

    def hg_chunk(j, r0, safe):
        q_ = hg_q
        rows = slice(r0, r0 + q_)
        brows = slice(j * t_rows + r0, j * t_rows + r0 + q_)
        qq = q_ref[j, rows, :]
        vv = v_ref[j, rows, :]
        bc = hb_ref[brows, :]
        kk = hk_ref[brows, :]
        blast = bc[q_ - 1:q_, :]
        mm = hqe_ref.dtype
        hqe_ref[brows, :] = (qq * jnp.exp(bc)).astype(mm)
        hkd_ref[brows, :] = (kk * jnp.exp(blast - bc)).astype(mm)
        hke_ref[brows, :] = (kk * jnp.exp(-bc)).astype(mm)
        eb = jnp.exp(blast)
        for h in range(HG_HEADS):
            hl = slice(h * HG_KDIM, (h + 1) * HG_KDIM)
            ebt_ref[j, h] = jnp.broadcast_to(eb[:, hl], (HG_KDIM, HG_KDIM)).T
        yield
        tri = lax.broadcasted_iota(jnp.int32, (q_, q_), 0) >= lax.broadcasted_iota(jnp.int32, (q_, q_), 1)
        for h in range(HG_HEADS):
            hl = slice(h * HG_KDIM, (h + 1) * HG_KDIM)
            qe = hqe_ref[brows, hl]
            att = jnp.where(tri, _bdot_nt(qe, hke_ref[brows, hl]), 0.0)
            intra = jnp.where(safe, _bdot(att, vv[:, hl]), ho_ref[brows, hl])
            sh = hg_ref[j, hl, :]
            oc_ref[j, rows, hl] = intra + _bdot(qe, sh)
            hg_ref[j, hl, :] = sh * ebt_ref[j, h] + _bdot_tn(hkd_ref[brows, hl], vv[:, hl])
            yield

    def ssd_seq(j):
        for c in range(t_rows // ssd_q):
            yield from ssd_chunk(j, c * ssd_q)

    def hg_seq(j, safe):
        for c in range(t_rows // hg_q):
            yield from hg_chunk(j, c * hg_q, safe)

    safe = hg_prepare()
    ssd_prepare()
    tasks = [t for j in range(nb) for t in (ssd_seq(j), hg_seq(j, safe))]
    for i, t in enumerate(tasks):
        for _ in range((MIX_STAGGER * i) % MIX_STAGGER_MOD):
            next(t, None)
    while tasks:
        for t in list(tasks):
            try:
                next(t)
            except StopIteration:
                tasks.remove(t)


def _ltri(n, q):
    return jnp.asarray(np.kron(np.eye(n // q, dtype=np.float32), np.tril(np.ones((q, q), np.float32))), BF16)


def _head_selectors():
    col = np.zeros((3, LANE, SSD_HEADS * LANE), np.float32)
    head = np.zeros((2, LANE, SSD_INNER), np.float32)
    for h in range(SSD_HEADS):
        col[:, h, h * LANE:(h + 1) * LANE] = 1.0
        head[:, h, h * SSD_HEAD_DIM:(h + 1) * SSD_HEAD_DIM] = 1.0
    return (jnp.asarray(col.reshape(3 * LANE, -1), BF16), jnp.asarray(head.reshape(2 * LANE, -1), BF16))


def _mix_call(pre, states, l_in, l_out, prev, lp, nseq, seqlen, t_rows, n_valid, nb, ssd_q, hg_q):
    nchunk = seqlen // t_rows
    view = lambda a: a.reshape(nseq, seqlen, a.shape[-1])
    seq_in = [view(pre[k]) for k in ('xc', 'dt', 'q', 'lf', 'kk', 'v')]
    consts = (lp['ssd_arow'], lp['ssd_dexp'], _ltri(nb * t_rows, ssd_q), _ltri(nb * t_rows, hg_q),
              *_head_selectors())
    tile = lambda c: pl.BlockSpec((nb, t_rows, c), lambda i, k: (i, k, 0))
    st = lambda a, l: pl.BlockSpec((None, nb) + a.shape[2:], lambda i, k: (l, i, 0, 0))
    prev = () if prev is None else tuple(prev)
    n_in = len(seq_in) + len(states) + len(consts)
    mm = BF16 if ssd_q % (2 * SUBLANE) == 0 and hg_q % (2 * SUBLANE) == 0 else F32
    kern = functools.partial(_mix_kernel, t_rows, n_valid, nb, ssd_q, hg_q, len(prev))
    return pl.pallas_call(
        kern,
        grid=(nseq // nb, nchunk),
        in_specs=[tile(a.shape[-1]) for a in seq_in] + [st(a, l_in) for a in states]
        + [_const_spec(c.shape) for c in consts]
        + [pl.BlockSpec(memory_space=pl.ANY) for _ in prev],
        out_specs=[tile(SSD_INNER), tile(HG_WIDTH)] + [st(a, l_out) for a in states],
        out_shape=[jax.ShapeDtypeStruct((nseq, seqlen, SSD_INNER), F32),
                   jax.ShapeDtypeStruct((nseq, seqlen, HG_WIDTH), F32)]
        + [jax.ShapeDtypeStruct((DEPTH,) + a.shape[1:], F32) for a in states],
        input_output_aliases={n_in + k: 2 + k for k in range(len(prev))},
        scratch_shapes=[pltpu.VMEM((nb * t_rows, HG_WIDTH), F32) for _ in range(5)]
        + [pltpu.VMEM((nb * t_rows, HG_WIDTH), mm) for _ in range(3)]
        + [pltpu.VMEM((nb, HG_HEADS, HG_KDIM, HG_KDIM), F32),
           pltpu.VMEM((nb, SSD_GROUPS, ssd_q, ssd_q), F32),
           pltpu.VMEM((nb, SSD_HEADS, ssd_q, ssd_q), mm),
           pltpu.VMEM((nb, SSD_HEADS, ssd_q, SSD_STATE), mm),
           pltpu.VMEM((nb * t_rows, SSD_INNER), mm),
           pltpu.VMEM((nb * t_rows, SSD_HEADS * LANE), F32),
           pltpu.VMEM((nb * t_rows, SSD_HEADS * LANE), F32),
           pltpu.VMEM((nb * t_rows, LANE), F32),
           pltpu.VMEM((nb * t_rows, LANE), F32)],
        compiler_params=_params(("parallel", "arbitrary")),
        name="ssd_hgrn_mix",
    )(*seq_in, *states, *consts, *prev)


def _post_kernel(x_ref, ya_ref, yb_ref, oc_ref, n1_ref, wz_ref, wg_ref, wgt_ref,
                 wglu_ref, ssdnw_ref, wssd_ref, hgnw_ref, whg_ref, wo_ref, x1_ref, mix_ref):
    x = x_ref[...]
    hb = _rms(x, n1_ref[...]).astype(BF16)
    proj = lambda w: jnp.dot(hb, w, preferred_element_type=F32)
    ya = ya_ref[...]
    cdf = 0.5 * (1.0 + jnp.tanh(math.sqrt(2.0 / math.pi) * (ya + 0.044715 * (ya * ya * ya))))
    ga = (ya * cdf).astype(BF16)
    z = proj(wz_ref[...])
    nb_ = _rms(yb_ref[...] * (z * _sigmoid(z)), ssdnw_ref[...]).astype(BF16)
    oc = oc_ref[...]
    normed = []
    for h in range(HG_HEADS):
        hl = slice(h * HG_VDIM, (h + 1) * HG_VDIM)
        normed.append(_rms(oc[:, hl], hgnw_ref[:, hl]))
    g = proj(wg_ref[...])
    nc = (jnp.concatenate(normed, axis=-1) * (g * _sigmoid(g))).astype(BF16)
    for j in range(D_MODEL // POST_BLK):
        cols = slice(j * POST_BLK, (j + 1) * POST_BLK)
        dot = lambda a, w_ref, off: jnp.dot(a, w_ref[:, off + j * POST_BLK:off + (j + 1) * POST_BLK],
                                            preferred_element_type=F32)
        out_a = dot(ga, wglu_ref, 0) * _sigmoid(dot(ga, wglu_ref, D_MODEL))
        out_b = dot(nb_, wssd_ref, 0)
        out_c = dot(nc, whg_ref, 0)
        mixed = (_sigmoid(dot(hb, wgt_ref, 0)) * out_a + _sigmoid(dot(hb, wgt_ref, D_MODEL)) * out_b
                 + _sigmoid(dot(hb, wgt_ref, 2 * D_MODEL)) * out_c)
        mix_ref[:, cols] = mixed.astype(BF16)
    x1_ref[...] = x + jnp.dot(mix_ref[...], wo_ref[...], preferred_element_type=F32)


def _post_call(x2d, ya, yb, oc, lp):
    rows = x2d.shape[0]
    tm = BIG_TM
    seq_in = (x2d, ya, yb, oc)
    consts = (lp['n1'], lp['w_z'], lp['w_g'], lp['w_gt'],
              lp['w_glu'], lp['ssd_nw'], lp['w_ssd'], lp['hg_nw'], lp['w_hg'], lp['w_o'])
    row = lambda c: pl.BlockSpec((tm, c), lambda i: (i, 0))
    return pl.pallas_call(
        _post_kernel,
        grid=(rows // tm,),
        in_specs=[row(a.shape[-1]) for a in seq_in] + [_const_spec(c.shape) for c in consts],
        out_specs=row(D_MODEL),
        out_shape=jax.ShapeDtypeStruct((rows, D_MODEL), F32),
        scratch_shapes=[pltpu.VMEM((tm, D_MODEL), BF16)],
        compiler_params=_params(("parallel",)),
        name="post_merge",
    )(*seq_in, *consts)


def _ffn_kernel(final, seq_rows, tm, x_ref, *rest):
    if seq_rows:
        (sp1_ref, sp2_ref, n2_ref, wup_ref, cw_ref, cb_ref, wdn_ref, nf_ref,
         out_ref, aff_ref, hm_ref) = rest
        t = lax.broadcasted_iota(jnp.int32, (tm, 1), 0) % seq_rows
    else:
        (n2_ref, wup_ref, cw_ref, cb_ref, wdn_ref, nf_ref, out_ref, aff_ref, hm_ref, carry_ref) = rest

        @pl.when(pl.program_id(1) == 0)
        def _():
            carry_ref[...] = jnp.zeros_like(carry_ref)
    tail = FFN_CONV - 1
    x = x_ref[...]
    hb = _rms(x, n2_ref[...]).astype(BF16)
    for j in range(D_FF // FFN_BLK):
        cols = slice(j * FFN_BLK, (j + 1) * FFN_BLK)
        a = jnp.dot(hb, wup_ref[:, cols], preferred_element_type=F32)
        g = jnp.dot(hb, wup_ref[:, D_FF + j * FFN_BLK:D_FF + (j + 1) * FFN_BLK],
                    preferred_element_type=F32)
        p1 = pltpu.roll(a, 1, axis=0)
        p2 = pltpu.roll(a, 2, axis=0)
        if seq_rows:
            p1 = jnp.where(t < 1, sp1_ref[:, cols], p1)
            p2 = jnp.where(t < 2, sp2_ref[:, cols], p2)
            aff_ref[:, cols] = a
        else:
            head = jnp.concatenate([carry_ref[:, cols], a[0:SUBLANE, :]], axis=0)
            p1 = jnp.concatenate([pltpu.roll(head, 1, axis=0)[SUBLANE:], p1[SUBLANE:, :]], axis=0)
            p2 = jnp.concatenate([pltpu.roll(head, 2, axis=0)[SUBLANE:], p2[SUBLANE:, :]], axis=0)
            carry_ref[:, cols] = a[tm - SUBLANE:tm, :]
            aff_ref[0, :, cols] = a[tm - tail:tm, :]
        ac = (cw_ref[0:1, cols] * p2 + cw_ref[1:2, cols] * p1 + cw_ref[2:3, cols] * a
              + cb_ref[:, cols])
        hm_ref[:, cols] = (ac * _sigmoid(ac) * g).astype(BF16)
    y = x + jnp.dot(hm_ref[...], wdn_ref[...], preferred_element_type=F32)
    if final:
        y = _rms(y, nf_ref[...])
    out_ref[...] = y


def _ffn_prompt_call(x1, lp, nf, final, nseq, seqlen):
    tm = BIG_TM
    nchunk = seqlen // tm
    consts = (lp['n2'], lp['w_up'], lp['ffn_cw'], lp['ffn_cb'], lp['w_dn'], nf)
    row = pl.BlockSpec((tm, D_MODEL), lambda b, c: (b * nchunk + c, 0))
    return pl.pallas_call(
        functools.partial(_ffn_kernel, final, 0, tm),
        grid=(nseq, nchunk),
        in_specs=[row] + [_const_spec(c.shape) for c in consts],
        out_specs=[row, pl.BlockSpec((1, FFN_CONV - 1, D_FF), lambda b, c: (b, 0, 0))],
        out_shape=[jax.ShapeDtypeStruct(x1.shape, F32),
                   jax.ShapeDtypeStruct((nseq, FFN_CONV - 1, D_FF), F32)],
        scratch_shapes=[pltpu.VMEM((tm, D_FF), BF16), pltpu.VMEM((SUBLANE, D_FF), F32)],
        compiler_params=_params(("parallel", "arbitrary")),
        name="ffn_prompt",
    )(x1, *consts)


def _ffn_sample_call(x1, sp1, sp2, lp, nf, final):
    rows = x1.shape[0]
    tm = SMALL_TM
    consts = (lp['n2'], lp['w_up'], lp['ffn_cw'], lp['ffn_cb'], lp['w_dn'], nf)
    row = lambda c: pl.BlockSpec((tm, c), lambda i: (i, 0))
    return pl.pallas_call(
        functools.partial(_ffn_kernel, final, SAMPLE_PAD, tm),
        grid=(rows // tm,),
        in_specs=[row(D_MODEL), row(D_FF), row(D_FF)] + [_const_spec(c.shape) for c in consts],
        out_specs=[row(D_MODEL), row(D_FF)],
        out_shape=[jax.ShapeDtypeStruct(x1.shape, F32), jax.ShapeDtypeStruct((rows, D_FF), F32)],
        scratch_shapes=[pltpu.VMEM((tm, D_FF), BF16)],
        compiler_params=_params(("arbitrary",)),
        name="ffn_sample",
    )(x1, sp1, sp2, *consts)


def _layer_params(l, lb, norm1_w, w_in, s5_log_dt, s5_lambda_re, s5_lambda_im, s5_b_re, s5_b_im,
                  s5_c_re, s5_c_im, s5_d, s5_w_glu, ssd_conv_w, ssd_conv_b, ssd_dt_bias, ssd_a_log,
                  ssd_d, ssd_norm_w, ssd_w_out, hg_norm_w, hg_w_out, w_o, norm2_w, ffn_w_up,
                  ffn_conv_w, ffn_conv_b, ffn_w_down):
    row = lambda v: v.astype(F32).reshape(1, -1)
    lp = {}
    lp['n1'] = row(norm1_w[l])
    widths = (S5_WIDTH, SSD_INNER, SSD_CONV_DIM, SSD_HEADS, HG_WIDTH, HG_WIDTH, HG_WIDTH, HG_WIDTH,
              3 * D_MODEL)
    names = ('w_ua', 'w_z', 'w_xbc', 'w_dt', 'w_q', 'w_f', 'w_i', 'w_g', 'w_gt')
    off = 0
    for name, w in zip(names, widths):
        lp[name] = w_in[l][:, off:off + w].astype(BF16)
        off += w
    lp['w_dt'] = jnp.pad(lp['w_dt'], ((0, 0), (0, LANE - SSD_HEADS)))
    lp['dtb'] = jnp.pad(row(ssd_dt_bias[l]), ((0, 0), (0, LANE - SSD_HEADS)))
    lp['loglb'] = jnp.log(row(lb))
    lp['log1mlb'] = jnp.log1p(-row(lb))
    lp['omlb'] = 1.0 - row(lb)

    delta = jnp.exp(s5_log_dt[l].astype(F32))[:, None]
    lr = s5_lambda_re[l].astype(F32)
    li = s5_lambda_im[l].astype(F32)
    mag = jnp.exp(lr * delta)
    ab_re = mag * jnp.cos(li * delta)
    ab_im = mag * jnp.sin(li * delta)
    den = lr * lr + li * li
    nr = ab_re - 1.0
    co_re = (nr * lr + ab_im * li) / den
    co_im = (ab_im * lr - nr * li) / den
    br = s5_b_re[l].astype(F32)
    bi = s5_b_im[l].astype(F32)
    bb_re = co_re[..., None] * br - co_im[..., None] * bi
    bb_im = co_re[..., None] * bi + co_im[..., None] * br
    eye = jnp.eye(S5_GROUPS, dtype=F32)
    bd_in = lambda m: jnp.einsum('gnj,gh->gjhn', m, eye).reshape(S5_WIDTH, S5_LANES)
    bd_out = lambda m: jnp.einsum('gjn,gh->gnhj', m, eye).reshape(S5_LANES, S5_WIDTH)
    lp['s5_bb'] = jnp.concatenate([bd_in(bb_re), bd_in(bb_im)], axis=1).astype(BF16)
    lp['s5_cc'] = jnp.concatenate([bd_out(s5_c_re[l].astype(F32)),
                                   -bd_out(s5_c_im[l].astype(F32))], axis=0).astype(BF16)
    pw = jnp.asarray(list(range(1, S5_R + 1)) + [2 * S5_R, 4 * S5_R], F32)[:, None, None]
    pm = jnp.exp(lr * delta * pw)
    ph = li * delta * pw
    pad = ((0, 40 - (S5_R + 2)), (0, 0))
    lp['s5_apr'] = jnp.pad((pm * jnp.cos(ph)).reshape(-1, S5_LANES), pad)
    lp['s5_api'] = jnp.pad((pm * jnp.sin(ph)).reshape(-1, S5_LANES), pad)
    lp['s5_apr8'] = jnp.repeat(lp['s5_apr'][:S5_R + 2], S5_CH, axis=0)
    lp['s5_api8'] = jnp.repeat(lp['s5_api'][:S5_R + 2], S5_CH, axis=0)
    npair = S5_LANES // LANE
    pair_cols = lambda m_re, m_im: jnp.stack(
        [m_re.reshape(-1, npair, LANE), m_im.reshape(-1, npair, LANE)], axis=2).reshape(-1, 2 * S5_LANES)
    bb_int = pair_cols(bd_in(bb_re), bd_in(bb_im)).astype(BF16)
    cc_int = pair_cols(bd_out(s5_c_re[l].astype(F32)).T, -bd_out(s5_c_im[l].astype(F32)).T).T.astype(BF16)
    kw = 2 * LANE
    nlc = 2 * S5_LANES // S5_LC
    ktile = lambda lc: lc * S5_LC // (2 * S5_LANES // (S5_WIDTH // kw))
    lp['s5_bbc'] = jnp.stack([bb_int[ktile(lc) * kw:(ktile(lc) + 1) * kw, lc * S5_LC:(lc + 1) * S5_LC]
                              for lc in range(nlc)])
    lp['s5_ccc'] = jnp.stack([cc_int[lc * S5_LC:(lc + 1) * S5_LC, ktile(lc) * kw:(ktile(lc) + 1) * kw]
                              for lc in range(nlc)])
    lp['s5_d'] = row(s5_d[l])
    lp['w_glu'] = s5_w_glu[l].astype(BF16)

    lp['conv_w'] = ssd_conv_w[l].astype(F32)
    lp['conv_b'] = row(ssd_conv_b[l])
    lp['ssd_arow'] = jnp.pad(-jnp.exp(row(ssd_a_log[l])), ((0, 0), (0, LANE - SSD_HEADS)))
    lp['ssd_dexp'] = jnp.repeat(row(ssd_d[l]), SSD_HEAD_DIM, axis=1)
    lp['ssd_nw'] = row(ssd_norm_w[l])
    lp['w_ssd'] = ssd_w_out[l].astype(BF16)
    lp['hg_nw'] = row(hg_norm_w[l])
    lp['w_hg'] = hg_w_out[l].astype(BF16)
    lp['w_o'] = w_o[l].astype(BF16)
    lp['n2'] = row(norm2_w[l])
    lp['w_up'] = ffn_w_up[l].astype(BF16)
    lp['ffn_cw'] = ffn_conv_w[l].astype(F32)
    lp['ffn_cb'] = row(ffn_conv_b[l])
    lp['w_dn'] = ffn_w_down[l].astype(BF16)
    return lp


_PRE_KEYS = ('ua', 'xc', 'dt', 'q', 'lf', 'kk', 'v')


def kernel(x_prompt, x_sample, state_s5_re, state_s5_im, state_ssd, state_ssd_conv, state_hgrn, state_ffn_conv, norm1_w, w_in, s5_log_dt, s5_lambda_re, s5_lambda_im, s5_b_re, s5_b_im, s5_c_re, s5_c_im, s5_d, s5_w_glu, ssd_conv_w, ssd_conv_b, ssd_dt_bias, ssd_a_log, ssd_d, ssd_norm_w, ssd_w_out, hg_lb_logits, hg_norm_w, hg_w_out, w_o, norm2_w, ffn_w_up, ffn_conv_w, ffn_conv_b, ffn_w_down, norm_f_w):
    nb_p, len_p, _ = x_prompt.shape
    nb_s, len_s, _ = x_sample.shape
    assert len_p % S5_T == 0 and len_p % MIX_T == 0 and len_p % SMALL_TM == 0 and len_p % BIG_TM == 0
    assert len_s <= SAMPLE_PAD and len_s >= SSD_CONV - 1
    assert (nb_s * SAMPLE_PAD) % SMALL_TM == 0 and (nb_s * SAMPLE_PAD) % BIG_TM == 0
    lb_cum = jnp.cumsum(jax.nn.softmax(hg_lb_logits.astype(F32), axis=0), axis=0)
    lb_all = lb_cum - lb_cum[0:1]
    nf = norm_f_w.astype(F32).reshape(1, -1)

    xp = x_prompt.astype(F32).reshape(nb_p * len_p, D_MODEL)
    xs = jnp.pad(x_sample.astype(F32), ((0, 0), (0, SAMPLE_PAD - len_s), (0, 0)))
    xs = xs.reshape(nb_s * SAMPLE_PAD, D_MODEL)
    zeros_p = (jnp.zeros((1, nb_p, SSD_INNER, SSD_STATE), F32),
               jnp.zeros((1, nb_p, HG_WIDTH, HG_VDIM), F32))
    states_s = (state_ssd.astype(F32).reshape(DEPTH, nb_s, SSD_INNER, SSD_STATE),
                state_hgrn.astype(F32).reshape(DEPTH, nb_s, HG_WIDTH, HG_VDIM))
    mix_p = None
    mix_s = None
    new_p = []
    new_s = []
    pad_t = lambda rows: jnp.concatenate(
        rows + [jnp.zeros((nb_s, SAMPLE_PAD - len(rows), rows[0].shape[-1]), F32)], axis=1
    ).reshape(nb_s * SAMPLE_PAD, rows[0].shape[-1])
    for l in range(DEPTH):
        lp = _layer_params(l, lb_all[l], norm1_w, w_in, s5_log_dt, s5_lambda_re, s5_lambda_im,
                           s5_b_re, s5_b_im, s5_c_re, s5_c_im, s5_d, s5_w_glu, ssd_conv_w,
                           ssd_conv_b, ssd_dt_bias, ssd_a_log, ssd_d, ssd_norm_w, ssd_w_out,
                           hg_norm_w, hg_w_out, w_o, norm2_w, ffn_w_up, ffn_conv_w, ffn_conv_b,
                           ffn_w_down)
        final = l == DEPTH - 1

        *pre, p_conv = _pre_prompt_call(xp, lp, nb_p, len_p)
        pre = dict(zip(_PRE_KEYS, pre))
        ya, p_re, p_im = _s5_prompt_call(pre['ua'], lp, nb_p, len_p)
        yb, oc, *mix_p = _mix_call(pre, zeros_p, 0, l, mix_p, lp, nb_p, len_p, MIX_T, MIX_T,
                                   MIX_NB_PROMPT, SSD_Q, HG_Q)
        x1 = _post_call(xp, ya, yb.reshape(-1, SSD_INNER), oc.reshape(-1, HG_WIDTH), lp)
        xp, p_ffn = _ffn_prompt_call(x1, lp, nf, final, nb_p, len_p)
        new_p.append((p_re.reshape(nb_p, S5_GROUPS, S5_STATE), p_im.reshape(nb_p, S5_GROUPS, S5_STATE),
                      p_conv, p_ffn))

        cs = state_ssd_conv[l].astype(F32)
        r = lambda i: cs[:, i:i + 1]
        *pre, raw = _pre_sample_call(xs, pad_t([r(2)]), pad_t([r(1), r(2)]), pad_t([r(0), r(1), r(2)]), lp)
        pre = dict(zip(_PRE_KEYS, pre))
        s_conv = raw.reshape(nb_s, SAMPLE_PAD, SSD_CONV_DIM)[:, len_s - (SSD_CONV - 1):len_s]
        ua_tm = pre['ua'].reshape(nb_s, SAMPLE_PAD, S5_WIDTH)[:, :len_s].transpose(1, 0, 2)
        ya_tm, s_re, s_im = _s5_sample_call(
            ua_tm.reshape(len_s * nb_s, S5_WIDTH),
            state_s5_re[l].astype(F32).reshape(nb_s, S5_LANES),
            state_s5_im[l].astype(F32).reshape(nb_s, S5_LANES), lp, len_s, nb_s)
        ya = jnp.pad(ya_tm.reshape(len_s, nb_s, S5_WIDTH).transpose(1, 0, 2),
                     ((0, 0), (0, SAMPLE_PAD - len_s), (0, 0))).reshape(nb_s * SAMPLE_PAD, S5_WIDTH)
        yb, oc, *mix_s = _mix_call(pre, states_s, l, l, mix_s, lp, nb_s, SAMPLE_PAD, SAMPLE_PAD, len_s,
                                   MIX_NB_SAMPLE, SAMPLE_PAD, SAMPLE_PAD)
        x1 = _post_call(xs, ya, yb.reshape(-1, SSD_INNER), oc.reshape(-1, HG_WIDTH), lp)
        st = state_ffn_conv[l].astype(F32)
        f = lambda i: st[:, i:i + 1]
        xs, aff = _ffn_sample_call(x1, pad_t([f(1)]), pad_t([f(0), f(1)]), lp, nf, final)
        s_ffn = aff.reshape(nb_s, SAMPLE_PAD, D_FF)[:, len_s - (FFN_CONV - 1):len_s]
        new_s.append((s_re.reshape(nb_s, S5_GROUPS, S5_STATE), s_im.reshape(nb_s, S5_GROUPS, S5_STATE),
                      s_conv, s_ffn))

    stk = lambda lst, i: jnp.stack([s[i] for s in lst])
    y_prompt = xp.reshape(nb_p, len_p, D_MODEL)
    y_sample = xs.reshape(nb_s, SAMPLE_PAD, D_MODEL)[:, :len_s]

    def mix_states(m, n):
        ssd, hg = m
        return (ssd.reshape(DEPTH, n, SSD_HEADS, SSD_HEAD_DIM, SSD_STATE),
                hg.reshape(DEPTH, n, HG_HEADS, HG_KDIM, HG_VDIM))

    p_ssd, p_hg = mix_states(mix_p, nb_p)
    s_ssd, s_hg = mix_states(mix_s, nb_s)
    return (y_prompt, y_sample,
            stk(new_p, 0), stk(new_p, 1), p_ssd, stk(new_p, 2), p_hg, stk(new_p, 3),
            stk(new_s, 0), stk(new_s, 1), s_ssd, stk(new_s, 2), s_hg, stk(new_s, 3))
```

```python
import functools
import math

import jax
import jax.numpy as jnp
import numpy as np
from jax import lax
from jax.experimental import pallas as pl
from jax.experimental.pallas import tpu as pltpu

F32 = jnp.float32
BF16 = jnp.bfloat16

D_MODEL = 1024
DEPTH = 2
S5_WIDTH = 512
S5_GROUP = 16
S5_GROUPS = 32
S5_STATE = 64
S5_LANES = S5_GROUPS * S5_STATE
SSD_INNER = 1024
SSD_HEAD_DIM = 64
SSD_HEADS = 16
SSD_GROUPS = 4
SSD_STATE = 128
SSD_CONV = 4
SSD_CONV_DIM = 2048
HG_WIDTH = 512
HG_HEADS = 4
HG_KDIM = 128
HG_VDIM = 128
D_FF = 2816
FFN_CONV = 3
EPS = 1e-6

LANE = 128
SUBLANE = 8
SAMPLE_PAD = 8
HG_SAFE_LOG = -80.0

BIG_TM = 512
FFN_TM = 1024
SMALL_TM = 256
PRE_BLK = 256
POST_BLK = 256
FFN_BLK = 256
S5_T = 256
S5_CH = 8
S5_R = S5_T // S5_CH
S5_LC = 512
MIX_T = 128
SSD_Q = 128
HG_Q = 64
MIX_STAGGER = 3
MIX_STAGGER_MOD = 7
ROW_CHAINS = 2
ROW_STAGGER = 5
FFN_STAGGER = 11
MIX_NB_PROMPT = 2
MIX_NB_SAMPLE = 8
VMEM_LIMIT = 56 * 1024 * 1024


def _const_spec(shape):
    nd = len(shape)
    return pl.BlockSpec(shape, lambda *_: (0,) * nd, pipeline_mode=pl.Buffered(1))


class _Layered:
    def __init__(self, stacked, layer):
        self.stacked, self.layer = stacked, layer


def _wspec(c):
    if isinstance(c, _Layered):
        shape, layer = c.stacked.shape, c.layer
        return pl.BlockSpec((None,) + shape[1:], lambda *_: (layer,) + (0,) * (len(shape) - 1),
                            pipeline_mode=pl.Buffered(1))
    return _const_spec(c.shape)


def _warg(c):
    return c.stacked if isinstance(c, _Layered) else c


def _params(sem):
    return pltpu.CompilerParams(dimension_semantics=sem, vmem_limit_bytes=VMEM_LIMIT)


def _bdot(a, b):
    return jnp.dot(a.astype(BF16), b.astype(BF16), preferred_element_type=F32)


def _bdot_nt(a, b):
    return lax.dot_general(a.astype(BF16), b.astype(BF16), (((1,), (1,)), ((), ())),
                           preferred_element_type=F32)


def _bdot_tn(a, b):
    return lax.dot_general(a.astype(BF16), b.astype(BF16), (((0,), (0,)), ((), ())),
                           preferred_element_type=F32)


def _split3(x):
    h = x.astype(BF16)
    r = x - h.astype(F32)
    m = r.astype(BF16)
    l = (r - m.astype(F32)).astype(BF16)
    return h, m, l


def _split_cat(x, n):
    return jnp.concatenate(_split3(x)[:n], axis=1)


def _sel_left(m01, x):
    h, m, l = _split3(x)
    d = lambda p: jnp.dot(m01, p, preferred_element_type=F32)
    return (d(h) + d(m)) + d(l)


def _sel_right(x, m01):
    h, m, l = _split3(x)
    d = lambda p: jnp.dot(p, m01, preferred_element_type=F32)
    return (d(h) + d(m)) + d(l)


def _interleave(chains, stagger, modulus=None):
    chains = list(chains)
    lead = [stagger * i if modulus is None else (stagger * i) % modulus for i in range(len(chains))]
    for n, c in zip(lead, reversed(chains)):
        for _ in range(n):
            next(c, None)
    while chains:
        for c in list(chains):
            try:
                next(c)
            except StopIteration:
                chains.remove(c)


def _sigmoid(x):
    return 1.0 / (1.0 + jnp.exp(-x))


def _softplus(x):
    return jnp.maximum(x, 0.0) + jnp.log1p(jnp.exp(-jnp.abs(x)))


def _rms(x, w):
    return x * lax.rsqrt(jnp.mean(x * x, axis=-1, keepdims=True) + EPS) * w


def _pre_kernel(seq_rows, tm, x_ref, *rest):
    if seq_rows:
        (s1_ref, s2_ref, s3_ref, n1_ref, w_ua, w_xbc, w_dt, w_q, w_f, w_i, cw_ref, cb_ref,
         dtb_ref, loglb_ref, log1mlb_ref, omlb_ref,
         ua_o, xc_o, dt_o, q_o, lf_o, kk_o, v_o, raw_o) = rest
        t = lax.broadcasted_iota(jnp.int32, (tm, 1), 0) % seq_rows
    else:
        (n1_ref, w_ua, w_xbc, w_dt, w_q, w_f, w_i, cw_ref, cb_ref,
         dtb_ref, loglb_ref, log1mlb_ref, omlb_ref,
         ua_o, xc_o, dt_o, q_o, lf_o, kk_o, v_o, cst_o, carry_ref) = rest

        @pl.when(pl.program_id(1) == 0)
        def _():
            carry_ref[...] = jnp.zeros_like(carry_ref)
    tail = SSD_CONV - 1
    hr = tm // ROW_CHAINS

    def chain(c):
        rows = slice(c * hr, (c + 1) * hr)
        hb = _rms(x_ref[rows, :], n1_ref[...]).astype(BF16)
        dot = lambda w: jnp.dot(hb, w, preferred_element_type=F32)
        ua_o[rows, :] = dot(w_ua[...])
        yield
        for j in range(SSD_CONV_DIM // PRE_BLK):
            cols = slice(j * PRE_BLK, (j + 1) * PRE_BLK)
            raw = dot(w_xbc[:, cols])
            if seq_rows:
                acc = cw_ref[tail:tail + 1, cols] * raw + cb_ref[:, cols]
                for k, s_ref in ((1, s1_ref), (2, s2_ref), (3, s3_ref)):
                    prev = jnp.where(t[rows, :] < k, s_ref[rows, cols], pltpu.roll(raw, k, axis=0))
                    acc = acc + cw_ref[tail - k:tail - k + 1, cols] * prev
                raw_o[rows, cols] = raw
            else:
                xx = jnp.concatenate([carry_ref[:, cols], raw], axis=0)
                acc = cw_ref[0:1, cols] * xx
                for k in range(1, SSD_CONV):
                    acc = pltpu.roll(acc, 1, axis=0) + cw_ref[k:k + 1, cols] * xx
                acc = acc[SUBLANE:, :] + cb_ref[:, cols]
                carry_ref[:, cols] = raw[hr - SUBLANE:hr, :]
                if c == ROW_CHAINS - 1:
                    cst_o[0, :, cols] = raw[hr - tail:hr, :]
            xc_o[rows, cols] = acc * _sigmoid(acc)
            yield
        dt_o[rows, :] = _softplus(dot(w_dt[...]) + dtb_ref[...])
        q_o[rows, :] = dot(w_q[...])
        yield
        zf = dot(w_f[...])
        log_sig = jnp.minimum(zf, 0.0) - jnp.log1p(jnp.exp(-jnp.abs(zf)))
        a = loglb_ref[...]
        b = log1mlb_ref[...] + log_sig
        lf_o[rows, :] = jnp.maximum(a, b) + jnp.log1p(jnp.exp(-jnp.abs(a - b)))
        kk_o[rows, :] = omlb_ref[...] * _sigmoid(-zf)
        yield
        v_o[rows, :] = dot(w_i[...])
        yield

    _interleave([chain(c) for c in range(ROW_CHAINS)], ROW_STAGGER)


_PRE_WIDTHS = (S5_WIDTH, SSD_CONV_DIM, LANE, HG_WIDTH, HG_WIDTH, HG_WIDTH, HG_WIDTH)


def _pre_consts(lp):
    return (lp['n1'], lp['w_ua'], lp['w_xbc'], lp['w_dt'], lp['w_q'], lp['w_f'], lp['w_i'],
            lp['conv_w'], lp['conv_b'], lp['dtb'], lp['loglb'], lp['log1mlb'], lp['omlb'])


def _pre_prompt_call(x2d, lp, nseq, seqlen):
    tm = BIG_TM
    nchunk = seqlen // tm
    consts = _pre_consts(lp)
    row = lambda c: pl.BlockSpec((tm, c), lambda b, k: (b * nchunk + k, 0))
    tail = SSD_CONV - 1
    return pl.pallas_call(
        functools.partial(_pre_kernel, 0, tm),
        grid=(nseq, nchunk),
        in_specs=[row(D_MODEL)] + [_const_spec(c.shape) for c in consts],
        out_specs=[row(c) for c in _PRE_WIDTHS]
        + [pl.BlockSpec((1, tail, SSD_CONV_DIM), lambda b, k: (b, 0, 0))],
        out_shape=[jax.ShapeDtypeStruct((x2d.shape[0], c), F32) for c in _PRE_WIDTHS]
        + [jax.ShapeDtypeStruct((nseq, tail, SSD_CONV_DIM), F32)],
        scratch_shapes=[pltpu.VMEM((SUBLANE, SSD_CONV_DIM), F32)],
        compiler_params=_params(("parallel", "arbitrary")),
        name="pre_proj_prompt",
    )(x2d, *consts)


def _pre_sample_call(x2d, s1, s2, s3, lp):
    rows = x2d.shape[0]
    tm = SMALL_TM
    consts = _pre_consts(lp)
    row = lambda c: pl.BlockSpec((tm, c), lambda i: (i, 0))
    return pl.pallas_call(
        functools.partial(_pre_kernel, SAMPLE_PAD, tm),
        grid=(rows // tm,),
        in_specs=[row(D_MODEL)] + [row(SSD_CONV_DIM)] * 3 + [_const_spec(c.shape) for c in consts],
        out_specs=[row(c) for c in _PRE_WIDTHS] + [row(SSD_CONV_DIM)],
        out_shape=[jax.ShapeDtypeStruct((rows, c), F32) for c in _PRE_WIDTHS]
        + [jax.ShapeDtypeStruct((rows, SSD_CONV_DIM), F32)],
        compiler_params=_params(("parallel",)),
        name="pre_proj_sample",
    )(x2d, s1, s2, s3, *consts)


def _cmul_add(ar, ai, xr, xi, br, bi):
    return ar * xr - ai * xi + br, ar * xi + ai * xr + bi


def _s5_prompt_kernel(ua_ref, perm_ref, permt_ref, bbc_ref, ccc_ref, apr_ref, api_ref, d_ref,
                      ya_ref, st_ref, bu_ref):
    @pl.when(pl.program_id(1) == 0)
    def _():
        st_ref[...] = jnp.zeros_like(st_ref)

    ua = ua_ref[...]
    up = jnp.dot(perm_ref[...], ua.astype(BF16), preferred_element_type=F32).astype(BF16)
    rowid = lax.broadcasted_iota(jnp.int32, (S5_CH, LANE), 0)
    tab = lambda ref, k, gp: ref[k * S5_CH:(k + 1) * S5_CH, gp * LANE:(gp + 1) * LANE]
    kw = 2 * LANE
    y_acc = [None] * (S5_WIDTH // kw)

    for lc in range(2 * S5_LANES // S5_LC):
        kt = lc * S5_LC // (2 * S5_LANES // (S5_WIDTH // kw))
        cols = slice(lc * S5_LC, (lc + 1) * S5_LC)
        bu_ref[:, cols] = jnp.dot(up[:, kt * kw:(kt + 1) * kw], bbc_ref[lc], preferred_element_type=F32)
        for gp in range(lc * S5_LC // kw, (lc + 1) * S5_LC // kw):
            re = slice(gp * kw, gp * kw + LANE)
            im = slice(gp * kw + LANE, (gp + 1) * kw)
            a_r, a_i = tab(apr_ref, 0, gp), tab(api_ref, 0, gp)
            sr = jnp.zeros((S5_CH, LANE), F32)
            si = jnp.zeros((S5_CH, LANE), F32)
            for r in range(S5_R):
                rows = slice(r * S5_CH, (r + 1) * S5_CH)
                sr, si = _cmul_add(a_r, a_i, sr, si, bu_ref[rows, re], bu_ref[rows, im])
                bu_ref[rows, re] = sr
                bu_ref[rows, im] = si
            pr = jnp.broadcast_to(st_ref[0, :, re], (S5_CH, LANE))
            pi = jnp.broadcast_to(st_ref[0, :, im], (S5_CH, LANE))
            vr = jnp.where(rowid == 0, pr, pltpu.roll(sr, 1, axis=0))
            vi = jnp.where(rowid == 0, pi, pltpu.roll(si, 1, axis=0))
            for d, k in ((1, S5_R - 1), (2, S5_R), (4, S5_R + 1)):
                keep = rowid >= d
                tr = jnp.where(keep, pltpu.roll(vr, d, axis=0), 0.0)
                ti = jnp.where(keep, pltpu.roll(vi, d, axis=0), 0.0)
                vr, vi = _cmul_add(tab(apr_ref, k, gp), tab(api_ref, k, gp), tr, ti, vr, vi)
            for r in range(S5_R):
                rows = slice(r * S5_CH, (r + 1) * S5_CH)
                xr, xi = _cmul_add(tab(apr_ref, r, gp), tab(api_ref, r, gp), vr, vi,
                                   bu_ref[rows, re], bu_ref[rows, im])
                bu_ref[rows, re] = xr
                bu_ref[rows, im] = xi
            st_ref[0, :, re] = xr[S5_CH - 1:S5_CH, :]
            st_ref[0, :, im] = xi[S5_CH - 1:S5_CH, :]
        part = _bdot(bu_ref[:, cols], ccc_ref[lc])
        y_acc[kt] = part if y_acc[kt] is None else y_acc[kt] + part

    yp = jnp.concatenate(y_acc, axis=1)
    ya_ref[...] = _sel_left(permt_ref[...], yp) + d_ref[...] * ua


def _s5_perm():
    p = np.zeros((S5_T, S5_T), np.float32)
    for i in range(S5_T):
        p[i, (i % S5_CH) * S5_R + i // S5_CH] = 1.0
    return jnp.asarray(p, BF16), jnp.asarray(p.T, BF16)


def _s5_prompt_call(ua, lp, nseq, seqlen):
    nchunk = seqlen // S5_T
    perm, permt = _s5_perm()
    consts = (perm, permt, lp['s5_bbc'], lp['s5_ccc'], lp['s5_apr8'], lp['s5_api8'], lp['s5_d'])
    st_spec = pl.BlockSpec((1, 1, 2 * S5_LANES), lambda b, c: (b, 0, 0))
    st_shape = jax.ShapeDtypeStruct((nseq, 1, 2 * S5_LANES), F32)
    ya, st = pl.pallas_call(
        _s5_prompt_kernel,
        grid=(nseq, nchunk),
        in_specs=[pl.BlockSpec((S5_T, S5_WIDTH), lambda b, c: (b * nchunk + c, 0))]
        + [_const_spec(c.shape) for c in consts],
        out_specs=[pl.BlockSpec((S5_T, S5_WIDTH), lambda b, c: (b * nchunk + c, 0)), st_spec],
        out_shape=[jax.ShapeDtypeStruct(ua.shape, F32), st_shape],
        scratch_shapes=[pltpu.VMEM((S5_T, 2 * S5_LANES), F32)],
        compiler_params=_params(("parallel", "arbitrary")),
        name="s5_prompt",
    )(ua, *consts)
    st = st.reshape(nseq, S5_LANES // LANE, 2, LANE)
    return ya, st[:, :, 0].reshape(nseq, S5_LANES), st[:, :, 1].reshape(nseq, S5_LANES)


def _s5_sample_kernel(nstep, nseq, ua_ref, sre0_ref, sim0_ref, bb_ref, cc_ref, apr_ref, api_ref,
                      d_ref, ya_ref, sre_ref, sim_ref, bu_ref, xb_ref):
    ua = ua_ref[...]
    bu_ref[...] = jnp.dot(ua.astype(BF16), bb_ref[...], preferred_element_type=F32)
    for lc in range(S5_LANES // S5_LC):
        re = slice(lc * S5_LC, (lc + 1) * S5_LC)
        im = slice(S5_LANES + lc * S5_LC, S5_LANES + (lc + 1) * S5_LC)
        a_r = apr_ref[0:1, re]
        a_i = api_ref[0:1, re]
        sr = sre0_ref[:, re]
        si = sim0_ref[:, re]
        for t in range(nstep):
            rows = slice(t * nseq, (t + 1) * nseq)
            sr, si = _cmul_add(a_r, a_i, sr, si, bu_ref[rows, re], bu_ref[rows, im])
            xb_ref[rows, re] = sr.astype(BF16)
            xb_ref[rows, im] = si.astype(BF16)
        sre_ref[:, re] = sr
        sim_ref[:, re] = si
    ya_ref[...] = jnp.dot(xb_ref[...], cc_ref[...], preferred_element_type=F32) + d_ref[...] * ua


def _s5_sample_call(ua_tm, sre0, sim0, lp, nstep, nseq):
    rows = nstep * nseq
    args = (ua_tm, sre0, sim0, lp['s5_bb'], lp['s5_cc'], lp['s5_apr'], lp['s5_api'], lp['s5_d'])
    st_shape = jax.ShapeDtypeStruct((nseq, S5_LANES), F32)
    full = lambda s: pl.BlockSpec(s, lambda i: (0,) * len(s))
    return pl.pallas_call(
        functools.partial(_s5_sample_kernel, nstep, nseq),
        grid=(1,),
        in_specs=[full(a.shape) for a in args],
        out_specs=[full((rows, S5_WIDTH)), full((nseq, S5_LANES)), full((nseq, S5_LANES))],
        out_shape=[jax.ShapeDtypeStruct((rows, S5_WIDTH), F32), st_shape, st_shape],
        scratch_shapes=[pltpu.VMEM((rows, 2 * S5_LANES), F32), pltpu.VMEM((rows, 2 * S5_LANES), BF16)],
        compiler_params=_params(("arbitrary",)),
        name="s5_sample",
    )(*args)


def _mix_kernel(t_rows, n_valid, nb, ssd_q, hg_q, n_alias,
                xc_ref, dt_ref, q_ref, lf_ref, kk_ref, v_ref, ssd0_ref, hg0_ref,
                arow_ref, dexp_ref, ltri_s_ref, ltri_h_ref, colsel_ref, headsel_ref, *rest):
    (yb_ref, oc_ref, ssd_ref, hg_ref, hb_ref, hk_ref, ho_ref, hq_ref, hv_ref,
     hqe_ref, hkd_ref, hke_ref, ebt_ref, cb_ref, gm_ref, ce_ref, xw_ref, acb_ref, eab_ref,
     ac_ref, dtm_ref) = rest[n_alias:]
    assert t_rows == ssd_q

    @pl.when(pl.program_id(1) == 0)
    def _():
        ssd_ref[...] = ssd0_ref[...]
        hg_ref[...] = hg0_ref[...]

    n_rows = nb * t_rows
    if n_valid < t_rows:
        valid_all = (lax.broadcasted_iota(jnp.int32, (n_rows, 1), 0) % t_rows < n_valid).astype(F32)
    else:
        valid_all = None
    sq = lambda m: m if m.shape[0] == LANE else jnp.concatenate(
        [m, jnp.zeros((LANE - m.shape[0], LANE), F32)], axis=0)

    def ssd_prepare():
        dt = dt_ref[...].reshape(n_rows, LANE)
        if valid_all is not None:
            dt = dt * valid_all
        acum = _sel_left(ltri_s_ref[...], dt * arow_ref[...])
        alast = jnp.concatenate(
            [jnp.broadcast_to(acum[(b + 1) * t_rows - 1:(b + 1) * t_rows, :], (t_rows, LANE))
             for b in range(nb)], axis=0)
        eacm = jnp.exp(acum)
        wcol = dt * jnp.exp(alast - acum)
        ac_ref[...] = acum
        dtm_ref[...] = dt
        acb_ref[...] = jnp.dot(_split_cat(acum, 3), colsel_ref[...], preferred_element_type=F32)
        eab_ref[...] = jnp.dot(_split_cat(eacm, 2), colsel_ref[0:2 * LANE, :], preferred_element_type=F32)
        wx = jnp.dot(_split_cat(wcol, 2), headsel_ref[...], preferred_element_type=F32)
        xw_ref[...] = (xc_ref[:, :, 0:SSD_INNER].reshape(n_rows, SSD_INNER) * wx).astype(xw_ref.dtype)

    def ssd_chunk(j, r0):
        q_ = ssd_q
        rows = slice(r0, r0 + q_)
        brows = slice(j * t_rows + r0, j * t_rows + r0 + q_)
        acum_t = sq(ac_ref[brows, :]).T
        dt_t = sq(dtm_ref[brows, :]).T
        ela_t = jnp.exp(jnp.broadcast_to(acum_t[:, q_ - 1:q_], (LANE, LANE)))
        tri = lax.broadcasted_iota(jnp.int32, (q_, q_), 0) >= lax.broadcasted_iota(jnp.int32, (q_, q_), 1)
        hpg = SSD_HEADS // SSD_GROUPS
        gw = hpg * SSD_HEAD_DIM
        b0 = SSD_INNER
        c0 = SSD_INNER + SSD_GROUPS * SSD_STATE
        cgs = lambda g: xc_ref[j, rows, c0 + g * SSD_STATE:c0 + (g + 1) * SSD_STATE]
        bgs = lambda g: xc_ref[j, rows, b0 + g * SSD_STATE:b0 + (g + 1) * SSD_STATE]
        for g in range(SSD_GROUPS):
            cb_ref[j, g] = _bdot_nt(cgs(g), bgs(g))
        yield
        for g in range(SSD_GROUPS):
            cg = cgs(g)
            for r in range(hpg):
                h = g * hpg + r
                seg = acb_ref[brows, h * LANE:h * LANE + q_] - acum_t[h:h + 1, :q_]
                dec = jnp.exp(jnp.where(tri, seg, -jnp.inf))
                gm_ref[j, h] = (cb_ref[j, g] * dec * dt_t[h:h + 1, :q_]).astype(gm_ref.dtype)
                ce_ref[j, h] = (cg * eab_ref[brows, h * LANE:(h + 1) * LANE]).astype(ce_ref.dtype)
            yield
        for g in range(SSD_GROUPS):
            bgb = bgs(g).astype(BF16)
            sg = ssd_ref[j, g * gw:(g + 1) * gw, :]
            sgb = sg.astype(BF16)
            scale = []
            for r in range(hpg):
                h = g * hpg + r
                hl = slice(h * SSD_HEAD_DIM, (h + 1) * SSD_HEAD_DIM)
                xs_h = xc_ref[j, rows, hl]
                yb_ref[j, rows, hl] = (_bdot(gm_ref[j, h], xs_h)
                                       + _bdot_nt(ce_ref[j, h], sgb[r * SSD_HEAD_DIM:(r + 1) * SSD_HEAD_DIM, :])
                                       + dexp_ref[:, hl] * xs_h)
                scale.append(jnp.broadcast_to(ela_t[h:h + 1, :], (SSD_HEAD_DIM, SSD_STATE)))
            upd = _bdot_tn(xw_ref[brows, g * gw:(g + 1) * gw], bgb)
            ssd_ref[j, g * gw:(g + 1) * gw, :] = sg * jnp.concatenate(scale, axis=0) + upd
            yield

    def hg_prepare():
        flat = lambda ref: ref[...].reshape(n_rows, HG_WIDTH)
        lf = flat(lf_ref)
        kk = flat(kk_ref)
        if valid_all is not None:
            lf = lf * valid_all
            kk = kk * valid_all
        bc = _sel_left(ltri_h_ref[...], lf)
        hb_ref[...] = bc
        hk_ref[...] = kk
        safe = jnp.min(bc) >= HG_SAFE_LOG

        @pl.when(safe)
        def _():
            ho_ref[...] = jnp.zeros((n_rows, HG_WIDTH), F32)

        @pl.when(jnp.logical_not(safe))
        def _():
            ho_ref[...] = jnp.zeros((n_rows, HG_WIDTH), F32)
            hq_ref[...] = flat(q_ref)
            hv_ref[...] = flat(v_ref)
            rid = lax.broadcasted_iota(jnp.int32, (n_rows, 1), 0)

            def body(s, c):
                brow = hb_ref[pl.ds(s, 1), :]
                krow = hk_ref[pl.ds(s, 1), :]
                vrow = hv_ref[pl.ds(s, 1), :]
                w = hq_ref[...] * krow * jnp.exp(jnp.minimum(hb_ref[...] - brow, 0.0))
                same_chunk_end = (s // hg_q + 1) * hg_q
                w = jnp.where(jnp.logical_and(rid >= s, rid < same_chunk_end), w, 0.0)
                for h in range(HG_HEADS):
                    hl = slice(h * HG_KDIM, (h + 1) * HG_KDIM)
                    a = jnp.sum(w[:, hl], axis=-1, keepdims=True)
                    ho_ref[:, hl] = ho_ref[:, hl] + a * vrow[:, hl]
                return c

            lax.fori_loop(0, n_rows, body, 0)

        return safe

    def hg_chunk(j, r0, safe):
        q_ = hg_q
        rows = slice(r0, r0 + q_)
        brows = slice(j * t_rows + r0, j * t_rows + r0 + q_)
        qq = q_ref[j, rows, :]
        vv = v_ref[j, rows, :]
        bc = hb_ref[brows, :]
        kk = hk_ref[brows, :]
        blast = bc[q_ - 1:q_, :]
        mm = hqe_ref.dtype
        hqe_ref[brows, :] = (qq * jnp.exp(bc)).astype(mm)
        hkd_ref[brows, :] = (kk * jnp.exp(blast - bc)).astype(mm)
        hke_ref[brows, :] = (kk * jnp.exp(-bc)).astype(mm)
        eb = jnp.exp(blast)
        for h in range(HG_HEADS):
            hl = slice(h * HG_KDIM, (h + 1) * HG_KDIM)
            ebt_ref[j, h] = jnp.broadcast_to(eb[:, hl], (HG_KDIM, HG_KDIM)).T
        yield
        tri = lax.broadcasted_iota(jnp.int32, (q_, q_), 0) >= lax.broadcasted_iota(jnp.int32, (q_, q_), 1)
        for h in range(HG_HEADS):
            hl = slice(h * HG_KDIM, (h + 1) * HG_KDIM)
            qe = hqe_ref[brows, hl]
            att = jnp.where(tri, _bdot_nt(qe, hke_ref[brows, hl]), 0.0)
            intra = jnp.where(safe, _bdot(att, vv[:, hl]), ho_ref[brows, hl])
            sh = hg_ref[j, hl, :]
            oc_ref[j, rows, hl] = intra + _bdot(qe, sh)
            hg_ref[j, hl, :] = sh * ebt_ref[j, h] + _bdot_tn(hkd_ref[brows, hl], vv[:, hl])
            yield

    def ssd_seq(j):
        for c in range(t_rows // ssd_q):
            yield from ssd_chunk(j, c * ssd_q)

    def hg_seq(j, safe):
        for c in range(t_rows // hg_q):
            yield from hg_chunk(j, c * hg_q, safe)

    safe = hg_prepare()
    ssd_prepare()
    _interleave([t for j in range(nb) for t in (ssd_seq(j), hg_seq(j, safe))],
                MIX_STAGGER, MIX_STAGGER_MOD)


def _ltri(n, q):
    return jnp.asarray(np.kron(np.eye(n // q, dtype=np.float32), np.tril(np.ones((q, q), np.float32))), BF16)


def _head_selectors():
    col = np.zeros((3, LANE, SSD_HEADS * LANE), np.float32)
    head = np.zeros((2, LANE, SSD_INNER), np.float32)
    for h in range(SSD_HEADS):
        col[:, h, h * LANE:(h + 1) * LANE] = 1.0
        head[:, h, h * SSD_HEAD_DIM:(h + 1) * SSD_HEAD_DIM] = 1.0
    return (jnp.asarray(col.reshape(3 * LANE, -1), BF16), jnp.asarray(head.reshape(2 * LANE, -1), BF16))


def _mix_call(pre, states, l_in, l_out, prev, lp, nseq, seqlen, t_rows, n_valid, nb, ssd_q, hg_q):
    nchunk = seqlen // t_rows
    view = lambda a: a.reshape(nseq, seqlen, a.shape[-1])
    seq_in = [view(pre[k]) for k in ('xc', 'dt', 'q', 'lf', 'kk', 'v')]
    consts = (lp['ssd_arow'], lp['ssd_dexp'], _ltri(nb * t_rows, ssd_q), _ltri(nb * t_rows, hg_q),
              *_head_selectors())
    tile = lambda c: pl.BlockSpec((nb, t_rows, c), lambda i, k: (i, k, 0))
    st = lambda a, l: pl.BlockSpec((None, nb) + a.shape[2:], lambda i, k: (l, i, 0, 0))
    prev = () if prev is None else tuple(prev)
    n_in = len(seq_in) + len(states) + len(consts)
    mm = BF16 if ssd_q % (2 * SUBLANE) == 0 and hg_q % (2 * SUBLANE) == 0 else F32
    kern = functools.partial(_mix_kernel, t_rows, n_valid, nb, ssd_q, hg_q, len(prev))
    return pl.pallas_call(
        kern,
        grid=(nseq // nb, nchunk),
        in_specs=[tile(a.shape[-1]) for a in seq_in] + [st(a, l_in) for a in states]
        + [_const_spec(c.shape) for c in consts]
        + [pl.BlockSpec(memory_space=pl.ANY) for _ in prev],
        out_specs=[tile(SSD_INNER), tile(HG_WIDTH)] + [st(a, l_out) for a in states],
        out_shape=[jax.ShapeDtypeStruct((nseq, seqlen, SSD_INNER), F32),
                   jax.ShapeDtypeStruct((nseq, seqlen, HG_WIDTH), F32)]
        + [jax.ShapeDtypeStruct((DEPTH,) + a.shape[1:], F32) for a in states],
        input_output_aliases={n_in + k: 2 + k for k in range(len(prev))},
        scratch_shapes=[pltpu.VMEM((nb * t_rows, HG_WIDTH), F32) for _ in range(5)]
        + [pltpu.VMEM((nb * t_rows, HG_WIDTH), mm) for _ in range(3)]
        + [pltpu.VMEM((nb, HG_HEADS, HG_KDIM, HG_KDIM), F32),
           pltpu.VMEM((nb, SSD_GROUPS, ssd_q, ssd_q), F32),
           pltpu.VMEM((nb, SSD_HEADS, ssd_q, ssd_q), mm),
           pltpu.VMEM((nb, SSD_HEADS, ssd_q, SSD_STATE), mm),
           pltpu.VMEM((nb * t_rows, SSD_INNER), mm),
           pltpu.VMEM((nb * t_rows, SSD_HEADS * LANE), F32),
           pltpu.VMEM((nb * t_rows, SSD_HEADS * LANE), F32),
           pltpu.VMEM((nb * t_rows, LANE), F32),
           pltpu.VMEM((nb * t_rows, LANE), F32)],
        compiler_params=_params(("parallel", "arbitrary")),
        name="ssd_hgrn_mix",
    )(*seq_in, *states, *consts, *prev)


def _post_kernel(x_ref, ya_ref, yb_ref, oc_ref, n1_ref, wz_ref, wg_ref, wgt_ref,
                 wglu_ref, ssdnw_ref, wssd_ref, hgnw_ref, whg_ref, wo_ref, x1_ref, mix_ref):
    x = x_ref[...]
    hb = _rms(x, n1_ref[...]).astype(BF16)
    proj = lambda w: jnp.dot(hb, w, preferred_element_type=F32)
    ya = ya_ref[...]
    cdf = 0.5 * (1.0 + jnp.tanh(math.sqrt(2.0 / math.pi) * (ya + 0.044715 * (ya * ya * ya))))
    ga = (ya * cdf).astype(BF16)
    z = proj(wz_ref[...])
    nb_ = _rms(yb_ref[...] * (z * _sigmoid(z)), ssdnw_ref[...]).astype(BF16)
    oc = oc_ref[...]
    normed = []
    for h in range(HG_HEADS):
        hl = slice(h * HG_VDIM, (h + 1) * HG_VDIM)
        normed.append(_rms(oc[:, hl], hgnw_ref[:, hl]))
    g = proj(wg_ref[...])
    nc = (jnp.concatenate(normed, axis=-1) * (g * _sigmoid(g))).astype(BF16)
    for j in range(D_MODEL // POST_BLK):
        cols = slice(j * POST_BLK, (j + 1) * POST_BLK)
        dot = lambda a, w_ref, off: jnp.dot(a, w_ref[:, off + j * POST_BLK:off + (j + 1) * POST_BLK],
                                            preferred_element_type=F32)
        out_a = dot(ga, wglu_ref, 0) * _sigmoid(dot(ga, wglu_ref, D_MODEL))
        out_b = dot(nb_, wssd_ref, 0)
        out_c = dot(nc, whg_ref, 0)
        mixed = (_sigmoid(dot(hb, wgt_ref, 0)) * out_a + _sigmoid(dot(hb, wgt_ref, D_MODEL)) * out_b
                 + _sigmoid(dot(hb, wgt_ref, 2 * D_MODEL)) * out_c)
        mix_ref[:, cols] = mixed.astype(BF16)
    x1_ref[...] = x + jnp.dot(mix_ref[...], wo_ref[...], preferred_element_type=F32)


def _post_call(x2d, ya, yb, oc, lp):
    rows = x2d.shape[0]
    tm = BIG_TM
    seq_in = (x2d, ya, yb, oc)
    consts = (lp['n1'], lp['w_z'], lp['w_g'], lp['w_gt'],
              lp['w_glu'], lp['ssd_nw'], lp['w_ssd'], lp['hg_nw'], lp['w_hg'], lp['w_o'])
    row = lambda c: pl.BlockSpec((tm, c), lambda i: (i, 0))
    return pl.pallas_call(
        _post_kernel,
        grid=(rows // tm,),
        in_specs=[row(a.shape[-1]) for a in seq_in] + [_wspec(c) for c in consts],
        out_specs=row(D_MODEL),
        out_shape=jax.ShapeDtypeStruct((rows, D_MODEL), F32),
        scratch_shapes=[pltpu.VMEM((tm, D_MODEL), BF16)],
        compiler_params=_params(("parallel",)),
        name="post_merge",
    )(*seq_in, *[_warg(c) for c in consts])


def _ffn_kernel(final, seq_rows, tm, x_ref, *rest):
    if seq_rows:
        (sp1_ref, sp2_ref, n2_ref, wup_ref, cw_ref, cb_ref, wdn_ref, nf_ref,
         out_ref, aff_ref, hm_ref) = rest
        t = lax.broadcasted_iota(jnp.int32, (tm, 1), 0) % seq_rows
    else:
        (n2_ref, wup_ref, cw_ref, cb_ref, wdn_ref, nf_ref, out_ref, aff_ref, hm_ref, carry_ref) = rest

        @pl.when(pl.program_id(1) == 0)
        def _():
            carry_ref[...] = jnp.zeros_like(carry_ref)
    tail = FFN_CONV - 1
    hr = tm // ROW_CHAINS

    def chain(c):
        rows = slice(c * hr, (c + 1) * hr)
        x = x_ref[rows, :]
        hb = _rms(x, n2_ref[...]).astype(BF16)
        for j in range(D_FF // FFN_BLK):
            cols = slice(j * FFN_BLK, (j + 1) * FFN_BLK)
            a = jnp.dot(hb, wup_ref[:, cols], preferred_element_type=F32)
            g = jnp.dot(hb, wup_ref[:, D_FF + j * FFN_BLK:D_FF + (j + 1) * FFN_BLK],
                        preferred_element_type=F32)
            p1 = pltpu.roll(a, 1, axis=0)
            p2 = pltpu.roll(a, 2, axis=0)
            if seq_rows:
                p1 = jnp.where(t[rows, :] < 1, sp1_ref[rows, cols], p1)
                p2 = jnp.where(t[rows, :] < 2, sp2_ref[rows, cols], p2)
                aff_ref[rows, cols] = a
            else:
                head = jnp.concatenate([carry_ref[:, cols], a[0:SUBLANE, :]], axis=0)
                p1 = jnp.concatenate([pltpu.roll(head, 1, axis=0)[SUBLANE:], p1[SUBLANE:, :]], axis=0)
                p2 = jnp.concatenate([pltpu.roll(head, 2, axis=0)[SUBLANE:], p2[SUBLANE:, :]], axis=0)
                carry_ref[:, cols] = a[hr - SUBLANE:hr, :]
                if c == ROW_CHAINS - 1:
                    aff_ref[0, :, cols] = a[hr - tail:hr, :]
            ac = (cw_ref[0:1, cols] * p2 + cw_ref[1:2, cols] * p1 + cw_ref[2:3, cols] * a
                  + cb_ref[:, cols])
            hm_ref[rows, cols] = (ac * _sigmoid(ac) * g).astype(BF16)
            yield
        for j in range(D_MODEL // FFN_BLK):
            cols = slice(j * FFN_BLK, (j + 1) * FFN_BLK)
            out_ref[rows, cols] = x[:, cols] + jnp.dot(hm_ref[rows, :], wdn_ref[:, cols],
                                                       preferred_element_type=F32)
            yield
        if final:
            out_ref[rows, :] = _rms(out_ref[rows, :], nf_ref[...])

    _interleave([chain(c) for c in range(ROW_CHAINS)], FFN_STAGGER)


def _ffn_prompt_call(x1, lp, nf, final, nseq, seqlen):
    tm = FFN_TM
    nchunk = seqlen // tm
    consts = (lp['n2'], lp['w_up'], lp['ffn_cw'], lp['ffn_cb'], lp['w_dn'], nf)
    row = pl.BlockSpec((tm, D_MODEL), lambda b, c: (b * nchunk + c, 0))
    return pl.pallas_call(
        functools.partial(_ffn_kernel, final, 0, tm),
        grid=(nseq, nchunk),
        in_specs=[row] + [_wspec(c) for c in consts],
        out_specs=[row, pl.BlockSpec((1, FFN_CONV - 1, D_FF), lambda b, c: (b, 0, 0))],
        out_shape=[jax.ShapeDtypeStruct(x1.shape, F32),
                   jax.ShapeDtypeStruct((nseq, FFN_CONV - 1, D_FF), F32)],
        scratch_shapes=[pltpu.VMEM((tm, D_FF), BF16), pltpu.VMEM((SUBLANE, D_FF), F32)],
        compiler_params=_params(("parallel", "arbitrary")),
        name="ffn_prompt",
    )(x1, *[_warg(c) for c in consts])


def _ffn_sample_call(x1, sp1, sp2, lp, nf, final):
    rows = x1.shape[0]
    tm = SMALL_TM
    consts = (lp['n2'], lp['w_up'], lp['ffn_cw'], lp['ffn_cb'], lp['w_dn'], nf)
    row = lambda c: pl.BlockSpec((tm, c), lambda i: (i, 0))
    return pl.pallas_call(
        functools.partial(_ffn_kernel, final, SAMPLE_PAD, tm),
        grid=(rows // tm,),
        in_specs=[row(D_MODEL), row(D_FF), row(D_FF)] + [_wspec(c) for c in consts],
        out_specs=[row(D_MODEL), row(D_FF)],
        out_shape=[jax.ShapeDtypeStruct(x1.shape, F32), jax.ShapeDtypeStruct((rows, D_FF), F32)],
        scratch_shapes=[pltpu.VMEM((tm, D_FF), BF16)],
        compiler_params=_params(("arbitrary",)),
        name="ffn_sample",
    )(x1, sp1, sp2, *[_warg(c) for c in consts])


def _layer_params(l, lb, norm1_w, w_in, s5_log_dt, s5_lambda_re, s5_lambda_im, s5_b_re, s5_b_im,
                  s5_c_re, s5_c_im, s5_d, s5_w_glu, ssd_conv_w, ssd_conv_b, ssd_dt_bias, ssd_a_log,
                  ssd_d, ssd_norm_w, ssd_w_out, hg_norm_w, hg_w_out, w_o, norm2_w, ffn_w_up,
                  ffn_conv_w, ffn_conv_b, ffn_w_down):
    row = lambda v: v.astype(F32).reshape(1, -1)
    lp = {}
    lp['n1'] = row(norm1_w[l])
    widths = (S5_WIDTH, SSD_INNER, SSD_CONV_DIM, SSD_HEADS, HG_WIDTH, HG_WIDTH, HG_WIDTH, HG_WIDTH,
              3 * D_MODEL)
    names = ('w_ua', 'w_z', 'w_xbc', 'w_dt', 'w_q', 'w_f', 'w_i', 'w_g', 'w_gt')
    off = 0
    for name, w in zip(names, widths):
        lp[name] = w_in[l][:, off:off + w].astype(BF16)
        off += w
    lp['w_dt'] = jnp.pad(lp['w_dt'], ((0, 0), (0, LANE - SSD_HEADS)))
    lp['dtb'] = jnp.pad(row(ssd_dt_bias[l]), ((0, 0), (0, LANE - SSD_HEADS)))
    lp['loglb'] = jnp.log(row(lb))
    lp['log1mlb'] = jnp.log1p(-row(lb))
    lp['omlb'] = 1.0 - row(lb)

    delta = jnp.exp(s5_log_dt[l].astype(F32))[:, None]
    lr = s5_lambda_re[l].astype(F32)
    li = s5_lambda_im[l].astype(F32)
    mag = jnp.exp(lr * delta)
    ab_re = mag * jnp.cos(li * delta)
    ab_im = mag * jnp.sin(li * delta)
    den = lr * lr + li * li
    nr = ab_re - 1.0
    co_re = (nr * lr + ab_im * li) / den
    co_im = (ab_im * lr - nr * li) / den
    br = s5_b_re[l].astype(F32)
    bi = s5_b_im[l].astype(F32)
    bb_re = co_re[..., None] * br - co_im[..., None] * bi
    bb_im = co_re[..., None] * bi + co_im[..., None] * br
    eye = jnp.eye(S5_GROUPS, dtype=F32)
    bd_in = lambda m: jnp.einsum('gnj,gh->gjhn', m, eye).reshape(S5_WIDTH, S5_LANES)
    bd_out = lambda m: jnp.einsum('gjn,gh->gnhj', m, eye).reshape(S5_LANES, S5_WIDTH)
    lp['s5_bb'] = jnp.concatenate([bd_in(bb_re), bd_in(bb_im)], axis=1).astype(BF16)
    lp['s5_cc'] = jnp.concatenate([bd_out(s5_c_re[l].astype(F32)),
                                   -bd_out(s5_c_im[l].astype(F32))], axis=0).astype(BF16)
    pw = jnp.asarray(list(range(1, S5_R + 1)) + [2 * S5_R, 4 * S5_R], F32)[:, None, None]
    pm = jnp.exp(lr * delta * pw)
    ph = li * delta * pw
    pad = ((0, 40 - (S5_R + 2)), (0, 0))
    lp['s5_apr'] = jnp.pad((pm * jnp.cos(ph)).reshape(-1, S5_LANES), pad)
    lp['s5_api'] = jnp.pad((pm * jnp.sin(ph)).reshape(-1, S5_LANES), pad)
    lp['s5_apr8'] = jnp.repeat(lp['s5_apr'][:S5_R + 2], S5_CH, axis=0)
    lp['s5_api8'] = jnp.repeat(lp['s5_api'][:S5_R + 2], S5_CH, axis=0)
    npair = S5_LANES // LANE
    pair_cols = lambda m_re, m_im: jnp.stack(
        [m_re.reshape(-1, npair, LANE), m_im.reshape(-1, npair, LANE)], axis=2).reshape(-1, 2 * S5_LANES)
    bb_int = pair_cols(bd_in(bb_re), bd_in(bb_im))
    cc_int = pair_cols(bd_out(s5_c_re[l].astype(F32)).T, -bd_out(s5_c_im[l].astype(F32)).T).T
    kw = 2 * LANE
    nlc = 2 * S5_LANES // S5_LC
    ktile = lambda lc: lc * S5_LC // (2 * S5_LANES // (S5_WIDTH // kw))
    lp['s5_bbc'] = jnp.stack([bb_int[ktile(lc) * kw:(ktile(lc) + 1) * kw, lc * S5_LC:(lc + 1) * S5_LC]
                              for lc in range(nlc)]).astype(BF16)
    lp['s5_ccc'] = jnp.stack([cc_int[lc * S5_LC:(lc + 1) * S5_LC, ktile(lc) * kw:(ktile(lc) + 1) * kw]
                              for lc in range(nlc)]).astype(BF16)
    lp['s5_d'] = row(s5_d[l])
    lp['w_glu'] = _Layered(s5_w_glu.astype(BF16), l)

    lp['conv_w'] = ssd_conv_w[l].astype(F32)
    lp['conv_b'] = row(ssd_conv_b[l])
    lp['ssd_arow'] = jnp.pad(-jnp.exp(row(ssd_a_log[l])), ((0, 0), (0, LANE - SSD_HEADS)))
    lp['ssd_dexp'] = jnp.repeat(row(ssd_d[l]), SSD_HEAD_DIM, axis=1)
    lp['ssd_nw'] = row(ssd_norm_w[l])
    lp['w_ssd'] = _Layered(ssd_w_out.astype(BF16), l)
    lp['hg_nw'] = row(hg_norm_w[l])
    lp['w_hg'] = _Layered(hg_w_out.astype(BF16), l)
    lp['w_o'] = _Layered(w_o.astype(BF16), l)
    lp['n2'] = row(norm2_w[l])
    lp['w_up'] = _Layered(ffn_w_up.astype(BF16), l)
    lp['ffn_cw'] = ffn_conv_w[l].astype(F32)
    lp['ffn_cb'] = row(ffn_conv_b[l])
    lp['w_dn'] = _Layered(ffn_w_down.astype(BF16), l)
    return lp


_PRE_KEYS = ('ua', 'xc', 'dt', 'q', 'lf', 'kk', 'v')


def kernel(x_prompt, x_sample, state_s5_re, state_s5_im, state_ssd, state_ssd_conv, state_hgrn, state_ffn_conv, norm1_w, w_in, s5_log_dt, s5_lambda_re, s5_lambda_im, s5_b_re, s5_b_im, s5_c_re, s5_c_im, s5_d, s5_w_glu, ssd_conv_w, ssd_conv_b, ssd_dt_bias, ssd_a_log, ssd_d, ssd_norm_w, ssd_w_out, hg_lb_logits, hg_norm_w, hg_w_out, w_o, norm2_w, ffn_w_up, ffn_conv_w, ffn_conv_b, ffn_w_down, norm_f_w):
    nb_p, len_p, _ = x_prompt.shape
    nb_s, len_s, _ = x_sample.shape
    assert len_p % S5_T == 0 and len_p % MIX_T == 0 and len_p % SMALL_TM == 0 and len_p % FFN_TM == 0
    assert len_s <= SAMPLE_PAD and len_s >= SSD_CONV - 1
    assert (nb_s * SAMPLE_PAD) % SMALL_TM == 0 and (nb_s * SAMPLE_PAD) % BIG_TM == 0
    lb_cum = jnp.cumsum(jax.nn.softmax(hg_lb_logits.astype(F32), axis=0), axis=0)
    lb_all = lb_cum - lb_cum[0:1]
    nf = norm_f_w.astype(F32).reshape(1, -1)

    xp = x_prompt.astype(F32).reshape(nb_p * len_p, D_MODEL)
    xs = jnp.pad(x_sample.astype(F32), ((0, 0), (0, SAMPLE_PAD - len_s), (0, 0)))
    xs = xs.reshape(nb_s * SAMPLE_PAD, D_MODEL)
    zeros_p = (jnp.zeros((1, nb_p, SSD_INNER, SSD_STATE), F32),
               jnp.zeros((1, nb_p, HG_WIDTH, HG_VDIM), F32))
    states_s = (state_ssd.astype(F32).reshape(DEPTH, nb_s, SSD_INNER, SSD_STATE),
                state_hgrn.astype(F32).reshape(DEPTH, nb_s, HG_WIDTH, HG_VDIM))
    mix_p = None
    mix_s = None
    new_p = []
    new_s = []
    pad_t = lambda rows: jnp.concatenate(
        rows + [jnp.zeros((nb_s, SAMPLE_PAD - len(rows), rows[0].shape[-1]), F32)], axis=1
    ).reshape(nb_s * SAMPLE_PAD, rows[0].shape[-1])
    for l in range(DEPTH):
        lp = _layer_params(l, lb_all[l], norm1_w, w_in, s5_log_dt, s5_lambda_re, s5_lambda_im,
                           s5_b_re, s5_b_im, s5_c_re, s5_c_im, s5_d, s5_w_glu, ssd_conv_w,
                           ssd_conv_b, ssd_dt_bias, ssd_a_log, ssd_d, ssd_norm_w, ssd_w_out,
                           hg_norm_w, hg_w_out, w_o, norm2_w, ffn_w_up, ffn_conv_w, ffn_conv_b,
                           ffn_w_down)
        final = l == DEPTH - 1

        *pre, p_conv = _pre_prompt_call(xp, lp, nb_p, len_p)
        pre = dict(zip(_PRE_KEYS, pre))
        ya, p_re, p_im = _s5_prompt_call(pre['ua'], lp, nb_p, len_p)
        yb, oc, *mix_p = _mix_call(pre, zeros_p, 0, l, mix_p, lp, nb_p, len_p, MIX_T, MIX_T,
                                   MIX_NB_PROMPT, SSD_Q, HG_Q)
        x1 = _post_call(xp, ya, yb.reshape(-1, SSD_INNER), oc.reshape(-1, HG_WIDTH), lp)
        xp, p_ffn = _ffn_prompt_call(x1, lp, nf, final, nb_p, len_p)
        new_p.append((p_re.reshape(nb_p, S5_GROUPS, S5_STATE), p_im.reshape(nb_p, S5_GROUPS, S5_STATE),
                      p_conv, p_ffn))

        cs = state_ssd_conv[l].astype(F32)
        r = lambda i: cs[:, i:i + 1]
        *pre, raw = _pre_sample_call(xs, pad_t([r(2)]), pad_t([r(1), r(2)]), pad_t([r(0), r(1), r(2)]), lp)
        pre = dict(zip(_PRE_KEYS, pre))
        s_conv = raw.reshape(nb_s, SAMPLE_PAD, SSD_CONV_DIM)[:, len_s - (SSD_CONV - 1):len_s]
        ua_tm = pre['ua'].reshape(nb_s, SAMPLE_PAD, S5_WIDTH)[:, :len_s].transpose(1, 0, 2)
        ya_tm, s_re, s_im = _s5_sample_call(
            ua_tm.reshape(len_s * nb_s, S5_WIDTH),
            state_s5_re[l].astype(F32).reshape(nb_s, S5_LANES),
            state_s5_im[l].astype(F32).reshape(nb_s, S5_LANES), lp, len_s, nb_s)
        ya = jnp.pad(ya_tm.reshape(len_s, nb_s, S5_WIDTH).transpose(1, 0, 2),
                     ((0, 0), (0, SAMPLE_PAD - len_s), (0, 0))).reshape(nb_s * SAMPLE_PAD, S5_WIDTH)
        yb, oc, *mix_s = _mix_call(pre, states_s, l, l, mix_s, lp, nb_s, SAMPLE_PAD, SAMPLE_PAD, len_s,
                                   MIX_NB_SAMPLE, SAMPLE_PAD, SAMPLE_PAD)
        x1 = _post_call(xs, ya, yb.reshape(-1, SSD_INNER), oc.reshape(-1, HG_WIDTH), lp)
        st = state_ffn_conv[l].astype(F32)
        f = lambda i: st[:, i:i + 1]
        xs, aff = _ffn_sample_call(x1, pad_t([f(1)]), pad_t([f(0), f(1)]), lp, nf, final)
        s_ffn = aff.reshape(nb_s, SAMPLE_PAD, D_FF)[:, len_s - (FFN_CONV - 1):len_s]
        new_s.append((s_re.reshape(nb_s, S5_GROUPS, S5_STATE), s_im.reshape(nb_s, S5_GROUPS, S5_STATE),
                      s_conv, s_ffn))

    stk = lambda lst, i: jnp.stack([s[i] for s in lst])
    y_prompt = xp.reshape(nb_p, len_p, D_MODEL)
    y_sample = xs.reshape(nb_s, SAMPLE_PAD, D_MODEL)[:, :len_s]

    def mix_states(m, n):
        ssd, hg = m
        return (ssd.reshape(DEPTH, n, SSD_HEADS, SSD_HEAD_DIM, SSD_STATE),
                hg.reshape(DEPTH, n, HG_HEADS, HG_KDIM, HG_VDIM))

    p_ssd, p_hg = mix_states(mix_p, nb_p)
    s_ssd, s_hg = mix_states(mix_s, nb_s)
    return (y_prompt, y_sample,
            stk(new_p, 0), stk(new_p, 1), p_ssd, stk(new_p, 2), p_hg, stk(new_p, 3),
            stk(new_s, 0), stk(new_s, 1), s_ssd, stk(new_s, 2), s_hg, stk(new_s, 3))
```

```python
import functools
import math

import jax
import jax.numpy as jnp
import numpy as np
from jax import lax
from jax.experimental import pallas as pl
from jax.experimental.pallas import tpu as pltpu

F32 = jnp.float32
BF16 = jnp.bfloat16

D_MODEL = 1024
DEPTH = 2
S5_WIDTH = 512
S5_GROUP = 16
S5_GROUPS = 32
S5_STATE = 64
S5_LANES = S5_GROUPS * S5_STATE
SSD_INNER = 1024
SSD_HEAD_DIM = 64
SSD_HEADS = 16
SSD_GROUPS = 4
SSD_STATE = 128
SSD_CONV = 4
SSD_CONV_DIM = 2048
HG_WIDTH = 512
HG_HEADS = 4
HG_KDIM = 128
HG_VDIM = 128
D_FF = 2816
FFN_CONV = 3
EPS = 1e-6

LANE = 128
SUBLANE = 8
SAMPLE_PAD = 8
HG_SAFE_LOG = -80.0

BIG_TM = 512
FFN_TM = 1024
SMALL_TM = 256
PRE_BLK = 256
POST_BLK = 256
FFN_BLK = 256
S5_T = 256
S5_CH = 8
S5_R = S5_T // S5_CH
S5_LC = 512
S5_LC_TOGETHER = 2
MIX_T = 128
SSD_Q = 128
HG_Q = 64
MIX_STAGGER = 3
MIX_STAGGER_MOD = 7
ROW_CHAINS = 2
ROW_STAGGER = 5
FFN_STAGGER = 11
MIX_NB_PROMPT = 2
MIX_NB_SAMPLE = 8
VMEM_LIMIT = 56 * 1024 * 1024


def _const_spec(shape):
    nd = len(shape)
    return pl.BlockSpec(shape, lambda *_: (0,) * nd, pipeline_mode=pl.Buffered(1))


class _Layered:
    def __init__(self, stacked, layer):
        self.stacked, self.layer = stacked, layer


def _wspec(c):
    if isinstance(c, _Layered):
        shape, layer = c.stacked.shape, c.layer
        return pl.BlockSpec((None,) + shape[1:], lambda *_: (layer,) + (0,) * (len(shape) - 1),
                            pipeline_mode=pl.Buffered(1))
    return _const_spec(c.shape)


def _warg(c):
    return c.stacked if isinstance(c, _Layered) else c


def _params(sem):
    return pltpu.CompilerParams(dimension_semantics=sem, vmem_limit_bytes=VMEM_LIMIT)


def _bdot(a, b):
    return jnp.dot(a.astype(BF16), b.astype(BF16), preferred_element_type=F32)


def _bdot_nt(a, b):
    return lax.dot_general(a.astype(BF16), b.astype(BF16), (((1,), (1,)), ((), ())),
                           preferred_element_type=F32)


def _bdot_tn(a, b):
    return lax.dot_general(a.astype(BF16), b.astype(BF16), (((0,), (0,)), ((), ())),
                           preferred_element_type=F32)


def _split3(x):
    h = x.astype(BF16)
    r = x - h.astype(F32)
    m = r.astype(BF16)
    l = (r - m.astype(F32)).astype(BF16)
    return h, m, l


def _split_cat(x, n):
    return jnp.concatenate(_split3(x)[:n], axis=1)


def _sel_left(m01, x):
    h, m, l = _split3(x)
    d = lambda p: jnp.dot(m01, p, preferred_element_type=F32)
    return (d(h) + d(m)) + d(l)


def _sel_right(x, m01):
    h, m, l = _split3(x)
    d = lambda p: jnp.dot(p, m01, preferred_element_type=F32)
    return (d(h) + d(m)) + d(l)


def _interleave(chains, stagger, modulus=None):
    chains = list(chains)
    lead = [stagger * i if modulus is None else (stagger * i) % modulus for i in range(len(chains))]
    for n, c in zip(lead, reversed(chains)):
        for _ in range(n):
            next(c, None)
    while chains:
        for c in list(chains):
            try:
                next(c)
            except StopIteration:
                chains.remove(c)


def _sigmoid(x):
    return 1.0 / (1.0 + jnp.exp(-x))


def _silu(x):
    h = 0.5 * x
    return h + h * jnp.tanh(h)


def _softplus(x):
    return jnp.maximum(x, 0.0) + jnp.log1p(jnp.exp(-jnp.abs(x)))


def _log1pexp_neg(d):
    return jnp.log(1.0 + jnp.exp(-d))


def _rms(x, w):
    return x * lax.rsqrt(jnp.mean(x * x, axis=-1, keepdims=True) + EPS) * w


def _pre_kernel(seq_rows, tm, x_ref, *rest):
    if seq_rows:
        (s1_ref, s2_ref, s3_ref, n1_ref, w_ua, w_xbc, w_dt, w_q, w_f, w_i, cw_ref, cb_ref,
         dtb_ref, loglb_ref, log1mlb_ref, omlb_ref,
         ua_o, xc_o, dt_o, q_o, lf_o, kk_o, v_o, raw_o) = rest
        t = lax.broadcasted_iota(jnp.int32, (tm, 1), 0) % seq_rows
    else:
        (n1_ref, w_ua, w_xbc, w_dt, w_q, w_f, w_i, cw_ref, cb_ref,
         dtb_ref, loglb_ref, log1mlb_ref, omlb_ref,
         ua_o, xc_o, dt_o, q_o, lf_o, kk_o, v_o, cst_o, carry_ref) = rest

        @pl.when(pl.program_id(1) == 0)
        def _():
            carry_ref[...] = jnp.zeros_like(carry_ref)
    tail = SSD_CONV - 1
    hr = tm // ROW_CHAINS

    def chain(c):
        rows = slice(c * hr, (c + 1) * hr)
        hb = _rms(x_ref[rows, :], n1_ref[...]).astype(BF16)
        dot = lambda w: jnp.dot(hb, w, preferred_element_type=F32)
        ua_o[rows, :] = dot(w_ua[...])
        yield
        for j in range(SSD_CONV_DIM // PRE_BLK):
            cols = slice(j * PRE_BLK, (j + 1) * PRE_BLK)
            raw = dot(w_xbc[:, cols])
            if seq_rows:
                acc = cw_ref[tail:tail + 1, cols] * raw + cb_ref[:, cols]
                for k, s_ref in ((1, s1_ref), (2, s2_ref), (3, s3_ref)):
                    prev = jnp.where(t[rows, :] < k, s_ref[rows, cols], pltpu.roll(raw, k, axis=0))
                    acc = acc + cw_ref[tail - k:tail - k + 1, cols] * prev
                raw_o[rows, cols] = raw
            else:
                xx = jnp.concatenate([carry_ref[:, cols], raw], axis=0)
                acc = cw_ref[0:1, cols] * xx
                for k in range(1, SSD_CONV):
                    acc = pltpu.roll(acc, 1, axis=0) + cw_ref[k:k + 1, cols] * xx
                acc = acc[SUBLANE:, :] + cb_ref[:, cols]
                carry_ref[:, cols] = raw[hr - SUBLANE:hr, :]
                if c == ROW_CHAINS - 1:
                    cst_o[0, :, cols] = raw[hr - tail:hr, :]
            xc_o[rows, cols] = _silu(acc)
            yield
        dt_o[rows, :] = _softplus(dot(w_dt[...]) + dtb_ref[...])
        q_o[rows, :] = dot(w_q[...])
        yield
        zf = dot(w_f[...])
        log_sig = jnp.minimum(zf, 0.0) - _log1pexp_neg(jnp.abs(zf))
        a = loglb_ref[...]
        b = log1mlb_ref[...] + log_sig
        lf_o[rows, :] = jnp.maximum(a, b) + _log1pexp_neg(jnp.abs(a - b))
        kk_o[rows, :] = omlb_ref[...] * _sigmoid(-zf)
        yield
        v_o[rows, :] = dot(w_i[...])
        yield

    _interleave([chain(c) for c in range(ROW_CHAINS)], ROW_STAGGER)


_PRE_WIDTHS = (S5_WIDTH, SSD_CONV_DIM, LANE, HG_WIDTH, HG_WIDTH, HG_WIDTH, HG_WIDTH)


def _pre_consts(lp):
    return (lp['n1'], lp['w_ua'], lp['w_xbc'], lp['w_dt'], lp['w_q'], lp['w_f'], lp['w_i'],
            lp['conv_w'], lp['conv_b'], lp['dtb'], lp['loglb'], lp['log1mlb'], lp['omlb'])


def _pre_prompt_call(x2d, lp, nseq, seqlen):
    tm = BIG_TM
    nchunk = seqlen // tm
    consts = _pre_consts(lp)
    row = lambda c: pl.BlockSpec((tm, c), lambda b, k: (b * nchunk + k, 0))
    tail = SSD_CONV - 1
    return pl.pallas_call(
        functools.partial(_pre_kernel, 0, tm),
        grid=(nseq, nchunk),
        in_specs=[row(D_MODEL)] + [_const_spec(c.shape) for c in consts],
        out_specs=[row(c) for c in _PRE_WIDTHS]
        + [pl.BlockSpec((1, tail, SSD_CONV_DIM), lambda b, k: (b, 0, 0))],
        out_shape=[jax.ShapeDtypeStruct((x2d.shape[0], c), F32) for c in _PRE_WIDTHS]
        + [jax.ShapeDtypeStruct((nseq, tail, SSD_CONV_DIM), F32)],
        scratch_shapes=[pltpu.VMEM((SUBLANE, SSD_CONV_DIM), F32)],
        compiler_params=_params(("parallel", "arbitrary")),
        name="pre_proj_prompt",
    )(x2d, *consts)


def _pre_sample_call(x2d, s1, s2, s3, lp):
    rows = x2d.shape[0]
    tm = SMALL_TM
    consts = _pre_consts(lp)
    row = lambda c: pl.BlockSpec((tm, c), lambda i: (i, 0))
    return pl.pallas_call(
        functools.partial(_pre_kernel, SAMPLE_PAD, tm),
        grid=(rows // tm,),
        in_specs=[row(D_MODEL)] + [row(SSD_CONV_DIM)] * 3 + [_const_spec(c.shape) for c in consts],
        out_specs=[row(c) for c in _PRE_WIDTHS] + [row(SSD_CONV_DIM)],
        out_shape=[jax.ShapeDtypeStruct((rows, c), F32) for c in _PRE_WIDTHS]
        + [jax.ShapeDtypeStruct((rows, SSD_CONV_DIM), F32)],
        compiler_params=_params(("parallel",)),
        name="pre_proj_sample",
    )(x2d, s1, s2, s3, *consts)


def _cmul_add(ar, ai, xr, xi, br, bi):
    return ar * xr - ai * xi + br, ar * xi + ai * xr + bi


def _s5_prompt_kernel(ua_ref, perm_ref, permt_ref, bbc_ref, ccc_ref, apr_ref, api_ref, d_ref,
                      ya_ref, st_ref, bu_ref):
    @pl.when(pl.program_id(1) == 0)
    def _():
        st_ref[...] = jnp.zeros_like(st_ref)

    ua = ua_ref[...]
    up = jnp.dot(perm_ref[...], ua.astype(BF16), preferred_element_type=F32).astype(BF16)
    rowid = lax.broadcasted_iota(jnp.int32, (S5_CH, LANE), 0)
    tab = lambda ref, k, gp: ref[k * S5_CH:(k + 1) * S5_CH, gp * LANE:(gp + 1) * LANE]
    kw = 2 * LANE
    y_acc = [None] * (S5_WIDTH // kw)

    def pair_chain(gp):
        re = slice(gp * kw, gp * kw + LANE)
        im = slice(gp * kw + LANE, (gp + 1) * kw)
        a_r, a_i = tab(apr_ref, 0, gp), tab(api_ref, 0, gp)
        sr = jnp.zeros((S5_CH, LANE), F32)
        si = jnp.zeros((S5_CH, LANE), F32)
        for r in range(S5_R):
            rows = slice(r * S5_CH, (r + 1) * S5_CH)
            sr, si = _cmul_add(a_r, a_i, sr, si, bu_ref[rows, re], bu_ref[rows, im])
            yield
        pr = jnp.broadcast_to(st_ref[0, :, re], (S5_CH, LANE))
        pi = jnp.broadcast_to(st_ref[0, :, im], (S5_CH, LANE))
        vr = jnp.where(rowid == 0, pr, pltpu.roll(sr, 1, axis=0))
        vi = jnp.where(rowid == 0, pi, pltpu.roll(si, 1, axis=0))
        for d, k in ((1, S5_R - 1), (2, S5_R), (4, S5_R + 1)):
            keep = rowid >= d
            tr = jnp.where(keep, pltpu.roll(vr, d, axis=0), 0.0)
            ti = jnp.where(keep, pltpu.roll(vi, d, axis=0), 0.0)
            vr, vi = _cmul_add(tab(apr_ref, k, gp), tab(api_ref, k, gp), tr, ti, vr, vi)
            yield
        xr, xi = vr, vi
        for r in range(S5_R):
            rows = slice(r * S5_CH, (r + 1) * S5_CH)
            xr, xi = _cmul_add(a_r, a_i, xr, xi, bu_ref[rows, re], bu_ref[rows, im])
            bu_ref[rows, re] = xr
            bu_ref[rows, im] = xi
            yield
        st_ref[0, :, re] = xr[S5_CH - 1:S5_CH, :]
        st_ref[0, :, im] = xi[S5_CH - 1:S5_CH, :]

    nlc = 2 * S5_LANES // S5_LC
    ktile = lambda lc: lc * S5_LC // (2 * S5_LANES // (S5_WIDTH // kw))
    lanes = lambda lc: slice(lc * S5_LC, (lc + 1) * S5_LC)
    for lc0 in range(0, nlc, S5_LC_TOGETHER):
        lcs = range(lc0, lc0 + S5_LC_TOGETHER)
        for lc in lcs:
            bu_ref[:, lanes(lc)] = jnp.dot(up[:, ktile(lc) * kw:(ktile(lc) + 1) * kw], bbc_ref[lc],
                                           preferred_element_type=F32)
        _interleave([pair_chain(gp) for gp in range(lc0 * S5_LC // kw, (lc0 + S5_LC_TOGETHER) * S5_LC // kw)], 0)
        for lc in lcs:
            part = _bdot(bu_ref[:, lanes(lc)], ccc_ref[lc])
            y_acc[ktile(lc)] = part if y_acc[ktile(lc)] is None else y_acc[ktile(lc)] + part

    yp = jnp.concatenate(y_acc, axis=1)
    ya_ref[...] = _sel_left(permt_ref[...], yp) + d_ref[...] * ua


def _s5_perm():
    p = np.zeros((S5_T, S5_T), np.float32)
    for i in range(S5_T):
        p[i, (i % S5_CH) * S5_R + i // S5_CH] = 1.0
    return jnp.asarray(p, BF16), jnp.asarray(p.T, BF16)


def _s5_prompt_call(ua, lp, nseq, seqlen):
    nchunk = seqlen // S5_T
    perm, permt = _s5_perm()
    consts = (perm, permt, lp['s5_bbc'], lp['s5_ccc'], lp['s5_apr8'], lp['s5_api8'], lp['s5_d'])
    st_spec = pl.BlockSpec((1, 1, 2 * S5_LANES), lambda b, c: (b, 0, 0))
    st_shape = jax.ShapeDtypeStruct((nseq, 1, 2 * S5_LANES), F32)
    ya, st = pl.pallas_call(
        _s5_prompt_kernel,
        grid=(nseq, nchunk),
        in_specs=[pl.BlockSpec((S5_T, S5_WIDTH), lambda b, c: (b * nchunk + c, 0))]
        + [_const_spec(c.shape) for c in consts],
        out_specs=[pl.BlockSpec((S5_T, S5_WIDTH), lambda b, c: (b * nchunk + c, 0)), st_spec],
        out_shape=[jax.ShapeDtypeStruct(ua.shape, F32), st_shape],
        scratch_shapes=[pltpu.VMEM((S5_T, 2 * S5_LANES), F32)],
        compiler_params=_params(("parallel", "arbitrary")),
        name="s5_prompt",
    )(ua, *consts)
    st = st.reshape(nseq, S5_LANES // LANE, 2, LANE)
    return ya, st[:, :, 0].reshape(nseq, S5_LANES), st[:, :, 1].reshape(nseq, S5_LANES)


def _s5_sample_kernel(nstep, nseq, ua_ref, sre0_ref, sim0_ref, bb_ref, cc_ref, apr_ref, api_ref,
                      d_ref, ya_ref, sre_ref, sim_ref, bu_ref, xb_ref):
    ua = ua_ref[...]
    bu_ref[...] = jnp.dot(ua.astype(BF16), bb_ref[...], preferred_element_type=F32)
    for lc in range(S5_LANES // S5_LC):
        re = slice(lc * S5_LC, (lc + 1) * S5_LC)
        im = slice(S5_LANES + lc * S5_LC, S5_LANES + (lc + 1) * S5_LC)
        a_r = apr_ref[0:1, re]
        a_i = api_ref[0:1, re]
        sr = sre0_ref[:, re]
        si = sim0_ref[:, re]
        for t in range(nstep):
            rows = slice(t * nseq, (t + 1) * nseq)
            sr, si = _cmul_add(a_r, a_i, sr, si, bu_ref[rows, re], bu_ref[rows, im])
            xb_ref[rows, re] = sr.astype(BF16)
            xb_ref[rows, im] = si.astype(BF16)
        sre_ref[:, re] = sr
        sim_ref[:, re] = si
    ya_ref[...] = jnp.dot(xb_ref[...], cc_ref[...], preferred_element_type=F32) + d_ref[...] * ua


def _s5_sample_call(ua_tm, sre0, sim0, lp, nstep, nseq):
    rows = nstep * nseq
    args = (ua_tm, sre0, sim0, lp['s5_bb'], lp['s5_cc'], lp['s5_apr'], lp['s5_api'], lp['s5_d'])
    st_shape = jax.ShapeDtypeStruct((nseq, S5_LANES), F32)
    full = lambda s: pl.BlockSpec(s, lambda i: (0,) * len(s))
    return pl.pallas_call(
        functools.partial(_s5_sample_kernel, nstep, nseq),
        grid=(1,),
        in_specs=[full(a.shape) for a in args],
        out_specs=[full((rows, S5_WIDTH)), full((nseq, S5_LANES)), full((nseq, S5_LANES))],
        out_shape=[jax.ShapeDtypeStruct((rows, S5_WIDTH), F32), st_shape, st_shape],
        scratch_shapes=[pltpu.VMEM((rows, 2 * S5_LANES), F32), pltpu.VMEM((rows, 2 * S5_LANES), BF16)],
        compiler_params=_params(("arbitrary",)),
        name="s5_sample",
    )(*args)


def _mix_kernel(t_rows, n_valid, nb, ssd_q, hg_q, n_alias,
                xc_ref, dt_ref, q_ref, lf_ref, kk_ref, v_ref, ssd0_ref, hg0_ref,
                arow_ref, dexp_ref, ltri_s_ref, ltri_h_ref, colsel_ref, headsel_ref, *rest):
    (yb_ref, oc_ref, ssd_ref, hg_ref, hb_ref, hk_ref, ho_ref, hq_ref, hv_ref,
     hqe_ref, hkd_ref, hke_ref, ebt_ref, cb_ref, gm_ref, ce_ref, xw_ref, acb_ref, eab_ref,
     ac_ref, dtm_ref) = rest[n_alias:]
    assert t_rows == ssd_q

    @pl.when(pl.program_id(1) == 0)
    def _():
        ssd_ref[...] = ssd0_ref[...]
        hg_ref[...] = hg0_ref[...]

    n_rows = nb * t_rows
    if n_valid < t_rows:
        valid_all = (lax.broadcasted_iota(jnp.int32, (n_rows, 1), 0) % t_rows < n_valid).astype(F32)
    else:
        valid_all = None
    sq = lambda m: m if m.shape[0] == LANE else jnp.concatenate(
        [m, jnp.zeros((LANE - m.shape[0], LANE), F32)], axis=0)

    def ssd_prepare():
        dt = dt_ref[...].reshape(n_rows, LANE)
        if valid_all is not None:
            dt = dt * valid_all
        acum = _sel_left(ltri_s_ref[...], dt * arow_ref[...])
        alast = jnp.concatenate(
            [jnp.broadcast_to(acum[(b + 1) * t_rows - 1:(b + 1) * t_rows, :], (t_rows, LANE))
             for b in range(nb)], axis=0)
        eacm = jnp.exp(acum)
        wcol = dt * jnp.exp(alast - acum)
        ac_ref[...] = acum
        dtm_ref[...] = dt
        acb_ref[...] = jnp.dot(_split_cat(acum, 3), colsel_ref[...], preferred_element_type=F32)
        eab_ref[...] = jnp.dot(_split_cat(eacm, 2), colsel_ref[0:2 * LANE, :], preferred_element_type=F32)
        wx = jnp.dot(_split_cat(wcol, 2), headsel_ref[...], preferred_element_type=F32)
        xw_ref[...] = (xc_ref[:, :, 0:SSD_INNER].reshape(n_rows, SSD_INNER) * wx).astype(xw_ref.dtype)

    def ssd_chunk(j, r0):
        q_ = ssd_q
        rows = slice(r0, r0 + q_)
        brows = slice(j * t_rows + r0, j * t_rows + r0 + q_)
        acum_t = sq(ac_ref[brows, :]).T
        dt_t = sq(dtm_ref[brows, :]).T
        ela_t = jnp.exp(jnp.broadcast_to(acum_t[:, q_ - 1:q_], (LANE, LANE)))
        tri = lax.broadcasted_iota(jnp.int32, (q_, q_), 0) >= lax.broadcasted_iota(jnp.int32, (q_, q_), 1)
        hpg = SSD_HEADS // SSD_GROUPS
        gw = hpg * SSD_HEAD_DIM
        b0 = SSD_INNER
        c0 = SSD_INNER + SSD_GROUPS * SSD_STATE
        cgs = lambda g: xc_ref[j, rows, c0 + g * SSD_STATE:c0 + (g + 1) * SSD_STATE]
        bgs = lambda g: xc_ref[j, rows, b0 + g * SSD_STATE:b0 + (g + 1) * SSD_STATE]
        for g in range(SSD_GROUPS):
            cb_ref[j, g] = _bdot_nt(cgs(g), bgs(g))
        yield
        for g in range(SSD_GROUPS):
            cg = cgs(g)
            for r in range(hpg):
                h = g * hpg + r
                seg = acb_ref[brows, h * LANE:h * LANE + q_] - acum_t[h:h + 1, :q_]
                dec = jnp.exp(jnp.where(tri, seg, -jnp.inf))
                gm_ref[j, h] = (cb_ref[j, g] * dec * dt_t[h:h + 1, :q_]).astype(gm_ref.dtype)
                ce_ref[j, h] = (cg * eab_ref[brows, h * LANE:(h + 1) * LANE]).astype(ce_ref.dtype)
            yield
        for g in range(SSD_GROUPS):
            bgb = bgs(g).astype(BF16)
            sg = ssd_ref[j, g * gw:(g + 1) * gw, :]
            sgb = sg.astype(BF16)
            scale = []
            for r in range(hpg):
                h = g * hpg + r
                hl = slice(h * SSD_HEAD_DIM, (h + 1) * SSD_HEAD_DIM)
                xs_h = xc_ref[j, rows, hl]
                yb_ref[j, rows, hl] = (_bdot(gm_ref[j, h], xs_h)
                                       + _bdot_nt(ce_ref[j, h], sgb[r * SSD_HEAD_DIM:(r + 1) * SSD_HEAD_DIM, :])
                                       + dexp_ref[:, hl] * xs_h)
                scale.append(jnp.broadcast_to(ela_t[h:h + 1, :], (SSD_HEAD_DIM, SSD_STATE)))
            upd = _bdot_tn(xw_ref[brows, g * gw:(g + 1) * gw], bgb)
            ssd_ref[j, g * gw:(g + 1) * gw, :] = sg * jnp.concatenate(scale, axis=0) + upd
            yield

    def hg_prepare():
        flat = lambda ref: ref[...].reshape(n_rows, HG_WIDTH)
        lf = flat(lf_ref)
        kk = flat(kk_ref)
        if valid_all is not None:
            lf = lf * valid_all
            kk = kk * valid_all
        bc = _sel_left(ltri_h_ref[...], lf)
        hb_ref[...] = bc
        hk_ref[...] = kk
        safe = jnp.min(bc) >= HG_SAFE_LOG

        @pl.when(safe)
        def _():
            ho_ref[...] = jnp.zeros((n_rows, HG_WIDTH), F32)

        @pl.when(jnp.logical_not(safe))
        def _():
            ho_ref[...] = jnp.zeros((n_rows, HG_WIDTH), F32)
            hq_ref[...] = flat(q_ref)
            hv_ref[...] = flat(v_ref)
            rid = lax.broadcasted_iota(jnp.int32, (n_rows, 1), 0)

            def body(s, c):
                brow = hb_ref[pl.ds(s, 1), :]
                krow = hk_ref[pl.ds(s, 1), :]
                vrow = hv_ref[pl.ds(s, 1), :]
                w = hq_ref[...] * krow * jnp.exp(jnp.minimum(hb_ref[...] - brow, 0.0))
                same_chunk_end = (s // hg_q + 1) * hg_q
                w = jnp.where(jnp.logical_and(rid >= s, rid < same_chunk_end), w, 0.0)
                for h in range(HG_HEADS):
                    hl = slice(h * HG_KDIM, (h + 1) * HG_KDIM)
                    a = jnp.sum(w[:, hl], axis=-1, keepdims=True)
                    ho_ref[:, hl] = ho_ref[:, hl] + a * vrow[:, hl]
                return c

            lax.fori_loop(0, n_rows, body, 0)

        return safe

    def hg_chunk(j, r0, safe):
        q_ = hg_q
        rows = slice(r0, r0 + q_)
        brows = slice(j * t_rows + r0, j * t_rows + r0 + q_)
        qq = q_ref[j, rows, :]
        vv = v_ref[j, rows, :]
        bc = hb_ref[brows, :]
        kk = hk_ref[brows, :]
        blast = bc[q_ - 1:q_, :]
        mm = hqe_ref.dtype
        hqe_ref[brows, :] = (qq * jnp.exp(bc)).astype(mm)
        hkd_ref[brows, :] = (kk * jnp.exp(blast - bc)).astype(mm)
        hke_ref[brows, :] = (kk * jnp.exp(-bc)).astype(mm)
        eb = jnp.exp(blast)
        for h in range(HG_HEADS):
            hl = slice(h * HG_KDIM, (h + 1) * HG_KDIM)
            ebt_ref[j, h] = jnp.broadcast_to(eb[:, hl], (HG_KDIM, HG_KDIM)).T
        yield
        tri = lax.broadcasted_iota(jnp.int32, (q_, q_), 0) >= lax.broadcasted_iota(jnp.int32, (q_, q_), 1)
        for h in range(HG_HEADS):
            hl = slice(h * HG_KDIM, (h + 1) * HG_KDIM)
            qe = hqe_ref[brows, hl]
            att = jnp.where(tri, _bdot_nt(qe, hke_ref[brows, hl]), 0.0)
            intra = jnp.where(safe, _bdot(att, vv[:, hl]), ho_ref[brows, hl])
            sh = hg_ref[j, hl, :]
            oc_ref[j, rows, hl] = intra + _bdot(qe, sh)
            hg_ref[j, hl, :] = sh * ebt_ref[j, h] + _bdot_tn(hkd_ref[brows, hl], vv[:, hl])
            yield

    def ssd_seq(j):
        for c in range(t_rows // ssd_q):
            yield from ssd_chunk(j, c * ssd_q)

    def hg_seq(j, safe):
        for c in range(t_rows // hg_q):
            yield from hg_chunk(j, c * hg_q, safe)

    safe = hg_prepare()
    ssd_prepare()
    _interleave([t for j in range(nb) for t in (ssd_seq(j), hg_seq(j, safe))],
                MIX_STAGGER, MIX_STAGGER_MOD)


def _ltri(n, q):
    return jnp.asarray(np.kron(np.eye(n // q, dtype=np.float32), np.tril(np.ones((q, q), np.float32))), BF16)


def _head_selectors():
    col = np.zeros((3, LANE, SSD_HEADS * LANE), np.float32)
    head = np.zeros((2, LANE, SSD_INNER), np.float32)
    for h in range(SSD_HEADS):
        col[:, h, h * LANE:(h + 1) * LANE] = 1.0
        head[:, h, h * SSD_HEAD_DIM:(h + 1) * SSD_HEAD_DIM] = 1.0
    return (jnp.asarray(col.reshape(3 * LANE, -1), BF16), jnp.asarray(head.reshape(2 * LANE, -1), BF16))


def _mix_call(pre, states, l_in, l_out, prev, lp, nseq, seqlen, t_rows, n_valid, nb, ssd_q, hg_q):
    nchunk = seqlen // t_rows
    view = lambda a: a.reshape(nseq, seqlen, a.shape[-1])
    seq_in = [view(pre[k]) for k in ('xc', 'dt', 'q', 'lf', 'kk', 'v')]
    consts = (lp['ssd_arow'], lp['ssd_dexp'], _ltri(nb * t_rows, ssd_q), _ltri(nb * t_rows, hg_q),
              *_head_selectors())
    tile = lambda c: pl.BlockSpec((nb, t_rows, c), lambda i, k: (i, k, 0))
    st = lambda a, l: pl.BlockSpec((None, nb) + a.shape[2:], lambda i, k: (l, i, 0, 0))
    prev = () if prev is None else tuple(prev)
    n_in = len(seq_in) + len(states) + len(consts)
    mm = BF16 if ssd_q % (2 * SUBLANE) == 0 and hg_q % (2 * SUBLANE) == 0 else F32
    kern = functools.partial(_mix_kernel, t_rows, n_valid, nb, ssd_q, hg_q, len(prev))
    return pl.pallas_call(
        kern,
        grid=(nseq // nb, nchunk),
        in_specs=[tile(a.shape[-1]) for a in seq_in] + [st(a, l_in) for a in states]
        + [_const_spec(c.shape) for c in consts]
        + [pl.BlockSpec(memory_space=pl.ANY) for _ in prev],
        out_specs=[tile(SSD_INNER), tile(HG_WIDTH)] + [st(a, l_out) for a in states],
        out_shape=[jax.ShapeDtypeStruct((nseq, seqlen, SSD_INNER), F32),
                   jax.ShapeDtypeStruct((nseq, seqlen, HG_WIDTH), F32)]
        + [jax.ShapeDtypeStruct((DEPTH,) + a.shape[1:], F32) for a in states],
        input_output_aliases={n_in + k: 2 + k for k in range(len(prev))},
        scratch_shapes=[pltpu.VMEM((nb * t_rows, HG_WIDTH), F32) for _ in range(5)]
        + [pltpu.VMEM((nb * t_rows, HG_WIDTH), mm) for _ in range(3)]
        + [pltpu.VMEM((nb, HG_HEADS, HG_KDIM, HG_KDIM), F32),
           pltpu.VMEM((nb, SSD_GROUPS, ssd_q, ssd_q), F32),
           pltpu.VMEM((nb, SSD_HEADS, ssd_q, ssd_q), mm),
           pltpu.VMEM((nb, SSD_HEADS, ssd_q, SSD_STATE), mm),
           pltpu.VMEM((nb * t_rows, SSD_INNER), mm),
           pltpu.VMEM((nb * t_rows, SSD_HEADS * LANE), F32),
           pltpu.VMEM((nb * t_rows, SSD_HEADS * LANE), F32),
           pltpu.VMEM((nb * t_rows, LANE), F32),
           pltpu.VMEM((nb * t_rows, LANE), F32)],
        compiler_params=_params(("parallel", "arbitrary")),
        name="ssd_hgrn_mix",
    )(*seq_in, *states, *consts, *prev)


def _post_kernel(x_ref, ya_ref, yb_ref, oc_ref, n1_ref, wz_ref, wg_ref, wgt_ref,
                 wglu_ref, ssdnw_ref, wssd_ref, hgnw_ref, whg_ref, wo_ref, x1_ref, mix_ref):
    x = x_ref[...]
    hb = _rms(x, n1_ref[...]).astype(BF16)
    proj = lambda w: jnp.dot(hb, w, preferred_element_type=F32)
    ya = ya_ref[...]
    cdf = 0.5 * (1.0 + jnp.tanh(math.sqrt(2.0 / math.pi) * (ya + 0.044715 * (ya * ya * ya))))
    ga = (ya * cdf).astype(BF16)
    z = proj(wz_ref[...])
    nb_ = _rms(yb_ref[...] * _silu(z), ssdnw_ref[...]).astype(BF16)
    oc = oc_ref[...]
    normed = []
    for h in range(HG_HEADS):
        hl = slice(h * HG_VDIM, (h + 1) * HG_VDIM)
        normed.append(_rms(oc[:, hl], hgnw_ref[:, hl]))
    g = proj(wg_ref[...])
    nc = (jnp.concatenate(normed, axis=-1) * _silu(g)).astype(BF16)
    for j in range(D_MODEL // POST_BLK):
        cols = slice(j * POST_BLK, (j + 1) * POST_BLK)
        dot = lambda a, w_ref, off: jnp.dot(a, w_ref[:, off + j * POST_BLK:off + (j + 1) * POST_BLK],
                                            preferred_element_type=F32)
        out_a = dot(ga, wglu_ref, 0) * _sigmoid(dot(ga, wglu_ref, D_MODEL))
        out_b = dot(nb_, wssd_ref, 0)
        out_c = dot(nc, whg_ref, 0)
        mixed = (_sigmoid(dot(hb, wgt_ref, 0)) * out_a + _sigmoid(dot(hb, wgt_ref, D_MODEL)) * out_b
                 + _sigmoid(dot(hb, wgt_ref, 2 * D_MODEL)) * out_c)
        mix_ref[:, cols] = mixed.astype(BF16)
    x1_ref[...] = x + jnp.dot(mix_ref[...], wo_ref[...], preferred_element_type=F32)


def _post_call(x2d, ya, yb, oc, lp):
    rows = x2d.shape[0]
    tm = BIG_TM
    seq_in = (x2d, ya, yb, oc)
    consts = (lp['n1'], lp['w_z'], lp['w_g'], lp['w_gt'],
              lp['w_glu'], lp['ssd_nw'], lp['w_ssd'], lp['hg_nw'], lp['w_hg'], lp['w_o'])
    row = lambda c: pl.BlockSpec((tm, c), lambda i: (i, 0))
    return pl.pallas_call(
        _post_kernel,
        grid=(rows // tm,),
        in_specs=[row(a.shape[-1]) for a in seq_in] + [_wspec(c) for c in consts],
        out_specs=row(D_MODEL),
        out_shape=jax.ShapeDtypeStruct((rows, D_MODEL), F32),
        scratch_shapes=[pltpu.VMEM((tm, D_MODEL), BF16)],
        compiler_params=_params(("parallel",)),
        name="post_merge",
    )(*seq_in, *[_warg(c) for c in consts])


def _ffn_kernel(final, seq_rows, tm, x_ref, *rest):
    if seq_rows:
        (sp1_ref, sp2_ref, n2_ref, wup_ref, cw_ref, cb_ref, wdn_ref, nf_ref,
         out_ref, aff_ref, hm_ref) = rest
        t = lax.broadcasted_iota(jnp.int32, (tm, 1), 0) % seq_rows
    else:
        (n2_ref, wup_ref, cw_ref, cb_ref, wdn_ref, nf_ref, out_ref, aff_ref, hm_ref, carry_ref) = rest

        @pl.when(pl.program_id(1) == 0)
        def _():
            carry_ref[...] = jnp.zeros_like(carry_ref)
    tail = FFN_CONV - 1
    hr = tm // ROW_CHAINS

    def chain(c):
        rows = slice(c * hr, (c + 1) * hr)
        x = x_ref[rows, :]
        hb = _rms(x, n2_ref[...]).astype(BF16)
        for j in range(D_FF // FFN_BLK):
            cols = slice(j * FFN_BLK, (j + 1) * FFN_BLK)
            a = jnp.dot(hb, wup_ref[:, cols], preferred_element_type=F32)
            g = jnp.dot(hb, wup_ref[:, D_FF + j * FFN_BLK:D_FF + (j + 1) * FFN_BLK],
                        preferred_element_type=F32)
            p1 = pltpu.roll(a, 1, axis=0)
            p2 = pltpu.roll(a, 2, axis=0)
            if seq_rows:
                p1 = jnp.where(t[rows, :] < 1, sp1_ref[rows, cols], p1)
                p2 = jnp.where(t[rows, :] < 2, sp2_ref[rows, cols], p2)
                aff_ref[rows, cols] = a
            else:
                head = jnp.concatenate([carry_ref[:, cols], a[0:SUBLANE, :]], axis=0)
                p1 = jnp.concatenate([pltpu.roll(head, 1, axis=0)[SUBLANE:], p1[SUBLANE:, :]], axis=0)
                p2 = jnp.concatenate([pltpu.roll(head, 2, axis=0)[SUBLANE:], p2[SUBLANE:, :]], axis=0)
                carry_ref[:, cols] = a[hr - SUBLANE:hr, :]
                if c == ROW_CHAINS - 1:
                    aff_ref[0, :, cols] = a[hr - tail:hr, :]
            ac = (cw_ref[0:1, cols] * p2 + cw_ref[1:2, cols] * p1 + cw_ref[2:3, cols] * a
                  + cb_ref[:, cols])
            hm_ref[rows, cols] = (_silu(ac) * g).astype(BF16)
            yield
        for j in range(D_MODEL // FFN_BLK):
            cols = slice(j * FFN_BLK, (j + 1) * FFN_BLK)
            out_ref[rows, cols] = x[:, cols] + jnp.dot(hm_ref[rows, :], wdn_ref[:, cols],
                                                       preferred_element_type=F32)
            yield
        if final:
            out_ref[rows, :] = _rms(out_ref[rows, :], nf_ref[...])

    _interleave([chain(c) for c in range(ROW_CHAINS)], FFN_STAGGER)


def _ffn_prompt_call(x1, lp, nf, final, nseq, seqlen):
    tm = FFN_TM
    nchunk = seqlen // tm
    consts = (lp['n2'], lp['w_up'], lp['ffn_cw'], lp['ffn_cb'], lp['w_dn'], nf)
    row = pl.BlockSpec((tm, D_MODEL), lambda b, c: (b * nchunk + c, 0))
    return pl.pallas_call(
        functools.partial(_ffn_kernel, final, 0, tm),
        grid=(nseq, nchunk),
        in_specs=[row] + [_wspec(c) for c in consts],
        out_specs=[row, pl.BlockSpec((1, FFN_CONV - 1, D_FF), lambda b, c: (b, 0, 0))],
        out_shape=[jax.ShapeDtypeStruct(x1.shape, F32),
                   jax.ShapeDtypeStruct((nseq, FFN_CONV - 1, D_FF), F32)],
        scratch_shapes=[pltpu.VMEM((tm, D_FF), BF16), pltpu.VMEM((SUBLANE, D_FF), F32)],
        compiler_params=_params(("parallel", "arbitrary")),
        name="ffn_prompt",
    )(x1, *[_warg(c) for c in consts])


def _ffn_sample_call(x1, sp1, sp2, lp, nf, final):
    rows = x1.shape[0]
    tm = SMALL_TM
    consts = (lp['n2'], lp['w_up'], lp['ffn_cw'], lp['ffn_cb'], lp['w_dn'], nf)
    row = lambda c: pl.BlockSpec((tm, c), lambda i: (i, 0))
    return pl.pallas_call(
        functools.partial(_ffn_kernel, final, SAMPLE_PAD, tm),
        grid=(rows // tm,),
        in_specs=[row(D_MODEL), row(D_FF), row(D_FF)] + [_wspec(c) for c in consts],
        out_specs=[row(D_MODEL), row(D_FF)],
        out_shape=[jax.ShapeDtypeStruct(x1.shape, F32), jax.ShapeDtypeStruct((rows, D_FF), F32)],
        scratch_shapes=[pltpu.VMEM((tm, D_FF), BF16)],
        compiler_params=_params(("arbitrary",)),
        name="ffn_sample",
    )(x1, sp1, sp2, *[_warg(c) for c in consts])


def _layer_params(l, lb, norm1_w, w_in, s5_log_dt, s5_lambda_re, s5_lambda_im, s5_b_re, s5_b_im,
                  s5_c_re, s5_c_im, s5_d, s5_w_glu, ssd_conv_w, ssd_conv_b, ssd_dt_bias, ssd_a_log,
                  ssd_d, ssd_norm_w, ssd_w_out, hg_norm_w, hg_w_out, w_o, norm2_w, ffn_w_up,
                  ffn_conv_w, ffn_conv_b, ffn_w_down):
    row = lambda v: v.astype(F32).reshape(1, -1)
    lp = {}
    lp['n1'] = row(norm1_w[l])
    widths = (S5_WIDTH, SSD_INNER, SSD_CONV_DIM, SSD_HEADS, HG_WIDTH, HG_WIDTH, HG_WIDTH, HG_WIDTH,
              3 * D_MODEL)
    names = ('w_ua', 'w_z', 'w_xbc', 'w_dt', 'w_q', 'w_f', 'w_i', 'w_g', 'w_gt')
    off = 0
    for name, w in zip(names, widths):
        lp[name] = w_in[l][:, off:off + w].astype(BF16)
        off += w
    lp['w_dt'] = jnp.pad(lp['w_dt'], ((0, 0), (0, LANE - SSD_HEADS)))
    lp['dtb'] = jnp.pad(row(ssd_dt_bias[l]), ((0, 0), (0, LANE - SSD_HEADS)))
    lp['loglb'] = jnp.log(row(lb))
    lp['log1mlb'] = jnp.log1p(-row(lb))
    lp['omlb'] = 1.0 - row(lb)

    delta = jnp.exp(s5_log_dt[l].astype(F32))[:, None]
    lr = s5_lambda_re[l].astype(F32)
    li = s5_lambda_im[l].astype(F32)
    mag = jnp.exp(lr * delta)
    ab_re = mag * jnp.cos(li * delta)
    ab_im = mag * jnp.sin(li * delta)
    den = lr * lr + li * li
    nr = ab_re - 1.0
    co_re = (nr * lr + ab_im * li) / den
    co_im = (ab_im * lr - nr * li) / den
    br = s5_b_re[l].astype(F32)
    bi = s5_b_im[l].astype(F32)
    bb_re = co_re[..., None] * br - co_im[..., None] * bi
    bb_im = co_re[..., None] * bi + co_im[..., None] * br
    eye = jnp.eye(S5_GROUPS, dtype=F32)
    bd_in = lambda m: jnp.einsum('gnj,gh->gjhn', m, eye).reshape(S5_WIDTH, S5_LANES)
    bd_out = lambda m: jnp.einsum('gjn,gh->gnhj', m, eye).reshape(S5_LANES, S5_WIDTH)
    lp['s5_bb'] = jnp.concatenate([bd_in(bb_re), bd_in(bb_im)], axis=1).astype(BF16)
    lp['s5_cc'] = jnp.concatenate([bd_out(s5_c_re[l].astype(F32)),
                                   -bd_out(s5_c_im[l].astype(F32))], axis=0).astype(BF16)
    pw = jnp.asarray(list(range(1, S5_R + 1)) + [2 * S5_R, 4 * S5_R], F32)[:, None, None]
    pm = jnp.exp(lr * delta * pw)
    ph = li * delta * pw
    pad = ((0, 40 - (S5_R + 2)), (0, 0))
    lp['s5_apr'] = jnp.pad((pm * jnp.cos(ph)).reshape(-1, S5_LANES), pad)
    lp['s5_api'] = jnp.pad((pm * jnp.sin(ph)).reshape(-1, S5_LANES), pad)
    lp['s5_apr8'] = jnp.repeat(lp['s5_apr'][:S5_R + 2], S5_CH, axis=0)
    lp['s5_api8'] = jnp.repeat(lp['s5_api'][:S5_R + 2], S5_CH, axis=0)
    npair = S5_LANES // LANE
    pair_cols = lambda m_re, m_im: jnp.stack(
        [m_re.reshape(-1, npair, LANE), m_im.reshape(-1, npair, LANE)], axis=2).reshape(-1, 2 * S5_LANES)
    bb_int = pair_cols(bd_in(bb_re), bd_in(bb_im))
    cc_int = pair_cols(bd_out(s5_c_re[l].astype(F32)).T, -bd_out(s5_c_im[l].astype(F32)).T).T
    kw = 2 * LANE
    nlc = 2 * S5_LANES // S5_LC
    ktile = lambda lc: lc * S5_LC // (2 * S5_LANES // (S5_WIDTH // kw))
    lp['s5_bbc'] = jnp.stack([bb_int[ktile(lc) * kw:(ktile(lc) + 1) * kw, lc * S5_LC:(lc + 1) * S5_LC]
                              for lc in range(nlc)]).astype(BF16)
    lp['s5_ccc'] = jnp.stack([cc_int[lc * S5_LC:(lc + 1) * S5_LC, ktile(lc) * kw:(ktile(lc) + 1) * kw]
                              for lc in range(nlc)]).astype(BF16)
    lp['s5_d'] = row(s5_d[l])
    lp['w_glu'] = _Layered(s5_w_glu.astype(BF16), l)

    lp['conv_w'] = ssd_conv_w[l].astype(F32)
    lp['conv_b'] = row(ssd_conv_b[l])
    lp['ssd_arow'] = jnp.pad(-jnp.exp(row(ssd_a_log[l])), ((0, 0), (0, LANE - SSD_HEADS)))
    lp['ssd_dexp'] = jnp.repeat(row(ssd_d[l]), SSD_HEAD_DIM, axis=1)
    lp['ssd_nw'] = row(ssd_norm_w[l])
    lp['w_ssd'] = _Layered(ssd_w_out.astype(BF16), l)
    lp['hg_nw'] = row(hg_norm_w[l])
    lp['w_hg'] = _Layered(hg_w_out.astype(BF16), l)
    lp['w_o'] = _Layered(w_o.astype(BF16), l)
    lp['n2'] = row(norm2_w[l])
    lp['w_up'] = _Layered(ffn_w_up.astype(BF16), l)
    lp['ffn_cw'] = ffn_conv_w[l].astype(F32)
    lp['ffn_cb'] = row(ffn_conv_b[l])
    lp['w_dn'] = _Layered(ffn_w_down.astype(BF16), l)
    return lp


_PRE_KEYS = ('ua', 'xc', 'dt', 'q', 'lf', 'kk', 'v')


def kernel(x_prompt, x_sample, state_s5_re, state_s5_im, state_ssd, state_ssd_conv, state_hgrn, state_ffn_conv, norm1_w, w_in, s5_log_dt, s5_lambda_re, s5_lambda_im, s5_b_re, s5_b_im, s5_c_re, s5_c_im, s5_d, s5_w_glu, ssd_conv_w, ssd_conv_b, ssd_dt_bias, ssd_a_log, ssd_d, ssd_norm_w, ssd_w_out, hg_lb_logits, hg_norm_w, hg_w_out, w_o, norm2_w, ffn_w_up, ffn_conv_w, ffn_conv_b, ffn_w_down, norm_f_w):
    nb_p, len_p, _ = x_prompt.shape
    nb_s, len_s, _ = x_sample.shape
    assert len_p % S5_T == 0 and len_p % MIX_T == 0 and len_p % SMALL_TM == 0 and len_p % FFN_TM == 0
    assert len_s <= SAMPLE_PAD and len_s >= SSD_CONV - 1
    assert (nb_s * SAMPLE_PAD) % SMALL_TM == 0 and (nb_s * SAMPLE_PAD) % BIG_TM == 0
    lb_cum = jnp.cumsum(jax.nn.softmax(hg_lb_logits.astype(F32), axis=0), axis=0)
    lb_all = lb_cum - lb_cum[0:1]
    nf = norm_f_w.astype(F32).reshape(1, -1)

    xp = x_prompt.astype(F32).reshape(nb_p * len_p, D_MODEL)
    xs = jnp.pad(x_sample.astype(F32), ((0, 0), (0, SAMPLE_PAD - len_s), (0, 0)))
    xs = xs.reshape(nb_s * SAMPLE_PAD, D_MODEL)
    zeros_p = (jnp.zeros((1, nb_p, SSD_INNER, SSD_STATE), F32),
               jnp.zeros((1, nb_p, HG_WIDTH, HG_VDIM), F32))
    states_s = (state_ssd.astype(F32).reshape(DEPTH, nb_s, SSD_INNER, SSD_STATE),
                state_hgrn.astype(F32).reshape(DEPTH, nb_s, HG_WIDTH, HG_VDIM))
    mix_p = None
    mix_s = None
    new_p = []
    new_s = []
    pad_t = lambda rows: jnp.concatenate(
        rows + [jnp.zeros((nb_s, SAMPLE_PAD - len(rows), rows[0].shape[-1]), F32)], axis=1
    ).reshape(nb_s * SAMPLE_PAD, rows[0].shape[-1])
    for l in range(DEPTH):
        lp = _layer_params(l, lb_all[l], norm1_w, w_in, s5_log_dt, s5_lambda_re, s5_lambda_im,
                           s5_b_re, s5_b_im, s5_c_re, s5_c_im, s5_d, s5_w_glu, ssd_conv_w,
                           ssd_conv_b, ssd_dt_bias, ssd_a_log, ssd_d, ssd_norm_w, ssd_w_out,
                           hg_norm_w, hg_w_out, w_o, norm2_w, ffn_w_up, ffn_conv_w, ffn_conv_b,
                           ffn_w_down)
        final = l == DEPTH - 1

        *pre, p_conv = _pre_prompt_call(xp, lp, nb_p, len_p)
        pre = dict(zip(_PRE_KEYS, pre))
        ya, p_re, p_im = _s5_prompt_call(pre['ua'], lp, nb_p, len_p)
        yb, oc, *mix_p = _mix_call(pre, zeros_p, 0, l, mix_p, lp, nb_p, len_p, MIX_T, MIX_T,
                                   MIX_NB_PROMPT, SSD_Q, HG_Q)
        x1 = _post_call(xp, ya, yb.reshape(-1, SSD_INNER), oc.reshape(-1, HG_WIDTH), lp)
        xp, p_ffn = _ffn_prompt_call(x1, lp, nf, final, nb_p, len_p)
        new_p.append((p_re.reshape(nb_p, S5_GROUPS, S5_STATE), p_im.reshape(nb_p, S5_GROUPS, S5_STATE),
                      p_conv, p_ffn))

        cs = state_ssd_conv[l].astype(F32)
        r = lambda i: cs[:, i:i + 1]
        *pre, raw = _pre_sample_call(xs, pad_t([r(2)]), pad_t([r(1), r(2)]), pad_t([r(0), r(1), r(2)]), lp)
        pre = dict(zip(_PRE_KEYS, pre))
        s_conv = raw.reshape(nb_s, SAMPLE_PAD, SSD_CONV_DIM)[:, len_s - (SSD_CONV - 1):len_s]
        ua_tm = pre['ua'].reshape(nb_s, SAMPLE_PAD, S5_WIDTH)[:, :len_s].transpose(1, 0, 2)
        ya_tm, s_re, s_im = _s5_sample_call(
            ua_tm.reshape(len_s * nb_s, S5_WIDTH),
            state_s5_re[l].astype(F32).reshape(nb_s, S5_LANES),
            state_s5_im[l].astype(F32).reshape(nb_s, S5_LANES), lp, len_s, nb_s)
        ya = jnp.pad(ya_tm.reshape(len_s, nb_s, S5_WIDTH).transpose(1, 0, 2),
                     ((0, 0), (0, SAMPLE_PAD - len_s), (0, 0))).reshape(nb_s * SAMPLE_PAD, S5_WIDTH)
        yb, oc, *mix_s = _mix_call(pre, states_s, l, l, mix_s, lp, nb_s, SAMPLE_PAD, SAMPLE_PAD, len_s,
                                   MIX_NB_SAMPLE, SAMPLE_PAD, SAMPLE_PAD)
        x1 = _post_call(xs, ya, yb.reshape(-1, SSD_INNER), oc.reshape(-1, HG_WIDTH), lp)
        st = state_ffn_conv[l].astype(F32)
        f = lambda i: st[:, i:i + 1]
        xs, aff = _ffn_sample_call(x1, pad_t([f(1)]), pad_t([f(0), f(1)]), lp, nf, final)
        s_ffn = aff.reshape(nb_s, SAMPLE_PAD, D_FF)[:, len_s - (FFN_CONV - 1):len_s]
        new_s.append((s_re.reshape(nb_s, S5_GROUPS, S5_STATE), s_im.reshape(nb_s, S5_GROUPS, S5_STATE),
                      s_conv, s_ffn))

    stk = lambda lst, i: jnp.stack([s[i] for s in lst])
    y_prompt = xp.reshape(nb_p, len_p, D_MODEL)
    y_sample = xs.reshape(nb_s, SAMPLE_PAD, D_MODEL)[:, :len_s]

    def mix_states(m, n):
        ssd, hg = m
        return (ssd.reshape(DEPTH, n, SSD_HEADS, SSD_HEAD_DIM, SSD_STATE),
                hg.reshape(DEPTH, n, HG_HEADS, HG_KDIM, HG_VDIM))

    p_ssd, p_hg = mix_states(mix_p, nb_p)
    s_ssd, s_hg = mix_states(mix_s, nb_s)
    return (y_prompt, y_sample,
            stk(new_p, 0), stk(new_p, 1), p_ssd, stk(new_p, 2), p_hg, stk(new_p, 3),
            stk(new_s, 0), stk(new_s, 1), s_ssd, stk(new_s, 2), s_hg, stk(new_s, 3))
```

```python
import functools
import math

import jax
import jax.numpy as jnp
import numpy as np
from jax import lax
from jax.experimental import pallas as pl
from jax.experimental.pallas import tpu as pltpu

F32 = jnp.float32
BF16 = jnp.bfloat16

D_MODEL = 1024
DEPTH = 2
S5_WIDTH = 512
S5_GROUP = 16
S5_GROUPS = 32
S5_STATE = 64
S5_LANES = S5_GROUPS * S5_STATE
SSD_INNER = 1024
SSD_HEAD_DIM = 64
SSD_HEADS = 16
SSD_GROUPS = 4
SSD_STATE = 128
SSD_CONV = 4
SSD_CONV_DIM = 2048
HG_WIDTH = 512
HG_HEADS = 4
HG_KDIM = 128
HG_VDIM = 128
D_FF = 2816
FFN_CONV = 3
EPS = 1e-6

LANE = 128
W_IN_WIDTH = {'ua': S5_WIDTH, 'z': SSD_INNER, 'xbc': SSD_CONV_DIM, 'dt': LANE, 'q': HG_WIDTH,
              'f': HG_WIDTH, 'i': HG_WIDTH, 'g': HG_WIDTH, 'gt': 3 * D_MODEL}
W_IN_OFF = {}
_off = 0
for _name in ('ua', 'z', 'xbc', 'dt', 'q', 'f', 'i', 'g', 'gt'):
    W_IN_OFF[_name] = _off
    _off += W_IN_WIDTH[_name]
SUBLANE = 8
SAMPLE_PAD = 8
HG_SAFE_LOG = -80.0

BIG_TM = 512
FFN_TM = 1024
SMALL_TM = 256
PRE_BLK = 256
POST_BLK = 256
FFN_BLK = 256
S5_T = 256
S5_CH = 8
S5_R = S5_T // S5_CH
S5_LC = 512
S5_LC_TOGETHER = 2
MIX_T = 128
SSD_Q = 128
HG_Q = 64
MIX_STAGGER = 3
MIX_STAGGER_MOD = 7
ROW_CHAINS = 2
ROW_STAGGER = 5
FFN_STAGGER = 11
MIX_NB_PROMPT = 2
MIX_NB_SAMPLE = 8
VMEM_LIMIT = 56 * 1024 * 1024


def _const_spec(shape):
    nd = len(shape)
    return pl.BlockSpec(shape, lambda *_: (0,) * nd, pipeline_mode=pl.Buffered(1))


class _Layered:
    def __init__(self, stacked, layer):
        self.stacked, self.layer = stacked, layer


def _wspec(c):
    if isinstance(c, _Layered):
        shape, layer = c.stacked.shape, c.layer
        return pl.BlockSpec((None,) + shape[1:], lambda *_: (layer,) + (0,) * (len(shape) - 1),
                            pipeline_mode=pl.Buffered(1))
    return _const_spec(c.shape)


def _warg(c):
    return c.stacked if isinstance(c, _Layered) else c


def _params(sem):
    return pltpu.CompilerParams(dimension_semantics=sem, vmem_limit_bytes=VMEM_LIMIT)


def _bdot(a, b):
    return jnp.dot(a.astype(BF16), b.astype(BF16), preferred_element_type=F32)


def _bdot_nt(a, b):
    return lax.dot_general(a.astype(BF16), b.astype(BF16), (((1,), (1,)), ((), ())),
                           preferred_element_type=F32)


def _bdot_tn(a, b):
    return lax.dot_general(a.astype(BF16), b.astype(BF16), (((0,), (0,)), ((), ())),
                           preferred_element_type=F32)


def _split3(x):
    h = x.astype(BF16)
    r = x - h.astype(F32)
    m = r.astype(BF16)
    l = (r - m.astype(F32)).astype(BF16)
    return h, m, l


def _split_cat(x, n):
    return jnp.concatenate(_split3(x)[:n], axis=1)


def _sel_left(m01, x):
    h, m, l = _split3(x)
    d = lambda p: jnp.dot(m01, p, preferred_element_type=F32)
    return (d(h) + d(m)) + d(l)


def _sel_right(x, m01):
    h, m, l = _split3(x)
    d = lambda p: jnp.dot(p, m01, preferred_element_type=F32)
    return (d(h) + d(m)) + d(l)


def _interleave(chains, stagger, modulus=None):
    chains = list(chains)
    lead = [stagger * i if modulus is None else (stagger * i) % modulus for i in range(len(chains))]
    for n, c in zip(lead, reversed(chains)):
        for _ in range(n):
            next(c, None)
    while chains:
        for c in list(chains):
            try:
                next(c)
            except StopIteration:
                chains.remove(c)


def _sigmoid(x):
    return 1.0 / (1.0 + jnp.exp(-x))


def _silu(x):
    h = 0.5 * x
    return h + h * jnp.tanh(h)


def _softplus(x):
    return jnp.maximum(x, 0.0) + jnp.log1p(jnp.exp(-jnp.abs(x)))


def _log1pexp_neg(d):
    return jnp.log(1.0 + jnp.exp(-d))


def _rms(x, w):
    return x * lax.rsqrt(jnp.mean(x * x, axis=-1, keepdims=True) + EPS) * w


def _pre_kernel(seq_rows, tm, x_ref, *rest):
    if seq_rows:
        (s1_ref, s2_ref, s3_ref, n1_ref, win_ref, cw_ref, cb_ref,
         dtb_ref, loglb_ref, log1mlb_ref, omlb_ref,
         ua_o, xc_o, dt_o, q_o, lf_o, kk_o, v_o, raw_o) = rest
        t = lax.broadcasted_iota(jnp.int32, (tm, 1), 0) % seq_rows
    else:
        (n1_ref, win_ref, cw_ref, cb_ref,
         dtb_ref, loglb_ref, log1mlb_ref, omlb_ref,
         ua_o, xc_o, dt_o, q_o, lf_o, kk_o, v_o, cst_o, carry_ref) = rest

        @pl.when(pl.program_id(1) == 0)
        def _():
            carry_ref[...] = jnp.zeros_like(carry_ref)
    seg = lambda name: win_ref[:, W_IN_OFF[name]:W_IN_OFF[name] + W_IN_WIDTH[name]]
    tail = SSD_CONV - 1
    hr = tm // ROW_CHAINS

    def chain(c):
        rows = slice(c * hr, (c + 1) * hr)
        hb = _rms(x_ref[rows, :], n1_ref[...]).astype(BF16)
        dot = lambda w: jnp.dot(hb, w, preferred_element_type=F32)
        ua_o[rows, :] = dot(seg('ua'))
        yield
        for j in range(SSD_CONV_DIM // PRE_BLK):
            cols = slice(j * PRE_BLK, (j + 1) * PRE_BLK)
            raw = dot(win_ref[:, W_IN_OFF['xbc'] + j * PRE_BLK:W_IN_OFF['xbc'] + (j + 1) * PRE_BLK])
            if seq_rows:
                acc = cw_ref[tail:tail + 1, cols] * raw + cb_ref[:, cols]
                for k, s_ref in ((1, s1_ref), (2, s2_ref), (3, s3_ref)):
                    prev = jnp.where(t[rows, :] < k, s_ref[rows, cols], pltpu.roll(raw, k, axis=0))
                    acc = acc + cw_ref[tail - k:tail - k + 1, cols] * prev
                raw_o[rows, cols] = raw
            else:
                def horner(xx):
                    acc = cw_ref[0:1, cols] * xx
                    for k in range(1, SSD_CONV):
                        acc = pltpu.roll(acc, 1, axis=0) + cw_ref[k:k + 1, cols] * xx
                    return acc

                head = horner(jnp.concatenate([carry_ref[:, cols], raw[0:SUBLANE, :]], axis=0))
                acc = jnp.concatenate([head[SUBLANE:, :], horner(raw)[SUBLANE:, :]], axis=0) + cb_ref[:, cols]
                carry_ref[:, cols] = raw[hr - SUBLANE:hr, :]
                if c == ROW_CHAINS - 1:
                    cst_o[0, :, cols] = raw[hr - tail:hr, :]
            xc_o[rows, cols] = _silu(acc)
            yield
        dt_o[rows, :] = _softplus(dot(seg('dt')) + dtb_ref[...])
        q_o[rows, :] = dot(seg('q'))
        yield
        zf = dot(seg('f'))
        log_sig = jnp.minimum(zf, 0.0) - _log1pexp_neg(jnp.abs(zf))
        a = loglb_ref[...]
        b = log1mlb_ref[...] + log_sig
        lf_o[rows, :] = jnp.maximum(a, b) + _log1pexp_neg(jnp.abs(a - b))
        kk_o[rows, :] = omlb_ref[...] * _sigmoid(-zf)
        yield
        v_o[rows, :] = dot(seg('i'))
        yield

    _interleave([chain(c) for c in range(ROW_CHAINS)], ROW_STAGGER)


_PRE_WIDTHS = (S5_WIDTH, SSD_CONV_DIM, LANE, HG_WIDTH, HG_WIDTH, HG_WIDTH, HG_WIDTH)


def _pre_consts(lp):
    return (lp['n1'], lp['w_in'],
            lp['conv_w'], lp['conv_b'], lp['dtb'], lp['loglb'], lp['log1mlb'], lp['omlb'])


def _pre_prompt_call(x2d, lp, nseq, seqlen):
    tm = BIG_TM
    nchunk = seqlen // tm
    consts = _pre_consts(lp)
    row = lambda c: pl.BlockSpec((tm, c), lambda b, k: (b * nchunk + k, 0))
    tail = SSD_CONV - 1
    return pl.pallas_call(
        functools.partial(_pre_kernel, 0, tm),
        grid=(nseq, nchunk),
        in_specs=[row(D_MODEL)] + [_wspec(c) for c in consts],
        out_specs=[row(c) for c in _PRE_WIDTHS]
        + [pl.BlockSpec((1, tail, SSD_CONV_DIM), lambda b, k: (b, 0, 0))],
        out_shape=[jax.ShapeDtypeStruct((x2d.shape[0], c), F32) for c in _PRE_WIDTHS]
        + [jax.ShapeDtypeStruct((nseq, tail, SSD_CONV_DIM), F32)],
        scratch_shapes=[pltpu.VMEM((SUBLANE, SSD_CONV_DIM), F32)],
        compiler_params=_params(("parallel", "arbitrary")),
        name="pre_proj_prompt",
    )(x2d, *[_warg(c) for c in consts])


def _pre_sample_call(x2d, s1, s2, s3, lp):
    rows = x2d.shape[0]
    tm = SMALL_TM
    consts = _pre_consts(lp)
    row = lambda c: pl.BlockSpec((tm, c), lambda i: (i, 0))
    return pl.pallas_call(
        functools.partial(_pre_kernel, SAMPLE_PAD, tm),
        grid=(rows // tm,),
        in_specs=[row(D_MODEL)] + [row(SSD_CONV_DIM)] * 3 + [_wspec(c) for c in consts],
        out_specs=[row(c) for c in _PRE_WIDTHS] + [row(SSD_CONV_DIM)],
        out_shape=[jax.ShapeDtypeStruct((rows, c), F32) for c in _PRE_WIDTHS]
        + [jax.ShapeDtypeStruct((rows, SSD_CONV_DIM), F32)],
        compiler_params=_params(("parallel",)),
        name="pre_proj_sample",
    )(x2d, s1, s2, s3, *[_warg(c) for c in consts])


def _cmul_add(ar, ai, xr, xi, br, bi):
    return ar * xr - ai * xi + br, ar * xi + ai * xr + bi


def _s5_prompt_kernel(ua_ref, perm_ref, permt_ref, bbc_ref, ccc_ref, apr_ref, api_ref, d_ref,
                      ya_ref, st_ref, bu_ref):
    @pl.when(pl.program_id(1) == 0)
    def _():
        st_ref[...] = jnp.zeros_like(st_ref)

    ua = ua_ref[...]
    up = jnp.dot(perm_ref[...], ua.astype(BF16), preferred_element_type=F32).astype(BF16)
    rowid = lax.broadcasted_iota(jnp.int32, (S5_CH, LANE), 0)
    tab = lambda ref, k, gp: ref[k * S5_CH:(k + 1) * S5_CH, gp * LANE:(gp + 1) * LANE]
    kw = 2 * LANE
    y_acc = [None] * (S5_WIDTH // kw)

    def pair_chain(gp):
        re = slice(gp * kw, gp * kw + LANE)
        im = slice(gp * kw + LANE, (gp + 1) * kw)
        a_r, a_i = tab(apr_ref, 0, gp), tab(api_ref, 0, gp)
        sr = jnp.zeros((S5_CH, LANE), F32)
        si = jnp.zeros((S5_CH, LANE), F32)
        for r in range(S5_R):
            rows = slice(r * S5_CH, (r + 1) * S5_CH)
            sr, si = _cmul_add(a_r, a_i, sr, si, bu_ref[rows, re], bu_ref[rows, im])
            yield
        pr = jnp.broadcast_to(st_ref[0, :, re], (S5_CH, LANE))
        pi = jnp.broadcast_to(st_ref[0, :, im], (S5_CH, LANE))
        vr = jnp.where(rowid == 0, pr, pltpu.roll(sr, 1, axis=0))
        vi = jnp.where(rowid == 0, pi, pltpu.roll(si, 1, axis=0))
        for d, k in ((1, S5_R - 1), (2, S5_R), (4, S5_R + 1)):
            keep = rowid >= d
            tr = jnp.where(keep, pltpu.roll(vr, d, axis=0), 0.0)
            ti = jnp.where(keep, pltpu.roll(vi, d, axis=0), 0.0)
            vr, vi = _cmul_add(tab(apr_ref, k, gp), tab(api_ref, k, gp), tr, ti, vr, vi)
            yield
        xr, xi = vr, vi
        for r in range(S5_R):
            rows = slice(r * S5_CH, (r + 1) * S5_CH)
            xr, xi = _cmul_add(a_r, a_i, xr, xi, bu_ref[rows, re], bu_ref[rows, im])
            bu_ref[rows, re] = xr
            bu_ref[rows, im] = xi
            yield
        st_ref[0, :, re] = xr[S5_CH - 1:S5_CH, :]
        st_ref[0, :, im] = xi[S5_CH - 1:S5_CH, :]

    nlc = 2 * S5_LANES // S5_LC
    ktile = lambda lc: lc * S5_LC // (2 * S5_LANES // (S5_WIDTH // kw))
    lanes = lambda lc: slice(lc * S5_LC, (lc + 1) * S5_LC)
    for lc0 in range(0, nlc, S5_LC_TOGETHER):
        lcs = range(lc0, lc0 + S5_LC_TOGETHER)
        for lc in lcs:
            bu_ref[:, lanes(lc)] = jnp.dot(up[:, ktile(lc) * kw:(ktile(lc) + 1) * kw], bbc_ref[lc],
                                           preferred_element_type=F32)
        _interleave([pair_chain(gp) for gp in range(lc0 * S5_LC // kw, (lc0 + S5_LC_TOGETHER) * S5_LC // kw)], 0)
        for lc in lcs:
            part = _bdot(bu_ref[:, lanes(lc)], ccc_ref[lc])
            y_acc[ktile(lc)] = part if y_acc[ktile(lc)] is None else y_acc[ktile(lc)] + part

    yp = jnp.concatenate(y_acc, axis=1)
    ya_ref[...] = _sel_left(permt_ref[...], yp) + d_ref[...] * ua


def _s5_perm():
    p = np.zeros((S5_T, S5_T), np.float32)
    for i in range(S5_T):
        p[i, (i % S5_CH) * S5_R + i // S5_CH] = 1.0
    return jnp.asarray(p, BF16), jnp.asarray(p.T, BF16)


def _s5_prompt_call(ua, lp, nseq, seqlen):
    nchunk = seqlen // S5_T
    perm, permt = _s5_perm()
    consts = (perm, permt, lp['s5_bbc'], lp['s5_ccc'], lp['s5_apr8'], lp['s5_api8'], lp['s5_d'])
    st_spec = pl.BlockSpec((1, 1, 2 * S5_LANES), lambda b, c: (b, 0, 0))
    st_shape = jax.ShapeDtypeStruct((nseq, 1, 2 * S5_LANES), F32)
    ya, st = pl.pallas_call(
        _s5_prompt_kernel,
        grid=(nseq, nchunk),
        in_specs=[pl.BlockSpec((S5_T, S5_WIDTH), lambda b, c: (b * nchunk + c, 0))]
        + [_const_spec(c.shape) for c in consts],
        out_specs=[pl.BlockSpec((S5_T, S5_WIDTH), lambda b, c: (b * nchunk + c, 0)), st_spec],
        out_shape=[jax.ShapeDtypeStruct(ua.shape, F32), st_shape],
        scratch_shapes=[pltpu.VMEM((S5_T, 2 * S5_LANES), F32)],
        compiler_params=_params(("parallel", "arbitrary")),
        name="s5_prompt",
    )(ua, *consts)
    st = st.reshape(nseq, S5_LANES // LANE, 2, LANE)
    return ya, st[:, :, 0].reshape(nseq, S5_LANES), st[:, :, 1].reshape(nseq, S5_LANES)


def _s5_sample_kernel(nstep, nseq, ua_ref, sre0_ref, sim0_ref, bb_ref, cc_ref, apr_ref, api_ref,
                      d_ref, ya_ref, sre_ref, sim_ref, bu_ref, xb_ref):
    ua = ua_ref[...]
    bu_ref[...] = jnp.dot(ua.astype(BF16), bb_ref[...], preferred_element_type=F32)
    for lc in range(S5_LANES // S5_LC):
        re = slice(lc * S5_LC, (lc + 1) * S5_LC)
        im = slice(S5_LANES + lc * S5_LC, S5_LANES + (lc + 1) * S5_LC)
        a_r = apr_ref[0:1, re]
        a_i = api_ref[0:1, re]
        sr = sre0_ref[:, re]
        si = sim0_ref[:, re]
        for t in range(nstep):
            rows = slice(t * nseq, (t + 1) * nseq)
            sr, si = _cmul_add(a_r, a_i, sr, si, bu_ref[rows, re], bu_ref[rows, im])
            xb_ref[rows, re] = sr.astype(BF16)
            xb_ref[rows, im] = si.astype(BF16)
        sre_ref[:, re] = sr
        sim_ref[:, re] = si
    ya_ref[...] = jnp.dot(xb_ref[...], cc_ref[...], preferred_element_type=F32) + d_ref[...] * ua


def _s5_sample_call(ua_tm, sre0, sim0, lp, nstep, nseq):
    rows = nstep * nseq
    args = (ua_tm, sre0, sim0, lp['s5_bb'], lp['s5_cc'], lp['s5_apr'], lp['s5_api'], lp['s5_d'])
    st_shape = jax.ShapeDtypeStruct((nseq, S5_LANES), F32)
    full = lambda s: pl.BlockSpec(s, lambda i: (0,) * len(s))
    return pl.pallas_call(
        functools.partial(_s5_sample_kernel, nstep, nseq),
        grid=(1,),
        in_specs=[full(a.shape) for a in args],
        out_specs=[full((rows, S5_WIDTH)), full((nseq, S5_LANES)), full((nseq, S5_LANES))],
        out_shape=[jax.ShapeDtypeStruct((rows, S5_WIDTH), F32), st_shape, st_shape],
        scratch_shapes=[pltpu.VMEM((rows, 2 * S5_LANES), F32), pltpu.VMEM((rows, 2 * S5_LANES), BF16)],
        compiler_params=_params(("arbitrary",)),
        name="s5_sample",
    )(*args)


def _mix_kernel(t_rows, n_valid, nb, ssd_q, hg_q, n_alias,
                xc_ref, dt_ref, q_ref, lf_ref, kk_ref, v_ref, ssd0_ref, hg0_ref,
                arow_ref, dexp_ref, ltri_s_ref, ltri_h_ref, colsel_ref, headsel_ref, *rest):
    (yb_ref, oc_ref, ssd_ref, hg_ref, hb_ref, hk_ref, ho_ref, hq_ref, hv_ref,
     hqe_ref, hkd_ref, hke_ref, ebt_ref, cb_ref, gm_ref, ce_ref, xw_ref, acb_ref, eab_ref,
     ac_ref, dtm_ref) = rest[n_alias:]
    assert t_rows == ssd_q

    @pl.when(pl.program_id(1) == 0)
    def _():
        ssd_ref[...] = ssd0_ref[...]
        hg_ref[...] = hg0_ref[...]

    n_rows = nb * t_rows
    if n_valid < t_rows:
        valid_all = (lax.broadcasted_iota(jnp.int32, (n_rows, 1), 0) % t_rows < n_valid).astype(F32)
    else:
        valid_all = None
    sq = lambda m: m if m.shape[0] == LANE else jnp.concatenate(
        [m, jnp.zeros((LANE - m.shape[0], LANE), F32)], axis=0)

    def ssd_prepare():
        dt = dt_ref[...].reshape(n_rows, LANE)
        if valid_all is not None:
            dt = dt * valid_all
        acum = _sel_left(ltri_s_ref[...], dt * arow_ref[...])
        alast = jnp.concatenate(
            [jnp.broadcast_to(acum[(b + 1) * t_rows - 1:(b + 1) * t_rows, :], (t_rows, LANE))
             for b in range(nb)], axis=0)
        eacm = jnp.exp(acum)
        wcol = dt * jnp.exp(alast - acum)
        ac_ref[...] = acum
        dtm_ref[...] = dt
        acb_ref[...] = jnp.dot(_split_cat(acum, 3), colsel_ref[...], preferred_element_type=F32)
        eab_ref[...] = jnp.dot(_split_cat(eacm, 2), colsel_ref[0:2 * LANE, :], preferred_element_type=F32)
        wx = jnp.dot(_split_cat(wcol, 2), headsel_ref[...], preferred_element_type=F32)
        xw_ref[...] = (xc_ref[:, :, 0:SSD_INNER].reshape(n_rows, SSD_INNER) * wx).astype(xw_ref.dtype)

    def ssd_chunk(j, r0):
        q_ = ssd_q
        rows = slice(r0, r0 + q_)
        brows = slice(j * t_rows + r0, j * t_rows + r0 + q_)
        acum_t = sq(ac_ref[brows, :]).T
        dt_t = sq(dtm_ref[brows, :]).T
        ela_t = jnp.exp(jnp.broadcast_to(acum_t[:, q_ - 1:q_], (LANE, LANE)))
        tri = lax.broadcasted_iota(jnp.int32, (q_, q_), 0) >= lax.broadcasted_iota(jnp.int32, (q_, q_), 1)
        hpg = SSD_HEADS // SSD_GROUPS
        gw = hpg * SSD_HEAD_DIM
        b0 = SSD_INNER
        c0 = SSD_INNER + SSD_GROUPS * SSD_STATE
        cgs = lambda g: xc_ref[j, rows, c0 + g * SSD_STATE:c0 + (g + 1) * SSD_STATE]
        bgs = lambda g: xc_ref[j, rows, b0 + g * SSD_STATE:b0 + (g + 1) * SSD_STATE]
        for g in range(SSD_GROUPS):
            cb_ref[j, g] = _bdot_nt(cgs(g), bgs(g))
        yield
        for g in range(SSD_GROUPS):
            cg = cgs(g)
            for r in range(hpg):
                h = g * hpg + r
                seg = acb_ref[brows, h * LANE:h * LANE + q_] - acum_t[h:h + 1, :q_]
                dec = jnp.exp(jnp.where(tri, seg, -jnp.inf))
                gm_ref[j, h] = (cb_ref[j, g] * dec * dt_t[h:h + 1, :q_]).astype(gm_ref.dtype)
                ce_ref[j, h] = (cg * eab_ref[brows, h * LANE:(h + 1) * LANE]).astype(ce_ref.dtype)
            yield
        for g in range(SSD_GROUPS):
            bgb = bgs(g).astype(BF16)
            sg = ssd_ref[j, g * gw:(g + 1) * gw, :]
            sgb = sg.astype(BF16)
            scale = []
            for r in range(hpg):
                h = g * hpg + r
                hl = slice(h * SSD_HEAD_DIM, (h + 1) * SSD_HEAD_DIM)
                xs_h = xc_ref[j, rows, hl]
                yb_ref[j, rows, hl] = (_bdot(gm_ref[j, h], xs_h)
                                       + _bdot_nt(ce_ref[j, h], sgb[r * SSD_HEAD_DIM:(r + 1) * SSD_HEAD_DIM, :])
                                       + dexp_ref[:, hl] * xs_h)
                scale.append(jnp.broadcast_to(ela_t[h:h + 1, :], (SSD_HEAD_DIM, SSD_STATE)))
            upd = _bdot_tn(xw_ref[brows, g * gw:(g + 1) * gw], bgb)
            ssd_ref[j, g * gw:(g + 1) * gw, :] = sg * jnp.concatenate(scale, axis=0) + upd
            yield

    def hg_prepare():
        flat = lambda ref: ref[...].reshape(n_rows, HG_WIDTH)
        lf = flat(lf_ref)
        kk = flat(kk_ref)
        if valid_all is not None:
            lf = lf * valid_all
            kk = kk * valid_all
        bc = _sel_left(ltri_h_ref[...], lf)
        hb_ref[...] = bc
        hk_ref[...] = kk
        safe = jnp.min(bc) >= HG_SAFE_LOG

        @pl.when(safe)
        def _():
            ho_ref[...] = jnp.zeros((n_rows, HG_WIDTH), F32)

        @pl.when(jnp.logical_not(safe))
        def _():
            ho_ref[...] = jnp.zeros((n_rows, HG_WIDTH), F32)
            hq_ref[...] = flat(q_ref)
            hv_ref[...] = flat(v_ref)
            rid = lax.broadcasted_iota(jnp.int32, (n_rows, 1), 0)

            def body(s, c):
                brow = hb_ref[pl.ds(s, 1), :]
                krow = hk_ref[pl.ds(s, 1), :]
                vrow = hv_ref[pl.ds(s, 1), :]
                w = hq_ref[...] * krow * jnp.exp(jnp.minimum(hb_ref[...] - brow, 0.0))
                same_chunk_end = (s // hg_q + 1) * hg_q
                w = jnp.where(jnp.logical_and(rid >= s, rid < same_chunk_end), w, 0.0)
                for h in range(HG_HEADS):
                    hl = slice(h * HG_KDIM, (h + 1) * HG_KDIM)
                    a = jnp.sum(w[:, hl], axis=-1, keepdims=True)
                    ho_ref[:, hl] = ho_ref[:, hl] + a * vrow[:, hl]
                return c

            lax.fori_loop(0, n_rows, body, 0)

        return safe

    def hg_chunk(j, r0, safe):
        q_ = hg_q
        rows = slice(r0, r0 + q_)
        brows = slice(j * t_rows + r0, j * t_rows + r0 + q_)
        qq = q_ref[j, rows, :]
        vv = v_ref[j, rows, :]
        bc = hb_ref[brows, :]
        kk = hk_ref[brows, :]
        blast = bc[q_ - 1:q_, :]
        mm = hqe_ref.dtype
        hqe_ref[brows, :] = (qq * jnp.exp(bc)).astype(mm)
        hkd_ref[brows, :] = (kk * jnp.exp(blast - bc)).astype(mm)
        hke_ref[brows, :] = (kk * jnp.exp(-bc)).astype(mm)
        eb = jnp.exp(blast)
        for h in range(HG_HEADS):
            hl = slice(h * HG_KDIM, (h + 1) * HG_KDIM)
            ebt_ref[j, h] = jnp.broadcast_to(eb[:, hl], (HG_KDIM, HG_KDIM)).T
        yield
        tri = lax.broadcasted_iota(jnp.int32, (q_, q_), 0) >= lax.broadcasted_iota(jnp.int32, (q_, q_), 1)
        for h in range(HG_HEADS):
            hl = slice(h * HG_KDIM, (h + 1) * HG_KDIM)
            qe = hqe_ref[brows, hl]
            att = jnp.where(tri, _bdot_nt(qe, hke_ref[brows, hl]), 0.0)
            intra = jnp.where(safe, _bdot(att, vv[:, hl]), ho_ref[brows, hl])
            sh = hg_ref[j, hl, :]
            oc_ref[j, rows, hl] = intra + _bdot(qe, sh)
            hg_ref[j, hl, :] = sh * ebt_ref[j, h] + _bdot_tn(hkd_ref[brows, hl], vv[:, hl])
            yield

    def ssd_seq(j):
        for c in range(t_rows // ssd_q):
            yield from ssd_chunk(j, c * ssd_q)

    def hg_seq(j, safe):
        for c in range(t_rows // hg_q):
            yield from hg_chunk(j, c * hg_q, safe)

    safe = hg_prepare()
    ssd_prepare()
    _interleave([t for j in range(nb) for t in (ssd_seq(j), hg_seq(j, safe))],
                MIX_STAGGER, MIX_STAGGER_MOD)


def _ltri(n, q):
    return jnp.asarray(np.kron(np.eye(n // q, dtype=np.float32), np.tril(np.ones((q, q), np.float32))), BF16)


def _head_selectors():
    col = np.zeros((3, LANE, SSD_HEADS * LANE), np.float32)
    head = np.zeros((2, LANE, SSD_INNER), np.float32)
    for h in range(SSD_HEADS):
        col[:, h, h * LANE:(h + 1) * LANE] = 1.0
        head[:, h, h * SSD_HEAD_DIM:(h + 1) * SSD_HEAD_DIM] = 1.0
    return (jnp.asarray(col.reshape(3 * LANE, -1), BF16), jnp.asarray(head.reshape(2 * LANE, -1), BF16))


def _mix_call(pre, states, l_in, l_out, prev, lp, nseq, seqlen, t_rows, n_valid, nb, ssd_q, hg_q):
    nchunk = seqlen // t_rows
    view = lambda a: a.reshape(nseq, seqlen, a.shape[-1])
    seq_in = [view(pre[k]) for k in ('xc', 'dt', 'q', 'lf', 'kk', 'v')]
    consts = (lp['ssd_arow'], lp['ssd_dexp'], _ltri(nb * t_rows, ssd_q), _ltri(nb * t_rows, hg_q),
              *_head_selectors())
    tile = lambda c: pl.BlockSpec((nb, t_rows, c), lambda i, k: (i, k, 0))
    st = lambda a, l: pl.BlockSpec((None, nb) + a.shape[2:], lambda i, k: (l, i, 0, 0))
    prev = () if prev is None else tuple(prev)
    n_in = len(seq_in) + len(states) + len(consts)
    mm = BF16 if ssd_q % (2 * SUBLANE) == 0 and hg_q % (2 * SUBLANE) == 0 else F32
    kern = functools.partial(_mix_kernel, t_rows, n_valid, nb, ssd_q, hg_q, len(prev))
    return pl.pallas_call(
        kern,
        grid=(nseq // nb, nchunk),
        in_specs=[tile(a.shape[-1]) for a in seq_in] + [st(a, l_in) for a in states]
        + [_const_spec(c.shape) for c in consts]
        + [pl.BlockSpec(memory_space=pl.ANY) for _ in prev],
        out_specs=[tile(SSD_INNER), tile(HG_WIDTH)] + [st(a, l_out) for a in states],
        out_shape=[jax.ShapeDtypeStruct((nseq, seqlen, SSD_INNER), F32),
                   jax.ShapeDtypeStruct((nseq, seqlen, HG_WIDTH), F32)]
        + [jax.ShapeDtypeStruct((DEPTH,) + a.shape[1:], F32) for a in states],
        input_output_aliases={n_in + k: 2 + k for k in range(len(prev))},
        scratch_shapes=[pltpu.VMEM((nb * t_rows, HG_WIDTH), F32) for _ in range(5)]
        + [pltpu.VMEM((nb * t_rows, HG_WIDTH), mm) for _ in range(3)]
        + [pltpu.VMEM((nb, HG_HEADS, HG_KDIM, HG_KDIM), F32),
           pltpu.VMEM((nb, SSD_GROUPS, ssd_q, ssd_q), F32),
           pltpu.VMEM((nb, SSD_HEADS, ssd_q, ssd_q), mm),
           pltpu.VMEM((nb, SSD_HEADS, ssd_q, SSD_STATE), mm),
           pltpu.VMEM((nb * t_rows, SSD_INNER), mm),
           pltpu.VMEM((nb * t_rows, SSD_HEADS * LANE), F32),
           pltpu.VMEM((nb * t_rows, SSD_HEADS * LANE), F32),
           pltpu.VMEM((nb * t_rows, LANE), F32),
           pltpu.VMEM((nb * t_rows, LANE), F32)],
        compiler_params=_params(("parallel", "arbitrary")),
        name="ssd_hgrn_mix",
    )(*seq_in, *states, *consts, *prev)


def _post_kernel(x_ref, ya_ref, yb_ref, oc_ref, n1_ref, win_ref,
                 wglu_ref, ssdnw_ref, wssd_ref, hgnw_ref, whg_ref, wo_ref, x1_ref, mix_ref):
    x = x_ref[...]
    hb = _rms(x, n1_ref[...]).astype(BF16)
    proj = lambda w: jnp.dot(hb, w, preferred_element_type=F32)
    seg = lambda name: win_ref[:, W_IN_OFF[name]:W_IN_OFF[name] + W_IN_WIDTH[name]]
    ya = ya_ref[...]
    cdf = 0.5 * (1.0 + jnp.tanh(math.sqrt(2.0 / math.pi) * (ya + 0.044715 * (ya * ya * ya))))
    ga = (ya * cdf).astype(BF16)
    z = proj(seg('z'))
    nb_ = _rms(yb_ref[...] * _silu(z), ssdnw_ref[...]).astype(BF16)
    oc = oc_ref[...]
    normed = []
    for h in range(HG_HEADS):
        hl = slice(h * HG_VDIM, (h + 1) * HG_VDIM)
        normed.append(_rms(oc[:, hl], hgnw_ref[:, hl]))
    g = proj(seg('g'))
    nc = (jnp.concatenate(normed, axis=-1) * _silu(g)).astype(BF16)
    for j in range(D_MODEL // POST_BLK):
        cols = slice(j * POST_BLK, (j + 1) * POST_BLK)
        dot = lambda a, w_ref, off: jnp.dot(a, w_ref[:, off + j * POST_BLK:off + (j + 1) * POST_BLK],
                                            preferred_element_type=F32)
        out_a = dot(ga, wglu_ref, 0) * _sigmoid(dot(ga, wglu_ref, D_MODEL))
        out_b = dot(nb_, wssd_ref, 0)
        out_c = dot(nc, whg_ref, 0)
        gt = W_IN_OFF['gt']
        mixed = (_sigmoid(dot(hb, win_ref, gt)) * out_a + _sigmoid(dot(hb, win_ref, gt + D_MODEL)) * out_b
                 + _sigmoid(dot(hb, win_ref, gt + 2 * D_MODEL)) * out_c)
        mix_ref[:, cols] = mixed.astype(BF16)
    x1_ref[...] = x + jnp.dot(mix_ref[...], wo_ref[...], preferred_element_type=F32)


def _post_call(x2d, ya, yb, oc, lp):
    rows = x2d.shape[0]
    tm = BIG_TM
    seq_in = (x2d, ya, yb, oc)
    consts = (lp['n1'], lp['w_in'],
              lp['w_glu'], lp['ssd_nw'], lp['w_ssd'], lp['hg_nw'], lp['w_hg'], lp['w_o'])
    row = lambda c: pl.BlockSpec((tm, c), lambda i: (i, 0))
    return pl.pallas_call(
        _post_kernel,
        grid=(rows // tm,),
        in_specs=[row(a.shape[-1]) for a in seq_in] + [_wspec(c) for c in consts],
        out_specs=row(D_MODEL),
        out_shape=jax.ShapeDtypeStruct((rows, D_MODEL), F32),
        scratch_shapes=[pltpu.VMEM((tm, D_MODEL), BF16)],
        compiler_params=_params(("parallel",)),
        name="post_merge",
    )(*seq_in, *[_warg(c) for c in consts])


def _ffn_kernel(final, seq_rows, tm, x_ref, *rest):
    if seq_rows:
        (sp1_ref, sp2_ref, n2_ref, wup_ref, cw_ref, cb_ref, wdn_ref, nf_ref,
         out_ref, aff_ref, hm_ref) = rest
        t = lax.broadcasted_iota(jnp.int32, (tm, 1), 0) % seq_rows
    else:
        (n2_ref, wup_ref, cw_ref, cb_ref, wdn_ref, nf_ref, out_ref, aff_ref, hm_ref, carry_ref) = rest

        @pl.when(pl.program_id(1) == 0)
        def _():
            carry_ref[...] = jnp.zeros_like(carry_ref)
    tail = FFN_CONV - 1
    hr = tm // ROW_CHAINS

    def chain(c):
        rows = slice(c * hr, (c + 1) * hr)
        x = x_ref[rows, :]
        hb = _rms(x, n2_ref[...]).astype(BF16)
        for j in range(D_FF // FFN_BLK):
            cols = slice(j * FFN_BLK, (j + 1) * FFN_BLK)
            a = jnp.dot(hb, wup_ref[:, cols], preferred_element_type=F32)
            g = jnp.dot(hb, wup_ref[:, D_FF + j * FFN_BLK:D_FF + (j + 1) * FFN_BLK],
                        preferred_element_type=F32)
            p1 = pltpu.roll(a, 1, axis=0)
            p2 = pltpu.roll(a, 2, axis=0)
            if seq_rows:
                p1 = jnp.where(t[rows, :] < 1, sp1_ref[rows, cols], p1)
                p2 = jnp.where(t[rows, :] < 2, sp2_ref[rows, cols], p2)
                aff_ref[rows, cols] = a
            else:
                head = jnp.concatenate([carry_ref[:, cols], a[0:SUBLANE, :]], axis=0)
                p1 = jnp.concatenate([pltpu.roll(head, 1, axis=0)[SUBLANE:], p1[SUBLANE:, :]], axis=0)
                p2 = jnp.concatenate([pltpu.roll(head, 2, axis=0)[SUBLANE:], p2[SUBLANE:, :]], axis=0)
                carry_ref[:, cols] = a[hr - SUBLANE:hr, :]
                if c == ROW_CHAINS - 1:
                    aff_ref[0, :, cols] = a[hr - tail:hr, :]
            ac = (cw_ref[0:1, cols] * p2 + cw_ref[1:2, cols] * p1 + cw_ref[2:3, cols] * a
                  + cb_ref[:, cols])
            hm_ref[rows, cols] = (_silu(ac) * g).astype(BF16)
            yield
        for j in range(D_MODEL // FFN_BLK):
            cols = slice(j * FFN_BLK, (j + 1) * FFN_BLK)
            out_ref[rows, cols] = x[:, cols] + jnp.dot(hm_ref[rows, :], wdn_ref[:, cols],
                                                       preferred_element_type=F32)
            yield
        if final:
            out_ref[rows, :] = _rms(out_ref[rows, :], nf_ref[...])

    _interleave([chain(c) for c in range(ROW_CHAINS)], FFN_STAGGER)


def _ffn_prompt_call(x1, lp, nf, final, nseq, seqlen):
    tm = FFN_TM
    nchunk = seqlen // tm
    consts = (lp['n2'], lp['w_up'], lp['ffn_cw'], lp['ffn_cb'], lp['w_dn'], nf)
    row = pl.BlockSpec((tm, D_MODEL), lambda b, c: (b * nchunk + c, 0))
    return pl.pallas_call(
        functools.partial(_ffn_kernel, final, 0, tm),
        grid=(nseq, nchunk),
        in_specs=[row] + [_wspec(c) for c in consts],
        out_specs=[row, pl.BlockSpec((1, FFN_CONV - 1, D_FF), lambda b, c: (b, 0, 0))],
        out_shape=[jax.ShapeDtypeStruct(x1.shape, F32),
                   jax.ShapeDtypeStruct((nseq, FFN_CONV - 1, D_FF), F32)],
        scratch_shapes=[pltpu.VMEM((tm, D_FF), BF16), pltpu.VMEM((SUBLANE, D_FF), F32)],
        compiler_params=_params(("parallel", "arbitrary")),
        name="ffn_prompt",
    )(x1, *[_warg(c) for c in consts])


def _ffn_sample_call(x1, sp1, sp2, lp, nf, final):
    rows = x1.shape[0]
    tm = SMALL_TM
    consts = (lp['n2'], lp['w_up'], lp['ffn_cw'], lp['ffn_cb'], lp['w_dn'], nf)
    row = lambda c: pl.BlockSpec((tm, c), lambda i: (i, 0))
    return pl.pallas_call(
        functools.partial(_ffn_kernel, final, SAMPLE_PAD, tm),
        grid=(rows // tm,),
        in_specs=[row(D_MODEL), row(D_FF), row(D_FF)] + [_wspec(c) for c in consts],
        out_specs=[row(D_MODEL), row(D_FF)],
        out_shape=[jax.ShapeDtypeStruct(x1.shape, F32), jax.ShapeDtypeStruct((rows, D_FF), F32)],
        scratch_shapes=[pltpu.VMEM((tm, D_FF), BF16)],
        compiler_params=_params(("arbitrary",)),
        name="ffn_sample",
    )(x1, sp1, sp2, *[_warg(c) for c in consts])


def _layer_params(l, lb, norm1_w, w_in, s5_log_dt, s5_lambda_re, s5_lambda_im, s5_b_re, s5_b_im,
                  s5_c_re, s5_c_im, s5_d, s5_w_glu, ssd_conv_w, ssd_conv_b, ssd_dt_bias, ssd_a_log,
                  ssd_d, ssd_norm_w, ssd_w_out, hg_norm_w, hg_w_out, w_o, norm2_w, ffn_w_up,
                  ffn_conv_w, ffn_conv_b, ffn_w_down):
    row = lambda v: v.astype(F32).reshape(1, -1)
    lp = {}
    lp['n1'] = row(norm1_w[l])
    dt_end = W_IN_OFF['dt'] + SSD_HEADS
    w_in_al = jnp.concatenate(
        [w_in[..., :dt_end], jnp.zeros(w_in.shape[:-1] + (LANE - SSD_HEADS,), w_in.dtype), w_in[..., dt_end:]],
        axis=-1).astype(BF16)
    lp['w_in'] = _Layered(w_in_al, l)
    lp['dtb'] = jnp.pad(row(ssd_dt_bias[l]), ((0, 0), (0, LANE - SSD_HEADS)))
    lp['loglb'] = jnp.log(row(lb))
    lp['log1mlb'] = jnp.log1p(-row(lb))
    lp['omlb'] = 1.0 - row(lb)

    delta = jnp.exp(s5_log_dt[l].astype(F32))[:, None]
    lr = s5_lambda_re[l].astype(F32)
    li = s5_lambda_im[l].astype(F32)
    mag = jnp.exp(lr * delta)
    ab_re = mag * jnp.cos(li * delta)
    ab_im = mag * jnp.sin(li * delta)
    den = lr * lr + li * li
    nr = ab_re - 1.0
    co_re = (nr * lr + ab_im * li) / den
    co_im = (ab_im * lr - nr * li) / den
    br = s5_b_re[l].astype(F32)
    bi = s5_b_im[l].astype(F32)
    bb_re = co_re[..., None] * br - co_im[..., None] * bi
    bb_im = co_re[..., None] * bi + co_im[..., None] * br
    eye = jnp.eye(S5_GROUPS, dtype=F32)
    bd_in = lambda m: jnp.einsum('gnj,gh->gjhn', m, eye).reshape(S5_WIDTH, S5_LANES)
    bd_out = lambda m: jnp.einsum('gjn,gh->gnhj', m, eye).reshape(S5_LANES, S5_WIDTH)
    lp['s5_bb'] = jnp.concatenate([bd_in(bb_re), bd_in(bb_im)], axis=1).astype(BF16)
    lp['s5_cc'] = jnp.concatenate([bd_out(s5_c_re[l].astype(F32)),
                                   -bd_out(s5_c_im[l].astype(F32))], axis=0).astype(BF16)
    pw = jnp.asarray(list(range(1, S5_R + 1)) + [2 * S5_R, 4 * S5_R], F32)[:, None, None]
    pm = jnp.exp(lr * delta * pw)
    ph = li * delta * pw
    pad = ((0, 40 - (S5_R + 2)), (0, 0))
    lp['s5_apr'] = jnp.pad((pm * jnp.cos(ph)).reshape(-1, S5_LANES), pad)
    lp['s5_api'] = jnp.pad((pm * jnp.sin(ph)).reshape(-1, S5_LANES), pad)
    lp['s5_apr8'] = jnp.repeat(lp['s5_apr'][:S5_R + 2], S5_CH, axis=0)
    lp['s5_api8'] = jnp.repeat(lp['s5_api'][:S5_R + 2], S5_CH, axis=0)
    npair = S5_LANES // LANE
    pair_cols = lambda m_re, m_im: jnp.stack(
        [m_re.reshape(-1, npair, LANE), m_im.reshape(-1, npair, LANE)], axis=2).reshape(-1, 2 * S5_LANES)
    bb_int = pair_cols(bd_in(bb_re), bd_in(bb_im))
    cc_int = pair_cols(bd_out(s5_c_re[l].astype(F32)).T, -bd_out(s5_c_im[l].astype(F32)).T).T
    kw = 2 * LANE
    nlc = 2 * S5_LANES // S5_LC
    ktile = lambda lc: lc * S5_LC // (2 * S5_LANES // (S5_WIDTH // kw))
    lp['s5_bbc'] = jnp.stack([bb_int[ktile(lc) * kw:(ktile(lc) + 1) * kw, lc * S5_LC:(lc + 1) * S5_LC]
                              for lc in range(nlc)]).astype(BF16)
    lp['s5_ccc'] = jnp.stack([cc_int[lc * S5_LC:(lc + 1) * S5_LC, ktile(lc) * kw:(ktile(lc) + 1) * kw]
                              for lc in range(nlc)]).astype(BF16)
    lp['s5_d'] = row(s5_d[l])
    lp['w_glu'] = _Layered(s5_w_glu.astype(BF16), l)

    lp['conv_w'] = ssd_conv_w[l].astype(F32)
    lp['conv_b'] = row(ssd_conv_b[l])
    lp['ssd_arow'] = jnp.pad(-jnp.exp(row(ssd_a_log[l])), ((0, 0), (0, LANE - SSD_HEADS)))
    lp['ssd_dexp'] = jnp.repeat(row(ssd_d[l]), SSD_HEAD_DIM, axis=1)
    lp['ssd_nw'] = row(ssd_norm_w[l])
    lp['w_ssd'] = _Layered(ssd_w_out.astype(BF16), l)
    lp['hg_nw'] = row(hg_norm_w[l])
    lp['w_hg'] = _Layered(hg_w_out.astype(BF16), l)
    lp['w_o'] = _Layered(w_o.astype(BF16), l)
    lp['n2'] = row(norm2_w[l])
    lp['w_up'] = _Layered(ffn_w_up.astype(BF16), l)
    lp['ffn_cw'] = ffn_conv_w[l].astype(F32)
    lp['ffn_cb'] = row(ffn_conv_b[l])
    lp['w_dn'] = _Layered(ffn_w_down.astype(BF16), l)
    return lp


_PRE_KEYS = ('ua', 'xc', 'dt', 'q', 'lf', 'kk', 'v')


def kernel(x_prompt, x_sample, state_s5_re, state_s5_im, state_ssd, state_ssd_conv, state_hgrn, state_ffn_conv, norm1_w, w_in, s5_log_dt, s5_lambda_re, s5_lambda_im, s5_b_re, s5_b_im, s5_c_re, s5_c_im, s5_d, s5_w_glu, ssd_conv_w, ssd_conv_b, ssd_dt_bias, ssd_a_log, ssd_d, ssd_norm_w, ssd_w_out, hg_lb_logits, hg_norm_w, hg_w_out, w_o, norm2_w, ffn_w_up, ffn_conv_w, ffn_conv_b, ffn_w_down, norm_f_w):
    nb_p, len_p, _ = x_prompt.shape
    nb_s, len_s, _ = x_sample.shape
    assert len_p % S5_T == 0 and len_p % MIX_T == 0 and len_p % SMALL_TM == 0 and len_p % FFN_TM == 0
    assert len_s <= SAMPLE_PAD and len_s >= SSD_CONV - 1
    assert (nb_s * SAMPLE_PAD) % SMALL_TM == 0 and (nb_s * SAMPLE_PAD) % BIG_TM == 0
    lb_cum = jnp.cumsum(jax.nn.softmax(hg_lb_logits.astype(F32), axis=0), axis=0)
    lb_all = lb_cum - lb_cum[0:1]
    nf = norm_f_w.astype(F32).reshape(1, -1)

    xp = x_prompt.astype(F32).reshape(nb_p * len_p, D_MODEL)
    xs = jnp.pad(x_sample.astype(F32), ((0, 0), (0, SAMPLE_PAD - len_s), (0, 0)))
    xs = xs.reshape(nb_s * SAMPLE_PAD, D_MODEL)
    zeros_p = (jnp.zeros((1, nb_p, SSD_INNER, SSD_STATE), F32),
               jnp.zeros((1, nb_p, HG_WIDTH, HG_VDIM), F32))
    states_s = (state_ssd.astype(F32).reshape(DEPTH, nb_s, SSD_INNER, SSD_STATE),
                state_hgrn.astype(F32).reshape(DEPTH, nb_s, HG_WIDTH, HG_VDIM))
    mix_p = None
    mix_s = None
    new_p = []
    new_s = []
    pad_t = lambda rows: jnp.concatenate(
        rows + [jnp.zeros((nb_s, SAMPLE_PAD - len(rows), rows[0].shape[-1]), F32)], axis=1
    ).reshape(nb_s * SAMPLE_PAD, rows[0].shape[-1])
    for l in range(DEPTH):
        lp = _layer_params(l, lb_all[l], norm1_w, w_in, s5_log_dt, s5_lambda_re, s5_lambda_im,
                           s5_b_re, s5_b_im, s5_c_re, s5_c_im, s5_d, s5_w_glu, ssd_conv_w,
                           ssd_conv_b, ssd_dt_bias, ssd_a_log, ssd_d, ssd_norm_w, ssd_w_out,
                           hg_norm_w, hg_w_out, w_o, norm2_w, ffn_w_up, ffn_conv_w, ffn_conv_b,
                           ffn_w_down)
        final = l == DEPTH - 1

        *pre, p_conv = _pre_prompt_call(xp, lp, nb_p, len_p)
        pre = dict(zip(_PRE_KEYS, pre))
        ya, p_re, p_im = _s5_prompt_call(pre['ua'], lp, nb_p, len_p)
        yb, oc, *mix_p = _mix_call(pre, zeros_p, 0, l, mix_p, lp, nb_p, len_p, MIX_T, MIX_T,
                                   MIX_NB_PROMPT, SSD_Q, HG_Q)
        x1 = _post_call(xp, ya, yb.reshape(-1, SSD_INNER), oc.reshape(-1, HG_WIDTH), lp)
        xp, p_ffn = _ffn_prompt_call(x1, lp, nf, final, nb_p, len_p)
        new_p.append((p_re.reshape(nb_p, S5_GROUPS, S5_STATE), p_im.reshape(nb_p, S5_GROUPS, S5_STATE),
                      p_conv, p_ffn))

        cs = state_ssd_conv[l].astype(F32)
        r = lambda i: cs[:, i:i + 1]
        *pre, raw = _pre_sample_call(xs, pad_t([r(2)]), pad_t([r(1), r(2)]), pad_t([r(0), r(1), r(2)]), lp)
        pre = dict(zip(_PRE_KEYS, pre))
        s_conv = raw.reshape(nb_s, SAMPLE_PAD, SSD_CONV_DIM)[:, len_s - (SSD_CONV - 1):len_s]
        ua_tm = pre['ua'].reshape(nb_s, SAMPLE_PAD, S5_WIDTH)[:, :len_s].transpose(1, 0, 2)
        ya_tm, s_re, s_im = _s5_sample_call(
            ua_tm.reshape(len_s * nb_s, S5_WIDTH),
            state_s5_re[l].astype(F32).reshape(nb_s, S5_LANES),
            state_s5_im[l].astype(F32).reshape(nb_s, S5_LANES), lp, len_s, nb_s)
        ya = jnp.pad(ya_tm.reshape(len_s, nb_s, S5_WIDTH).transpose(1, 0, 2),
                     ((0, 0), (0, SAMPLE_PAD - len_s), (0, 0))).reshape(nb_s * SAMPLE_PAD, S5_WIDTH)
        yb, oc, *mix_s = _mix_call(pre, states_s, l, l, mix_s, lp, nb_s, SAMPLE_PAD, SAMPLE_PAD, len_s,
                                   MIX_NB_SAMPLE, SAMPLE_PAD, SAMPLE_PAD)
        x1 = _post_call(xs, ya, yb.reshape(-1, SSD_INNER), oc.reshape(-1, HG_WIDTH), lp)
        st = state_ffn_conv[l].astype(F32)
        f = lambda i: st[:, i:i + 1]
        xs, aff = _ffn_sample_call(x1, pad_t([f(1)]), pad_t([f(0), f(1)]), lp, nf, final)
        s_ffn = aff.reshape(nb_s, SAMPLE_PAD, D_FF)[:, len_s - (FFN_CONV - 1):len_s]
        new_s.append((s_re.reshape(nb_s, S5_GROUPS, S5_STATE), s_im.reshape(nb_s, S5_GROUPS, S5_STATE),
                      s_conv, s_ffn))

    stk = lambda lst, i: jnp.stack([s[i] for s in lst])
    y_prompt = xp.reshape(nb_p, len_p, D_MODEL)
    y_sample = xs.reshape(nb_s, SAMPLE_PAD, D_MODEL)[:, :len_s]

    def mix_states(m, n):
        ssd, hg = m
        return (ssd.reshape(DEPTH, n, SSD_HEADS, SSD_HEAD_DIM, SSD_STATE),
                hg.reshape(DEPTH, n, HG_HEADS, HG_KDIM, HG_VDIM))

    p_ssd, p_hg = mix_states(mix_p, nb_p)
    s_ssd, s_hg = mix_states(mix_s, nb_s)
    return (y_prompt, y_sample,
            stk(new_p, 0), stk(new_p, 1), p_ssd, stk(new_p, 2), p_hg, stk(new_p, 3),
            stk(new_s, 0), stk(new_s, 1), s_ssd, stk(new_s, 2), s_hg, stk(new_s, 3))
```
